```python
import math
import jax
import jax.numpy as jnp
from jax import lax
import numpy as np

D_MODEL = 1024
BATCH = 1
SEQ = 16384
DEPTH = 1
DEC_BATCH = 16
DEC_SEQ = 32
PAST_LEN = 2048

CHUNK = 64
Q_BLOCK = 128
SB_HEADS = 16
SB_HEAD_DIM = 64
SB_WIDTH = SB_HEADS * SB_HEAD_DIM
SSM_INNER = 2 * D_MODEL
SSM_HEAD_DIM = 64
SSM_HEADS = SSM_INNER // SSM_HEAD_DIM
SSM_GROUPS = 8
SSM_HPG = SSM_HEADS // SSM_GROUPS
SSM_STATE = 128
CONV_WIDTH = 4
CONV_CH = SSM_INNER + 2 * SSM_GROUPS * SSM_STATE
D_FF = 4 * D_MODEL
IN_PROJ_WIDTH = 3 * SB_WIDTH + SSM_INNER + CONV_CH + SSM_HEADS + 2 * D_MODEL
ALPHA = (2 * DEPTH) ** 0.25
BETA = (8 * DEPTH) ** -0.25
LN_EPS = 1e-5

kernel_name = 'hybrid_stickbreak_ssd_stream_step'


def _in_offsets():
    sizes = [SB_WIDTH, SB_WIDTH, SB_WIDTH, SSM_INNER, CONV_CH, SSM_HEADS, D_MODEL, D_MODEL]
    return [int(o) for o in np.cumsum(sizes)[:-1]]


def _layernorm(x, gain=None, bias=None):
    xf = x.astype(jnp.float32)
    mu = jnp.mean(xf, axis=-1, keepdims=True)
    var = jnp.mean(jnp.square(xf - mu), axis=-1, keepdims=True)
    out = (xf - mu) * lax.rsqrt(var + LN_EPS)
    if gain is not None:
        out = out * gain.astype(jnp.float32) + bias.astype(jnp.float32)
    return out


def _stick_breaking(q, k, v, q_pos, k_pos):
    z = jnp.einsum('bqhd,bkhd->bhqk', q, k).astype(jnp.float32) * (SB_HEAD_DIM ** -0.5)
    visible = k_pos[None, :] < q_pos[:, None]
    log_beta = jax.nn.log_sigmoid(z)
    log_keep = jnp.where(visible, jax.nn.log_sigmoid(-z), 0.0)
    between = lax.cumsum(log_keep, axis=3, reverse=True) - log_keep
    weights = jnp.where(visible, jnp.exp(log_beta + between), 0.0)
    return jnp.einsum('bhqk,bkhd->bqhd', weights.astype(v.dtype), v)


def _stick_breaking_prompt(q, k, v):
    b, t, h, d = q.shape
    nb = t // Q_BLOCK
    qb = jnp.moveaxis(q.reshape(b, nb, Q_BLOCK, h, d), 1, 0)
    k_pos = jnp.arange(t)

    def block(args):
        q_blk, i = args
        q_pos = i * Q_BLOCK + jnp.arange(Q_BLOCK)
        return _stick_breaking(q_blk, k, v, q_pos, k_pos)

    out = lax.map(block, (qb, jnp.arange(nb)))
    return jnp.moveaxis(out, 0, 1).reshape(b, t, h, d)


def _causal_conv(xpad, w, bias):
    out = lax.conv_general_dilated(
        xpad, w[:, None, :].astype(xpad.dtype), window_strides=(1,), padding='VALID',
        dimension_numbers=('NWC', 'WIO', 'NWC'), feature_group_count=xpad.shape[-1])
    return out + bias


def _ssd(x, dt, a_neg, bmat, cmat, h0, chunk):
    b, t, g, r, p = x.shape
    n = bmat.shape[-1]
    nc = t // chunk
    dt = dt.reshape(b, nc, chunk, g, r)
    xdt = x.reshape(b, nc, chunk, g, r, p) * dt[..., None]
    bmat = bmat.reshape(b, nc, chunk, g, n)
    cmat = cmat.reshape(b, nc, chunk, g, n)
    a_cs = jnp.cumsum(dt * a_neg, axis=2)
    causal = jnp.tril(jnp.ones((chunk, chunk), dtype=bool))
    seg = a_cs[:, :, :, None] - a_cs[:, :, None, :]
    decay = jnp.exp(jnp.where(causal[None, None, :, :, None, None], seg, -jnp.inf))
    cb = jnp.einsum('bclgn,bcsgn->bclsg', cmat, bmat)
    y_diag = jnp.einsum('bclsgr,bcsgrp->bclgrp', cb[..., None] * decay, xdt)
    to_end = jnp.exp(a_cs[:, :, -1:] - a_cs)
    chunk_states = jnp.einsum('bclgn,bclgrp->bcgrpn', bmat, xdt * to_end[..., None])
    chunk_decay = jnp.exp(a_cs[:, :, -1])

    def step(hs, inp):
        dec, st = inp
        return dec[..., None, None] * hs + st, hs

    h_final, h_enter = lax.scan(
        step, h0, (jnp.moveaxis(chunk_decay, 1, 0), jnp.moveaxis(chunk_states, 1, 0)))
    y_off = jnp.einsum('bclgn,cbgrpn->bclgrp', cmat, h_enter) * jnp.exp(a_cs)[..., None]
    return (y_diag + y_off).reshape(b, t, g, r, p), h_final


def _gated_rmsnorm(y, z, gain):
    gy = (y * jax.nn.silu(z)).astype(jnp.float32)
    shp = gy.shape
    gy = gy.reshape(*shp[:-1], SSM_GROUPS, SSM_INNER // SSM_GROUPS)
    gy = gy * lax.rsqrt(jnp.mean(jnp.square(gy), axis=-1, keepdims=True) + LN_EPS)
    return gy.reshape(shp) * gain.astype(jnp.float32)


def _layer(x, c, cache_k, cache_v, conv_prev, ssm_prev,
           w_ada, b_ada, w_in, conv_w, conv_b, dt_bias, a_log, d_skip, ssm_norm_w,
           w_branch_sb, w_branch_ssm, w_out, ln_attn_g, ln_attn_b, w_up, w_down, ln_ffn_g, ln_ffn_b):
    b, t, _ = x.shape
    mod = jax.nn.silu(c) @ w_ada + b_ada
    shift_a, scale_a, gate_a, shift_f, scale_f, gate_f = jnp.split(mod[:, None, :], 6, axis=-1)
    h = _layernorm(x) * (1.0 + scale_a) + shift_a
    u = h @ w_in
    q, k, v, z, xbc, dt_raw, g_sb, g_ssm = jnp.split(u, _in_offsets(), axis=-1)
    head_shape = (b, t, SB_HEADS, SB_HEAD_DIM)
    q, k, v = q.reshape(head_shape), k.reshape(head_shape), v.reshape(head_shape)
    if cache_k is None:
        o_sb = _stick_breaking_prompt(q, k, v)
    else:
        past = cache_k.shape[1]
        k_all = jnp.concatenate([cache_k.astype(k.dtype), k], axis=1)
        v_all = jnp.concatenate([cache_v.astype(v.dtype), v], axis=1)
        o_sb = _stick_breaking(q, k_all, v_all, past + jnp.arange(t), jnp.arange(past + t))
    xpad = jnp.concatenate([conv_prev.astype(xbc.dtype), xbc], axis=1)
    conv_new = xpad[:, -(CONV_WIDTH - 1):]
    xbc = jax.nn.silu(_causal_conv(xpad, conv_w, conv_b))
    xs, bm, cm = jnp.split(xbc, [SSM_INNER, SSM_INNER + SSM_GROUPS * SSM_STATE], axis=-1)
    xs = xs.reshape(b, t, SSM_GROUPS, SSM_HPG, SSM_HEAD_DIM)
    bm = bm.reshape(b, t, SSM_GROUPS, SSM_STATE)
    cm = cm.reshape(b, t, SSM_GROUPS, SSM_STATE)
    dt = jax.nn.softplus(dt_raw.astype(jnp.float32) + dt_bias.astype(jnp.float32))
    dt = dt.reshape(b, t, SSM_GROUPS, SSM_HPG)
    a_neg = -jnp.exp(a_log.astype(jnp.float32)).reshape(SSM_GROUPS, SSM_HPG)
    h0 = ssm_prev.astype(jnp.float32).reshape(b, SSM_GROUPS, SSM_HPG, SSM_HEAD_DIM, SSM_STATE)
    y, h_final = _ssd(xs, dt, a_neg, bm, cm, h0, min(CHUNK, t))
    y = y + d_skip.reshape(SSM_GROUPS, SSM_HPG)[..., None] * xs
    o_ssm = _gated_rmsnorm(y.reshape(b, t, SSM_INNER), z, ssm_norm_w)
    merged = (jax.nn.sigmoid(g_sb) * (o_sb.reshape(b, t, SB_WIDTH) @ w_branch_sb)
              + jax.nn.sigmoid(g_ssm) * (o_ssm @ w_branch_ssm))
    x1 = _layernorm(ALPHA * x + gate_a * (merged @ w_out), ln_attn_g, ln_attn_b)
    h2 = _layernorm(x1) * (1.0 + scale_f) + shift_f
    f = jnp.square(jax.nn.relu(h2 @ w_up)) @ w_down
    y_out = _layernorm(ALPHA * x1 + gate_f * f, ln_ffn_g, ln_ffn_b).astype(x.dtype)
    ssm_new = h_final.reshape(b, SSM_HEADS, SSM_HEAD_DIM, SSM_STATE)
    return (y_out, k.astype(x.dtype), v.astype(x.dtype), conv_new, ssm_new)


def setup_inputs(seed: int = 0) -> dict:
    key = jax.random.key(seed)
    ks = jax.random.split(key, 32)
    f32 = jnp.float32

    def nrm(k, shape, scale):
        return jax.random.normal(k, shape, f32) * scale

    L = DEPTH
    in_col_scale = jnp.ones((IN_PROJ_WIDTH,), f32).at[2 * SB_WIDTH:3 * SB_WIDTH].set(BETA)
    dt0 = jnp.exp(jax.random.uniform(ks[13], (L, SSM_HEADS), f32, math.log(1e-3), math.log(1e-1)))
    return {
        'x_prompt': nrm(ks[0], (BATCH, SEQ, D_MODEL), 1.0),
        'x_sample': nrm(ks[1], (DEC_BATCH, DEC_SEQ, D_MODEL), 1.0),
        'cache_sb_k': nrm(ks[2], (L, DEC_BATCH, PAST_LEN, SB_HEADS, SB_HEAD_DIM), 1.0),
        'cache_sb_v': nrm(ks[3], (L, DEC_BATCH, PAST_LEN, SB_HEADS, SB_HEAD_DIM), 1.0),
        'state_conv': nrm(ks[4], (L, DEC_BATCH, CONV_WIDTH - 1, CONV_CH), 1.0),
        'state_ssm': nrm(ks[5], (L, DEC_BATCH, SSM_HEADS, SSM_HEAD_DIM, SSM_STATE), 0.1),
        'c_prompt': nrm(ks[6], (BATCH, D_MODEL), 1.0),
        'c_sample': nrm(ks[7], (DEC_BATCH, D_MODEL), 1.0),
        'w_ada': nrm(ks[8], (L, D_MODEL, 6 * D_MODEL), D_MODEL ** -0.5),
        'b_ada': nrm(ks[9], (L, 6 * D_MODEL), 0.02),
        'w_in': nrm(ks[10], (L, D_MODEL, IN_PROJ_WIDTH), D_MODEL ** -0.5) * in_col_scale,
        'conv_w': nrm(ks[11], (L, CONV_WIDTH, CONV_CH), CONV_WIDTH ** -0.5),
        'conv_b': nrm(ks[12], (L, CONV_CH), 0.02),
        'dt_bias': dt0 + jnp.log(-jnp.expm1(-dt0)),
        'a_log': jnp.log(jax.random.uniform(ks[14], (L, SSM_HEADS), f32, 1.0, 16.0)),
        'd_skip': 1.0 + nrm(ks[15], (L, SSM_HEADS), 0.1),
        'ssm_norm_w': 1.0 + nrm(ks[16], (L, SSM_INNER), 0.02),
        'w_branch_sb': nrm(ks[17], (L, SB_WIDTH, D_MODEL), SB_WIDTH ** -0.5 * BETA),
        'w_branch_ssm': nrm(ks[18], (L, SSM_INNER, D_MODEL), SSM_INNER ** -0.5 * BETA),
        'w_out': nrm(ks[19], (L, D_MODEL, D_MODEL), D_MODEL ** -0.5 * BETA),
        'ln_attn_g': 1.0 + nrm(ks[20], (L, D_MODEL), 0.02),
        'ln_attn_b': nrm(ks[21], (L, D_MODEL), 0.02),
        'w_up': nrm(ks[22], (L, D_MODEL, D_FF), D_MODEL ** -0.5 * BETA),
        'w_down': nrm(ks[23], (L, D_FF, D_MODEL), D_FF ** -0.5 * BETA),
        'ln_ffn_g': 1.0 + nrm(ks[24], (L, D_MODEL), 0.02),
        'ln_ffn_b': nrm(ks[25], (L, D_MODEL), 0.02),
    }


def reference(x_prompt, x_sample, cache_sb_k, cache_sb_v, state_conv, state_ssm, c_prompt, c_sample,
              w_ada, b_ada, w_in, conv_w, conv_b, dt_bias, a_log, d_skip, ssm_norm_w,
              w_branch_sb, w_branch_ssm, w_out, ln_attn_g, ln_attn_b, w_up, w_down, ln_ffn_g, ln_ffn_b):
    n_p = x_prompt.shape[0]
    y_p, y_s = x_prompt, x_sample
    new_p = ([], [], [], [])
    new_s = ([], [], [], [])
    for l in range(DEPTH):
        lw = (w_ada[l], b_ada[l], w_in[l], conv_w[l], conv_b[l], dt_bias[l], a_log[l], d_skip[l],
              ssm_norm_w[l], w_branch_sb[l], w_branch_ssm[l], w_out[l], ln_attn_g[l], ln_attn_b[l],
              w_up[l], w_down[l], ln_ffn_g[l], ln_ffn_b[l])
        conv0 = jnp.zeros((n_p, CONV_WIDTH - 1, CONV_CH), x_prompt.dtype)
        ssm0 = jnp.zeros((n_p, SSM_HEADS, SSM_HEAD_DIM, SSM_STATE), jnp.float32)
        y_p, k_p, v_p, cv_p, s_p = _layer(y_p, c_prompt, None, None, conv0, ssm0, *lw)
        y_s, k_s, v_s, cv_s, s_s = _layer(y_s, c_sample, cache_sb_k[l], cache_sb_v[l],
                                          state_conv[l], state_ssm[l], *lw)
        for lst, arr in zip(new_p, (k_p, v_p, cv_p, s_p)):
            lst.append(arr)
        for lst, arr in zip(new_s, (k_s, v_s, cv_s, s_s)):
            lst.append(arr)
    return (y_p, y_s,
            jnp.stack(new_p[0]), jnp.stack(new_p[1]), jnp.stack(new_p[2]), jnp.stack(new_p[3]),
            jnp.stack(new_s[0]), jnp.stack(new_s[1]), jnp.stack(new_s[2]), jnp.stack(new_s[3]))
```

```python
import functools

import jax
import jax.numpy as jnp
from jax import lax
from jax.experimental import pallas as pl
from jax.experimental.pallas import tpu as pltpu

F32 = jnp.float32
BF16 = jnp.bfloat16

SB_HEAD_DIM = 64
SSM_HEAD_DIM = 64
SSM_STATE = 128
SSM_GROUPS = 8
CONV_WIDTH = 4
LN_EPS = 1e-5
LANES = 128
CONV_PAD_ROWS = 8
VMEM_LIMIT = 56 * 1024 * 1024


def _cparams(sem):
    return pltpu.CompilerParams(dimension_semantics=sem, vmem_limit_bytes=VMEM_LIMIT)


def _dot(a, b):
    return jnp.dot(a, b, preferred_element_type=F32)


def _dot_nt(a, b):
    return lax.dot_general(a, b, (((1,), (1,)), ((), ())), preferred_element_type=F32)


def _split_bf16(x, n):
    parts, r = [], x
    for _ in range(n):
        p = r.astype(BF16)
        parts.append(p)
        r = r - p.astype(F32)
    return parts


def _dot_exact_rhs(a_bf16, b_f32, n=3):
    out = None
    for p in _split_bf16(b_f32, n):
        t = _dot(a_bf16, p)
        out = t if out is None else out + t
    return out


def _dot_exact_lhs(a_f32, b_bf16, n):
    out = None
    for p in _split_bf16(a_f32, n):
        t = _dot(p, b_bf16)
        out = t if out is None else out + t
    return out


def _transpose_exact(eye_bf16, b_f32, n=3):
    out = None
    for p in _split_bf16(b_f32, n):
        t = _dot_nt(eye_bf16, p)
        out = t if out is None else out + t
    return out


def _softplus(x):
    return jnp.maximum(x, 0.0) + jnp.log(1.0 + jnp.exp(-jnp.abs(x)))


def _silu(x):
    return x * jax.nn.sigmoid(x)


def _ln(x):
    mu = jnp.mean(x, axis=-1, keepdims=True)
    xc = x - mu
    var = jnp.mean(xc * xc, axis=-1, keepdims=True)
    return xc * lax.rsqrt(var + LN_EPS)


def _eye(n, dtype):
    r = lax.broadcasted_iota(jnp.int32, (n, n), 0)
    c = lax.broadcasted_iota(jnp.int32, (n, n), 1)
    return jnp.where(r == c, 1.0, 0.0).astype(dtype)


def _row_spec(tm, width, rows):
    if rows == 1:
        return pl.BlockSpec((1, width), lambda i: (0, 0))
    return pl.BlockSpec((tm, width), lambda i: (i, 0))


def _mod_kernel(c_ref, w_ref, b_ref, o_ref):
    s = _silu(c_ref[...])
    s_hi, s_lo = _split_bf16(s, 2)
    w_hi, w_lo = _split_bf16(w_ref[...], 2)
    o_ref[...] = _dot(s_hi, w_hi) + _dot(s_hi, w_lo) + _dot(s_lo, w_hi) + b_ref[...]


def _mod_call(c, w, b):
    r, d = c.shape
    n = w.shape[1]
    tn = 1024
    return pl.pallas_call(
        _mod_kernel,
        grid=(n // tn,),
        in_specs=[pl.BlockSpec((r, d), lambda j: (0, 0)),
                  pl.BlockSpec((d, tn), lambda j: (0, j)),
                  pl.BlockSpec((1, tn), lambda j: (0, j))],
        out_specs=pl.BlockSpec((r, tn), lambda j: (0, j)),
        out_shape=jax.ShapeDtypeStruct((r, n), F32),
        compiler_params=_cparams(("arbitrary",)),
        name="ada_mod",
    )(c, w, b)


def _ln_mod_kernel(x_ref, sc_ref, sh_ref, o_ref):
    o_ref[...] = (_ln(x_ref[...]) * (1.0 + sc_ref[...]) + sh_ref[...]).astype(o_ref.dtype)


def _ln_mod_call(x, scale, shift):
    m, d = x.shape
    tm = min(m, 512)
    return pl.pallas_call(
        _ln_mod_kernel,
        grid=(m // tm,),
        in_specs=[pl.BlockSpec((tm, d), lambda i: (i, 0)),
                  _row_spec(tm, d, scale.shape[0]),
                  _row_spec(tm, d, shift.shape[0])],
        out_specs=pl.BlockSpec((tm, d), lambda i: (i, 0)),
        out_shape=jax.ShapeDtypeStruct((m, d), BF16),
        compiler_params=_cparams(("arbitrary",)),
        name="ln_mod",
    )(x, scale, shift)


def _matmul_kernel(x_ref, w_ref, *o_refs, scales):
    acc = _dot(x_ref[...], w_ref[...])
    for o_ref, s in zip(o_refs, scales):
        o_ref[...] = (acc if s == 1.0 else acc * s).astype(o_ref.dtype)


def _matmul_call(x, w, outs, name):
    m, k = x.shape
    n = w.shape[1]
    tm = min(m, 512)
    tn = min(n, 1024)
    res = pl.pallas_call(
        functools.partial(_matmul_kernel, scales=tuple(s for _, s in outs)),
        grid=(n // tn, m // tm),
        in_specs=[pl.BlockSpec((tm, k), lambda j, i: (i, 0)),
                  pl.BlockSpec((k, tn), lambda j, i: (0, j))],
        out_specs=[pl.BlockSpec((tm, tn), lambda j, i: (i, j)) for _ in outs],
        out_shape=[jax.ShapeDtypeStruct((m, n), dt) for dt, _ in outs],
        compiler_params=_cparams(("arbitrary", "arbitrary")),
        name=name,
    )(x, w)
    return res


def _sb_masks(tq, tk):
    lane = lax.broadcasted_iota(jnp.int32, (tq, LANES), 1)
    low = lane < SB_HEAD_DIM
    r = lax.broadcasted_iota(jnp.int32, (tk, tk), 0)
    c = lax.broadcasted_iota(jnp.int32, (tk, tk), 1)
    suffix = jnp.where(r >= c, 1.0, 0.0).astype(BF16)
    return low, suffix


def _sb_split_heads(q, low):
    qf = q.astype(F32)
    qa = jnp.where(low, qf, 0.0).astype(BF16)
    qb = jnp.where(low, 0.0, qf).astype(BF16)
    return qa, qb


def _sb_block(qs, kblk, vblk, carry, low, suffix, visible):
    c_a, c_b, acc = carry
    pvs, cs = [], []
    for qh, c in zip(qs, (c_a, c_b)):
        z = _dot_nt(qh, kblk)
        lk = -_softplus(z)
        if visible is not None:
            lk = jnp.where(visible, lk, 0.0)
        s = _dot_exact_lhs(lk, suffix, 2) + c
        w = jnp.exp(z + s)
        if visible is not None:
            w = jnp.where(visible, w, 0.0)
        pvs.append(_dot(w.astype(BF16), vblk))
        cs.append(s[:, 0:1])
    return cs[0], cs[1], acc + jnp.where(low, pvs[0], pvs[1])


def _sb_prompt_kernel(q_ref, k_ref, v_ref, o_ref, *, tq):
    i = pl.program_id(1)
    low, suffix = _sb_masks(tq, tq)
    qs = _sb_split_heads(q_ref[...], low)
    r = lax.broadcasted_iota(jnp.int32, (tq, tq), 0)
    c = lax.broadcasted_iota(jnp.int32, (tq, tq), 1)
    zero_c = jnp.zeros((tq, 1), F32)
    start = pl.multiple_of(i * tq, tq)
    carry = _sb_block(qs, k_ref[pl.ds(start, tq), :], v_ref[pl.ds(start, tq), :],
                      (zero_c, zero_c, jnp.zeros((tq, LANES), F32)), low, suffix, c < r)

    def body(n, carry):
        st = pl.multiple_of((i - 1 - n) * tq, tq)
        return _sb_block(qs, k_ref[pl.ds(st, tq), :], v_ref[pl.ds(st, tq), :], carry, low, suffix, None)

    _, _, acc = lax.fori_loop(0, i, body, carry)
    o_ref[...] = acc.astype(o_ref.dtype)


def _sb_prompt_call(q, k, v):
    t, w = q.shape
    tq = min(t, 256)
    return pl.pallas_call(
        functools.partial(_sb_prompt_kernel, tq=tq),
        grid=(w // LANES, t // tq),
        in_specs=[pl.BlockSpec((tq, LANES), lambda p, i: (i, p)),
                  pl.BlockSpec((t, LANES), lambda p, i: (0, p)),
                  pl.BlockSpec((t, LANES), lambda p, i: (0, p))],
        out_specs=pl.BlockSpec((tq, LANES), lambda p, i: (i, p)),
        out_shape=jax.ShapeDtypeStruct((t, w), BF16),
        compiler_params=_cparams(("arbitrary", "arbitrary")),
        name="sb_prompt",
    )(q, k, v)


def _sb_sample_kernel(q_ref, kn_ref, vn_ref, kc_ref, vc_ref, o_ref, *, tq, tk, past):
    low, suffix_new = _sb_masks(tq, tq)
    _, suffix = _sb_masks(tq, tk)
    qs = _sb_split_heads(q_ref[...], low)
    r = lax.broadcasted_iota(jnp.int32, (tq, tq), 0)
    c = lax.broadcasted_iota(jnp.int32, (tq, tq), 1)
    zero_c = jnp.zeros((tq, 1), F32)
    carry = _sb_block(qs, kn_ref[...], vn_ref[...],
                      (zero_c, zero_c, jnp.zeros((tq, LANES), F32)), low, suffix_new, c < r)
    nblk = past // tk

    def body(n, carry):
        st = pl.multiple_of((nblk - 1 - n) * tk, tk)
        kblk = kc_ref[0, pl.ds(st, tk), :].astype(BF16)
        vblk = vc_ref[0, pl.ds(st, tk), :].astype(BF16)
        return _sb_block(qs, kblk, vblk, carry, low, suffix, None)

    _, _, acc = lax.fori_loop(0, nblk, body, carry)
    o_ref[...] = acc.astype(o_ref.dtype)


def _sb_sample_call(q, kn, vn, kc, vc, tq):
    m, w = q.shape
    b, past, _ = kc.shape
    tk = min(past, 256)
    new_spec = pl.BlockSpec((tq, LANES), lambda bi, p: (bi, p))
    cache_spec = pl.BlockSpec((1, past, LANES), lambda bi, p: (bi, 0, p))
    return pl.pallas_call(
        functools.partial(_sb_sample_kernel, tq=tq, tk=tk, past=past),
        grid=(b, w // LANES),
        in_specs=[new_spec, new_spec, new_spec, cache_spec, cache_spec],
        out_specs=new_spec,
        out_shape=jax.ShapeDtypeStruct((m, w), BF16),
        compiler_params=_cparams(("arbitrary", "arbitrary")),
        name="sb_sample",
    )(q, kn, vn, kc, vc)


def _ssd_kernel(xbc_ref, z_ref, dt_ref, cprev_ref, h0_ref, cw_ref, cb_ref, dtb_ref, alog_ref, dsk_ref,
                nw_ref, o_ref, hout_ref, xpad_ref, st_ref, act_ref, y_ref, *, lc, inner, npairs):
    ci = pl.program_id(1)
    eye = _eye(LANES, BF16)

    @pl.when(ci == 0)
    def _():
        xpad_ref[0:CONV_PAD_ROWS, :] = cprev_ref[0]
        for p in range(npairs):
            st_ref[p] = _transpose_exact(eye, h0_ref[0, p])

    @pl.when(ci > 0)
    def _():
        xpad_ref[0:CONV_PAD_ROWS, :] = xpad_ref[lc:lc + CONV_PAD_ROWS, :]

    xpad_ref[CONV_PAD_ROWS:CONV_PAD_ROWS + lc, :] = xbc_ref[...]
    conv = cb_ref[...]
    first = CONV_PAD_ROWS - (CONV_WIDTH - 1)
    for w in range(CONV_WIDTH):
        conv = conv + cw_ref[w:w + 1, :] * xpad_ref[first + w:first + w + lc, :]
    act_ref[...] = _silu(conv)

    dtv = _softplus(dt_ref[...] + dtb_ref[...])
    a_neg = -jnp.exp(alog_ref[...])
    d_a = dtv * a_neg
    r = lax.broadcasted_iota(jnp.int32, (lc, lc), 0)
    c = lax.broadcasted_iota(jnp.int32, (lc, lc), 1)
    causal = r >= c
    prefix = jnp.where(causal, 1.0, 0.0).astype(BF16)
    a_cs = _dot_exact_rhs(prefix, d_a)
    a_cs_t = _transpose_exact(eye, a_cs)
    total = a_cs[lc - 1:lc, :]
    ea = jnp.exp(a_cs)
    to_end = jnp.exp(total - a_cs)
    chunk_decay = jnp.exp(total)
    low = lax.broadcasted_iota(jnp.int32, (lc, LANES), 1) < SSM_HEAD_DIM
    low1 = lax.broadcasted_iota(jnp.int32, (1, LANES), 1) < SSM_HEAD_DIM

    def pair_cols(arr, ha):
        return jnp.where(low, arr[:, ha:ha + 1], arr[:, ha + 1:ha + 2])

    pairs_per_group = npairs // SSM_GROUPS
    for g in range(SSM_GROUPS):
        b_bf = act_ref[:, inner + g * SSM_STATE:inner + (g + 1) * SSM_STATE].astype(BF16)
        c_off = inner + SSM_GROUPS * SSM_STATE
        c_bf = act_ref[:, c_off + g * SSM_STATE:c_off + (g + 1) * SSM_STATE].astype(BF16)
        cb = _dot_nt(c_bf, b_bf)
        b_t = _dot_nt(eye, b_bf).astype(BF16)
        for kk in range(pairs_per_group):
            p = g * pairs_per_group + kk
            ha = 2 * p
            x = act_ref[:, p * LANES:(p + 1) * LANES]
            xdt = x * pair_cols(dtv, ha)
            xdt_bf = xdt.astype(BF16)
            ys = []
            for h in (ha, ha + 1):
                seg = a_cs[:, h:h + 1] - a_cs_t[h:h + 1, :]
                decay = jnp.where(causal, jnp.exp(jnp.where(causal, seg, 0.0)), 0.0)
                ys.append(_dot((cb * decay).astype(BF16), xdt_bf))
            st = st_ref[p]
            y_off = _dot(c_bf, st.astype(BF16)) * pair_cols(ea, ha)
            y_ref[:, p * LANES:(p + 1) * LANES] = (
                jnp.where(low, ys[0], ys[1]) + y_off + dsk_ref[:, p * LANES:(p + 1) * LANES] * x)
            xw = (xdt * pair_cols(to_end, ha)).astype(BF16)
            dec = jnp.where(low1, chunk_decay[:, ha:ha + 1], chunk_decay[:, ha + 1:ha + 2])
            st_ref[p] = st * dec + _dot(b_t, xw)

    gy = y_ref[...] * _silu(z_ref[...])
    gw = inner // SSM_GROUPS
    for g in range(SSM_GROUPS):
        seg = gy[:, g * gw:(g + 1) * gw]
        ms = jnp.mean(seg * seg, axis=-1, keepdims=True)
        o_ref[:, g * gw:(g + 1) * gw] = (seg * lax.rsqrt(ms + LN_EPS) * nw_ref[:, g * gw:(g + 1) * gw]).astype(o_ref.dtype)

    @pl.when(ci == pl.num_programs(1) - 1)
    def _():
        for p in range(npairs):
            hout_ref[0, p] = _transpose_exact(eye, st_ref[p])


def _ssd_call(u, dtr, conv_prev, h0, cw, cb, dtb, alog, dsk, nw, *, batch, lc):
    m = u.shape[0]
    conv_ch = cw.shape[1]
    inner = nw.shape[1]
    npairs = h0.shape[1]
    nc = m // batch // lc
    zblk = conv_ch // inner
    row = lambda b, c: b * nc + c
    const = lambda b, c: (0, 0)
    return pl.pallas_call(
        functools.partial(_ssd_kernel, lc=lc, inner=inner, npairs=npairs),
        grid=(batch, nc),
        in_specs=[pl.BlockSpec((lc, conv_ch), lambda b, c: (row(b, c), 0)),
                  pl.BlockSpec((lc, inner), lambda b, c: (row(b, c), zblk)),
                  pl.BlockSpec((lc, LANES), lambda b, c: (row(b, c), 0)),
                  pl.BlockSpec((1, CONV_PAD_ROWS, conv_ch), lambda b, c: (b, 0, 0)),
                  pl.BlockSpec((1, npairs, LANES, LANES), lambda b, c: (b, 0, 0, 0)),
                  pl.BlockSpec((CONV_WIDTH, conv_ch), const),
                  pl.BlockSpec((1, conv_ch), const),
                  pl.BlockSpec((1, LANES), const),
                  pl.BlockSpec((1, LANES), const),
                  pl.BlockSpec((1, inner), const),
                  pl.BlockSpec((1, inner), const)],
        out_specs=[pl.BlockSpec((lc, inner), lambda b, c: (row(b, c), 0)),
                   pl.BlockSpec((1, npairs, LANES, LANES), lambda b, c: (b, 0, 0, 0))],
        out_shape=[jax.ShapeDtypeStruct((m, inner), BF16),
                   jax.ShapeDtypeStruct(h0.shape, F32)],
        scratch_shapes=[pltpu.VMEM((CONV_PAD_ROWS + lc, conv_ch), F32),
                        pltpu.VMEM((npairs, LANES, LANES), F32),
                        pltpu.VMEM((lc, conv_ch), F32),
                        pltpu.VMEM((lc, inner), F32)],
        compiler_params=_cparams(("arbitrary", "arbitrary")),
        name="ssd",
    )(u, u, dtr, conv_prev, h0, cw, cb, dtb, alog, dsk, nw)


def _merge_kernel(osb_ref, ossm_ref, gsb_ref, gssm_ref, x_ref, ga_ref, lg_ref, lb_ref,
                  wsb_ref, wssm_ref, wout_ref, o_ref, *, alpha):
    merged = (jax.nn.sigmoid(gsb_ref[...]) * _dot(osb_ref[...], wsb_ref[...])
              + jax.nn.sigmoid(gssm_ref[...]) * _dot(ossm_ref[...], wssm_ref[...]))
    res = alpha * x_ref[...] + ga_ref[...] * _dot(merged.astype(BF16), wout_ref[...])
    o_ref[...] = _ln(res) * lg_ref[...] + lb_ref[...]


def _resident(shape):
    return pl.BlockSpec(shape, lambda i: (0, 0), pipeline_mode=pl.Buffered(1))


def _merge_call(osb, ossm, u, x, gate_a, ln_g, ln_b, wsb, wssm, wout, *, alpha):
    m, d = x.shape
    sbw = osb.shape[1]
    inner = ossm.shape[1]
    tm = min(m, 512)
    gcol = (u.shape[1] - 2 * d) // d
    return pl.pallas_call(
        functools.partial(_merge_kernel, alpha=alpha),
        grid=(m // tm,),
        in_specs=[pl.BlockSpec((tm, sbw), lambda i: (i, 0)),
                  pl.BlockSpec((tm, inner), lambda i: (i, 0)),
                  pl.BlockSpec((tm, d), lambda i: (i, gcol)),
                  pl.BlockSpec((tm, d), lambda i: (i, gcol + 1)),
                  pl.BlockSpec((tm, d), lambda i: (i, 0)),
                  _row_spec(tm, d, gate_a.shape[0]),
                  pl.BlockSpec((1, d), lambda i: (0, 0)),
                  pl.BlockSpec((1, d), lambda i: (0, 0)),
                  _resident(wsb.shape), _resident(wssm.shape), _resident(wout.shape)],
        out_specs=pl.BlockSpec((tm, d), lambda i: (i, 0)),
        out_shape=jax.ShapeDtypeStruct((m, d), F32),
        compiler_params=_cparams(("arbitrary",)),
        name="merge",
    )(osb, ossm, u, u, x, gate_a, ln_g, ln_b, wsb, wssm, wout)


def _ffn_kernel(x1_ref, sc_ref, sh_ref, gf_ref, lg_ref, lb_ref, wup_ref, wdn_ref, o_ref, *, alpha, nchunk):
    x1 = x1_ref[...]
    h2 = (_ln(x1) * (1.0 + sc_ref[...]) + sh_ref[...]).astype(BF16)
    dff = wup_ref.shape[1]
    cw = dff // nchunk
    f = None
    for ck in range(nchunk):
        a = jnp.maximum(_dot(h2, wup_ref[:, ck * cw:(ck + 1) * cw]), 0.0)
        t = _dot((a * a).astype(BF16), wdn_ref[ck * cw:(ck + 1) * cw, :])
        f = t if f is None else f + t
    o_ref[...] = _ln(alpha * x1 + gf_ref[...] * f) * lg_ref[...] + lb_ref[...]


def _ffn_call(x1, scale, shift, gate, ln_g, ln_b, wup, wdn, *, alpha):
    m, d = x1.shape
    tm = min(m, 512)
    return pl.pallas_call(
        functools.partial(_ffn_kernel, alpha=alpha, nchunk=4),
        grid=(m // tm,),
        in_specs=[pl.BlockSpec((tm, d), lambda i: (i, 0)),
                  _row_spec(tm, d, scale.shape[0]),
                  _row_spec(tm, d, shift.shape[0]),
                  _row_spec(tm, d, gate.shape[0]),
                  pl.BlockSpec((1, d), lambda i: (0, 0)),
                  pl.BlockSpec((1, d), lambda i: (0, 0)),
                  _resident(wup.shape), _resident(wdn.shape)],
        out_specs=pl.BlockSpec((tm, d), lambda i: (i, 0)),
        out_shape=jax.ShapeDtypeStruct((m, d), F32),
        compiler_params=_cparams(("arbitrary",)),
        name="ffn",
    )(x1, scale, shift, gate, ln_g, ln_b, wup, wdn)


def _layer(x, mod, cache_k, cache_v, conv_prev, ssm_prev, wts, *, batch, alpha):
    m, d = x.shape
    t = m // batch
    shift_a, scale_a, gate_a, shift_f, scale_f, gate_f = [mod[:, i * d:(i + 1) * d] for i in range(6)]
    h = _ln_mod_call(x, scale_a, shift_a)
    (q,) = _matmul_call(h, wts["w_q"], [(BF16, SB_HEAD_DIM ** -0.5)], "proj_q")
    k32, kb = _matmul_call(h, wts["w_k"], [(F32, 1.0), (BF16, 1.0)], "proj_k")
    v32, vb = _matmul_call(h, wts["w_v"], [(F32, 1.0), (BF16, 1.0)], "proj_v")
    (u,) = _matmul_call(h, wts["w_rest"], [(F32, 1.0)], "proj_rest")
    (dtr,) = _matmul_call(h, wts["w_dt"], [(F32, 1.0)], "proj_dt")
    if cache_k is None:
        o_sb = _sb_prompt_call(q, kb, vb)
    else:
        o_sb = _sb_sample_call(q, kb, vb, cache_k, cache_v, t)
    conv_ch = wts["conv_w"].shape[1]
    cprev = jnp.pad(conv_prev, ((0, 0), (CONV_PAD_ROWS - (CONV_WIDTH - 1), 0), (0, 0)))
    npairs = ssm_prev.shape[1] // 2
    h0 = ssm_prev.reshape(batch, npairs, LANES, SSM_STATE)
    o_ssm, h_fin = _ssd_call(u, dtr, cprev, h0, wts["conv_w"], wts["conv_b"], wts["dt_bias"], wts["a_log"],
                             wts["d_skip"], wts["ssm_norm_w"], batch=batch, lc=min(t, 128))
    x1 = _merge_call(o_sb, o_ssm, u, x, gate_a, wts["ln_attn_g"], wts["ln_attn_b"],
                     wts["w_branch_sb"], wts["w_branch_ssm"], wts["w_out"], alpha=alpha)
    y = _ffn_call(x1, scale_f, shift_f, gate_f, wts["ln_ffn_g"], wts["ln_ffn_b"],
                  wts["w_up"], wts["w_down"], alpha=alpha)
    conv_new = u.reshape(batch, t, -1)[:, t - (CONV_WIDTH - 1):, :conv_ch]
    return y, k32, v32, conv_new, h_fin.reshape(ssm_prev.shape)


def _prep_weights(l, w_in, conv_w, conv_b, dt_bias, a_log, d_skip, ssm_norm_w, w_branch_sb, w_branch_ssm,
                  w_out, ln_attn_g, ln_attn_b, w_up, w_down, ln_ffn_g, ln_ffn_b, sbw):
    d = w_in.shape[1]
    inner = ssm_norm_w.shape[1]
    conv_ch = conv_w.shape[2]
    nh = a_log.shape[1]
    o_z = 3 * sbw
    o_xbc = o_z + inner
    o_dt = o_xbc + conv_ch
    o_g = o_dt + nh
    wi = w_in[l]
    pad_h = lambda v: jnp.pad(v, (0, LANES - nh))[None, :]
    return {
        "w_q": wi[:, :sbw].astype(BF16),
        "w_k": wi[:, sbw:2 * sbw].astype(BF16),
        "w_v": wi[:, 2 * sbw:3 * sbw].astype(BF16),
        "w_rest": jnp.concatenate([wi[:, o_xbc:o_dt], wi[:, o_z:o_xbc], wi[:, o_g:]], axis=1).astype(BF16),
        "w_dt": jnp.pad(wi[:, o_dt:o_g], ((0, 0), (0, LANES - nh))).astype(BF16),
        "conv_w": conv_w[l], "conv_b": conv_b[l][None, :],
        "dt_bias": pad_h(dt_bias[l]), "a_log": pad_h(a_log[l]),
        "d_skip": jnp.repeat(d_skip[l], SSM_HEAD_DIM)[None, :],
        "ssm_norm_w": ssm_norm_w[l][None, :],
        "w_branch_sb": w_branch_sb[l].astype(BF16), "w_branch_ssm": w_branch_ssm[l].astype(BF16),
        "w_out": w_out[l].astype(BF16),
        "ln_attn_g": ln_attn_g[l][None, :], "ln_attn_b": ln_attn_b[l][None, :],
        "w_up": w_up[l].astype(BF16), "w_down": w_down[l].astype(BF16),
        "ln_ffn_g": ln_ffn_g[l][None, :], "ln_ffn_b": ln_ffn_b[l][None, :],
    }


def kernel(x_prompt, x_sample, cache_sb_k, cache_sb_v, state_conv, state_ssm, c_prompt, c_sample, w_ada, b_ada, w_in, conv_w, conv_b, dt_bias, a_log, d_skip, ssm_norm_w, w_branch_sb, w_branch_ssm, w_out, ln_attn_g, ln_attn_b, w_up, w_down, ln_ffn_g, ln_ffn_b):
    depth = w_ada.shape[0]
    alpha = (2 * depth) ** 0.25
    n_p, t_p, d = x_prompt.shape
    n_s, t_s, _ = x_sample.shape
    past = cache_sb_k.shape[2]
    sbw = cache_sb_k.shape[3] * cache_sb_k.shape[4]
    conv_ch = conv_w.shape[2]
    y_p = x_prompt.reshape(n_p * t_p, d)
    y_s = x_sample.reshape(n_s * t_s, d)
    c_all = jnp.concatenate([c_prompt, c_sample], axis=0)
    rows = c_all.shape[0]
    c_all = jnp.pad(c_all, ((0, -rows % 8), (0, 0)))
    new_p, new_s = [], []
    for l in range(depth):
        wts = _prep_weights(l, w_in, conv_w, conv_b, dt_bias, a_log, d_skip, ssm_norm_w, w_branch_sb,
                            w_branch_ssm, w_out, ln_attn_g, ln_attn_b, w_up, w_down, ln_ffn_g, ln_ffn_b, sbw)
        mod = _mod_call(c_all, w_ada[l], b_ada[l][None, :])
        mod_p = mod[:n_p] if n_p == 1 else jnp.repeat(mod[:n_p], t_p, axis=0)
        mod_s = jnp.repeat(mod[n_p:n_p + n_s], t_s, axis=0)
        conv0 = jnp.zeros((n_p, CONV_WIDTH - 1, conv_ch), F32)
        ssm0 = jnp.zeros((n_p,) + state_ssm.shape[2:], F32)
        y_p, k_p, v_p, cv_p, s_p = _layer(y_p, mod_p, None, None, conv0, ssm0, wts, batch=n_p, alpha=alpha)
        y_s, k_s, v_s, cv_s, s_s = _layer(y_s, mod_s, cache_sb_k[l].reshape(n_s, past, sbw),
                                          cache_sb_v[l].reshape(n_s, past, sbw), state_conv[l], state_ssm[l],
                                          wts, batch=n_s, alpha=alpha)
        hshape = cache_sb_k.shape[3:]
        new_p.append((k_p.reshape((n_p, t_p) + hshape), v_p.reshape((n_p, t_p) + hshape), cv_p, s_p))
        new_s.append((k_s.reshape((n_s, t_s) + hshape), v_s.reshape((n_s, t_s) + hshape), cv_s, s_s))
    stack = lambda lst, i: jnp.stack([e[i] for e in lst])
    return (y_p.reshape(n_p, t_p, d), y_s.reshape(n_s, t_s, d),
            stack(new_p, 0), stack(new_p, 1), stack(new_p, 2), stack(new_p, 3),
            stack(new_s, 0), stack(new_s, 1), stack(new_s, 2), stack(new_s, 3))
```

```python
import functools

import jax
import jax.numpy as jnp
from jax import lax
from jax.experimental import pallas as pl
from jax.experimental.pallas import tpu as pltpu

F32 = jnp.float32
BF16 = jnp.bfloat16

SB_HEAD_DIM = 64
SSM_HEAD_DIM = 64
SSM_STATE = 128
SSM_GROUPS = 8
CONV_WIDTH = 4
LN_EPS = 1e-5
LANES = 128
CONV_PAD_ROWS = 8
VMEM_LIMIT = 56 * 1024 * 1024
EXP_UNDERFLOW = -104.0
NORM_SLACK = 1.001


def _cparams(sem):
    return pltpu.CompilerParams(dimension_semantics=sem, vmem_limit_bytes=VMEM_LIMIT)


def _dot(a, b):
    return jnp.dot(a, b, preferred_element_type=F32)


def _dot_nt(a, b):
    return lax.dot_general(a, b, (((1,), (1,)), ((), ())), preferred_element_type=F32)


def _split_bf16(x, n):
    parts, r = [], x
    for _ in range(n):
        p = r.astype(BF16)
        parts.append(p)
        r = r - p.astype(F32)
    return parts


def _dot_exact_rhs(a_bf16, b_f32, n=3):
    out = None
    for p in _split_bf16(b_f32, n):
        t = _dot(a_bf16, p)
        out = t if out is None else out + t
    return out


def _dot_exact_lhs(a_f32, b_bf16, n):
    out = None
    for p in _split_bf16(a_f32, n):
        t = _dot(p, b_bf16)
        out = t if out is None else out + t
    return out


def _transpose_exact(eye_bf16, b_f32, n=3):
    out = None
    for p in _split_bf16(b_f32, n):
        t = _dot_nt(eye_bf16, p)
        out = t if out is None else out + t
    return out


def _softplus(x):
    return jnp.maximum(x, 0.0) + jnp.log(1.0 + jnp.exp(-jnp.abs(x)))


def _silu(x):
    return x * jax.nn.sigmoid(x)


def _ln(x):
    mu = jnp.mean(x, axis=-1, keepdims=True)
    xc = x - mu
    var = jnp.mean(xc * xc, axis=-1, keepdims=True)
    return xc * lax.rsqrt(var + LN_EPS)


def _eye(n, dtype):
    r = lax.broadcasted_iota(jnp.int32, (n, n), 0)
    c = lax.broadcasted_iota(jnp.int32, (n, n), 1)
    return jnp.where(r == c, 1.0, 0.0).astype(dtype)


def _row_spec(tm, width, rows):
    if rows == 1:
        return pl.BlockSpec((1, width), lambda i: (0, 0))
    return pl.BlockSpec((tm, width), lambda i: (i, 0))


def _mod_kernel(c_ref, w_ref, b_ref, o_ref):
    s = _silu(c_ref[...])
    s_hi, s_lo = _split_bf16(s, 2)
    w_hi, w_lo = _split_bf16(w_ref[...], 2)
    o_ref[...] = _dot(s_hi, w_hi) + _dot(s_hi, w_lo) + _dot(s_lo, w_hi) + b_ref[...]


def _mod_call(c, w, b):
    r, d = c.shape
    n = w.shape[1]
    tn = 1024
    return pl.pallas_call(
        _mod_kernel,
        grid=(n // tn,),
        in_specs=[pl.BlockSpec((r, d), lambda j: (0, 0)),
                  pl.BlockSpec((d, tn), lambda j: (0, j)),
                  pl.BlockSpec((1, tn), lambda j: (0, j))],
        out_specs=pl.BlockSpec((r, tn), lambda j: (0, j)),
        out_shape=jax.ShapeDtypeStruct((r, n), F32),
        compiler_params=_cparams(("arbitrary",)),
        name="ada_mod",
    )(c, w, b)


def _ln_mod_kernel(x_ref, sc_ref, sh_ref, o_ref):
    o_ref[...] = (_ln(x_ref[...]) * (1.0 + sc_ref[...]) + sh_ref[...]).astype(o_ref.dtype)


def _ln_mod_call(x, scale, shift):
    m, d = x.shape
    tm = min(m, 512)
    return pl.pallas_call(
        _ln_mod_kernel,
        grid=(m // tm,),
        in_specs=[pl.BlockSpec((tm, d), lambda i: (i, 0)),
                  _row_spec(tm, d, scale.shape[0]),
                  _row_spec(tm, d, shift.shape[0])],
        out_specs=pl.BlockSpec((tm, d), lambda i: (i, 0)),
        out_shape=jax.ShapeDtypeStruct((m, d), BF16),
        compiler_params=_cparams(("arbitrary",)),
        name="ln_mod",
    )(x, scale, shift)


def _matmul_kernel(x_ref, w_ref, *o_refs, scales):
    acc = _dot(x_ref[...], w_ref[...])
    for o_ref, s in zip(o_refs, scales):
        o_ref[...] = (acc if s == 1.0 else acc * s).astype(o_ref.dtype)


def _matmul_call(x, w, outs, name):
    m, k = x.shape
    n = w.shape[1]
    tm = min(m, 512)
    tn = min(n, 1024)
    res = pl.pallas_call(
        functools.partial(_matmul_kernel, scales=tuple(s for _, s in outs)),
        grid=(n // tn, m // tm),
        in_specs=[pl.BlockSpec((tm, k), lambda j, i: (i, 0)),
                  pl.BlockSpec((k, tn), lambda j, i: (0, j))],
        out_specs=[pl.BlockSpec((tm, tn), lambda j, i: (i, j)) for _ in outs],
        out_shape=[jax.ShapeDtypeStruct((m, n), dt) for dt, _ in outs],
        compiler_params=_cparams(("arbitrary", "arbitrary")),
        name=name,
    )(x, w)
    return res


def _sb_masks(tq, tk):
    lane = lax.broadcasted_iota(jnp.int32, (tq, LANES), 1)
    low = lane < SB_HEAD_DIM
    r = lax.broadcasted_iota(jnp.int32, (tk, tk), 0)
    c = lax.broadcasted_iota(jnp.int32, (tk, tk), 1)
    suffix = jnp.where(r >= c, 1.0, 0.0).astype(BF16)
    return low, suffix


def _sb_split_heads(q, low):
    qf = q.astype(F32)
    qa = jnp.where(low, qf, 0.0).astype(BF16)
    qb = jnp.where(low, 0.0, qf).astype(BF16)
    return qa, qb


def _sb_block(qs, kblk, vblk, carry, low, suffix, visible):
    c_a, c_b, acc = carry
    pvs, cs = [], []
    for qh, c in zip(qs, (c_a, c_b)):
        z = _dot_nt(qh, kblk)
        lk = -_softplus(z)
        if visible is not None:
            lk = jnp.where(visible, lk, 0.0)
        s = _dot_exact_lhs(lk, suffix, 2) + c
        w = jnp.exp(z + s)
        if visible is not None:
            w = jnp.where(visible, w, 0.0)
        pvs.append(_dot(w.astype(BF16), vblk))
        cs.append(s[:, 0:1])
    return cs[0], cs[1], acc + jnp.where(low, pvs[0], pvs[1])


def _row_norm(x):
    return jnp.sqrt(jnp.sum(x * x, axis=-1, keepdims=True))


def _sb_live(c_a, c_b, zb_a, zb_b):
    return (jnp.max(jnp.maximum(c_a + zb_a, c_b + zb_b)) >= EXP_UNDERFLOW).astype(jnp.int32)


def _sb_prompt_kernel(q_ref, k_ref, v_ref, o_ref, kmax_ref, *, tq):
    i = pl.program_id(1)
    nkb = k_ref.shape[0] // tq

    @pl.when(i == 0)
    def _():
        def body(n, m):
            kb = k_ref[pl.ds(pl.multiple_of(n * tq, tq), tq), :].astype(F32)
            return jnp.maximum(m, jnp.max(_row_norm(kb), axis=0, keepdims=True))
        kmax_ref[...] = jnp.broadcast_to(lax.fori_loop(0, nkb, body, jnp.zeros((1, 1), F32)), kmax_ref.shape)

    low, suffix = _sb_masks(tq, tq)
    qs = _sb_split_heads(q_ref[...], low)
    kmax = kmax_ref[0:1, 0:1] * NORM_SLACK
    zb_a = _row_norm(qs[0].astype(F32)) * kmax
    zb_b = _row_norm(qs[1].astype(F32)) * kmax
    r = lax.broadcasted_iota(jnp.int32, (tq, tq), 0)
    c = lax.broadcasted_iota(jnp.int32, (tq, tq), 1)
    zero_c = jnp.zeros((tq, 1), F32)
    start = pl.multiple_of(i * tq, tq)
    carry = _sb_block(qs, k_ref[pl.ds(start, tq), :], v_ref[pl.ds(start, tq), :],
                      (zero_c, zero_c, jnp.zeros((tq, LANES), F32)), low, suffix, c < r)
    prev = pl.multiple_of(jnp.maximum(i - 1, 0) * tq, tq)
    c_a, c_b, acc = _sb_block(qs, k_ref[pl.ds(prev, tq), :], v_ref[pl.ds(prev, tq), :], carry, low, suffix,
                              r >= jnp.where(i > 0, 0, tq))

    def cond(st):
        return jnp.logical_and(st[0] >= 0, st[4] > 0)

    def body(st):
        j, c_a, c_b, acc, _ = st
        at = pl.multiple_of(j * tq, tq)
        c_a, c_b, acc = _sb_block(qs, k_ref[pl.ds(at, tq), :], v_ref[pl.ds(at, tq), :], (c_a, c_b, acc),
                                  low, suffix, None)
        return j - 1, c_a, c_b, acc, _sb_live(c_a, c_b, zb_a, zb_b)

    _, _, _, acc, _ = lax.while_loop(cond, body, (i - 2, c_a, c_b, acc, _sb_live(c_a, c_b, zb_a, zb_b)))
    o_ref[...] = acc.astype(o_ref.dtype)


def _sb_prompt_call(q, k, v):
    t, w = q.shape
    tq = min(t, 256)
    return pl.pallas_call(
        functools.partial(_sb_prompt_kernel, tq=tq),
        grid=(w // LANES, t // tq),
        in_specs=[pl.BlockSpec((tq, LANES), lambda p, i: (i, p)),
                  pl.BlockSpec((t, LANES), lambda p, i: (0, p)),
                  pl.BlockSpec((t, LANES), lambda p, i: (0, p))],
        out_specs=pl.BlockSpec((tq, LANES), lambda p, i: (i, p)),
        out_shape=jax.ShapeDtypeStruct((t, w), BF16),
        scratch_shapes=[pltpu.VMEM((8, LANES), F32)],
        compiler_params=_cparams(("arbitrary", "arbitrary")),
        name="sb_prompt",
    )(q, k, v)


def _sb_sample_kernel(q_ref, kn_ref, vn_ref, kc_ref, vc_ref, o_ref, *, tq, tk, past):
    low, suffix_new = _sb_masks(tq, tq)
    _, suffix = _sb_masks(tq, tk)
    qs = _sb_split_heads(q_ref[...], low)
    r = lax.broadcasted_iota(jnp.int32, (tq, tq), 0)
    c = lax.broadcasted_iota(jnp.int32, (tq, tq), 1)
    zero_c = jnp.zeros((tq, 1), F32)
    carry = _sb_block(qs, kn_ref[...], vn_ref[...],
                      (zero_c, zero_c, jnp.zeros((tq, LANES), F32)), low, suffix_new, c < r)
    nblk = past // tk

    def body(n, carry):
        st = pl.multiple_of((nblk - 1 - n) * tk, tk)
        kblk = kc_ref[0, pl.ds(st, tk), :].astype(BF16)
        vblk = vc_ref[0, pl.ds(st, tk), :].astype(BF16)
        return _sb_block(qs, kblk, vblk, carry, low, suffix, None)

    _, _, acc = lax.fori_loop(0, nblk, body, carry)
    o_ref[...] = acc.astype(o_ref.dtype)


def _sb_sample_call(q, kn, vn, kc, vc, tq):
    m, w = q.shape
    b, past, _ = kc.shape
    tk = min(past, 256)
    new_spec = pl.BlockSpec((tq, LANES), lambda bi, p: (bi, p))
    cache_spec = pl.BlockSpec((1, past, LANES), lambda bi, p: (bi, 0, p))
    return pl.pallas_call(
        functools.partial(_sb_sample_kernel, tq=tq, tk=tk, past=past),
        grid=(b, w // LANES),
        in_specs=[new_spec, new_spec, new_spec, cache_spec, cache_spec],
        out_specs=new_spec,
        out_shape=jax.ShapeDtypeStruct((m, w), BF16),
        compiler_params=_cparams(("arbitrary", "arbitrary")),
        name="sb_sample",
    )(q, kn, vn, kc, vc)


def _ssd_kernel(xbc_ref, z_ref, dt_ref, cprev_ref, h0_ref, cw_ref, cb_ref, dtb_ref, alog_ref, dsk_ref,
                nw_ref, o_ref, hout_ref, xpad_ref, st_ref, act_ref, y_ref, *, lc, inner, npairs):
    ci = pl.program_id(1)
    eye = _eye(LANES, BF16)

    @pl.when(ci == 0)
    def _():
        xpad_ref[0:CONV_PAD_ROWS, :] = cprev_ref[0]
        for p in range(npairs):
            st_ref[p] = _transpose_exact(eye, h0_ref[0, p])

    @pl.when(ci > 0)
    def _():
        xpad_ref[0:CONV_PAD_ROWS, :] = xpad_ref[lc:lc + CONV_PAD_ROWS, :]

    xpad_ref[CONV_PAD_ROWS:CONV_PAD_ROWS + lc, :] = xbc_ref[...]
    conv = cb_ref[...]
    first = CONV_PAD_ROWS - (CONV_WIDTH - 1)
    for w in range(CONV_WIDTH):
        conv = conv + cw_ref[w:w + 1, :] * xpad_ref[first + w:first + w + lc, :]
    act_ref[...] = _silu(conv)

    dtv = _softplus(dt_ref[...] + dtb_ref[...])
    a_neg = -jnp.exp(alog_ref[...])
    d_a = dtv * a_neg
    r = lax.broadcasted_iota(jnp.int32, (lc, lc), 0)
    c = lax.broadcasted_iota(jnp.int32, (lc, lc), 1)
    causal = r >= c
    prefix = jnp.where(causal, 1.0, 0.0).astype(BF16)
    a_cs = _dot_exact_rhs(prefix, d_a)
    a_cs_t = _transpose_exact(eye, a_cs)
    total = a_cs[lc - 1:lc, :]
    ea = jnp.exp(a_cs)
    to_end = jnp.exp(total - a_cs)
    chunk_decay = jnp.exp(total)
    low = lax.broadcasted_iota(jnp.int32, (lc, LANES), 1) < SSM_HEAD_DIM
    low1 = lax.broadcasted_iota(jnp.int32, (1, LANES), 1) < SSM_HEAD_DIM

    def pair_cols(arr, ha):
        return jnp.where(low, arr[:, ha:ha + 1], arr[:, ha + 1:ha + 2])

    pairs_per_group = npairs // SSM_GROUPS
    for g in range(SSM_GROUPS):
        b_bf = act_ref[:, inner + g * SSM_STATE:inner + (g + 1) * SSM_STATE].astype(BF16)
        c_off = inner + SSM_GROUPS * SSM_STATE
        c_bf = act_ref[:, c_off + g * SSM_STATE:c_off + (g + 1) * SSM_STATE].astype(BF16)
        cb = _dot_nt(c_bf, b_bf)
        b_t = _dot_nt(eye, b_bf).astype(BF16)
        for kk in range(pairs_per_group):
            p = g * pairs_per_group + kk
            ha = 2 * p
            x = act_ref[:, p * LANES:(p + 1) * LANES]
            xdt = x * pair_cols(dtv, ha)
            xdt_bf = xdt.astype(BF16)
            ys = []
            for h in (ha, ha + 1):
                seg = a_cs[:, h:h + 1] - a_cs_t[h:h + 1, :]
                decay = jnp.where(causal, jnp.exp(jnp.where(causal, seg, 0.0)), 0.0)
                ys.append(_dot((cb * decay).astype(BF16), xdt_bf))
            st = st_ref[p]
            y_off = _dot(c_bf, st.astype(BF16)) * pair_cols(ea, ha)
            y_ref[:, p * LANES:(p + 1) * LANES] = (
                jnp.where(low, ys[0], ys[1]) + y_off + dsk_ref[:, p * LANES:(p + 1) * LANES] * x)
            xw = (xdt * pair_cols(to_end, ha)).astype(BF16)
            dec = jnp.where(low1, chunk_decay[:, ha:ha + 1], chunk_decay[:, ha + 1:ha + 2])
            st_ref[p] = st * dec + _dot(b_t, xw)

    gy = y_ref[...] * _silu(z_ref[...])
    gw = inner // SSM_GROUPS
    for g in range(SSM_GROUPS):
        seg = gy[:, g * gw:(g + 1) * gw]
        ms = jnp.mean(seg * seg, axis=-1, keepdims=True)
        o_ref[:, g * gw:(g + 1) * gw] = (seg * lax.rsqrt(ms + LN_EPS) * nw_ref[:, g * gw:(g + 1) * gw]).astype(o_ref.dtype)

    @pl.when(ci == pl.num_programs(1) - 1)
    def _():
        for p in range(npairs):
            hout_ref[0, p] = _transpose_exact(eye, st_ref[p])


def _ssd_call(u, dtr, conv_prev, h0, cw, cb, dtb, alog, dsk, nw, *, batch, lc):
    m = u.shape[0]
    conv_ch = cw.shape[1]
    inner = nw.shape[1]
    npairs = h0.shape[1]
    nc = m // batch // lc
    zblk = conv_ch // inner
    row = lambda b, c: b * nc + c
    const = lambda b, c: (0, 0)
    return pl.pallas_call(
        functools.partial(_ssd_kernel, lc=lc, inner=inner, npairs=npairs),
        grid=(batch, nc),
        in_specs=[pl.BlockSpec((lc, conv_ch), lambda b, c: (row(b, c), 0)),
                  pl.BlockSpec((lc, inner), lambda b, c: (row(b, c), zblk)),
                  pl.BlockSpec((lc, LANES), lambda b, c: (row(b, c), 0)),
                  pl.BlockSpec((1, CONV_PAD_ROWS, conv_ch), lambda b, c: (b, 0, 0)),
                  pl.BlockSpec((1, npairs, LANES, LANES), lambda b, c: (b, 0, 0, 0)),
                  pl.BlockSpec((CONV_WIDTH, conv_ch), const),
                  pl.BlockSpec((1, conv_ch), const),
                  pl.BlockSpec((1, LANES), const),
                  pl.BlockSpec((1, LANES), const),
                  pl.BlockSpec((1, inner), const),
                  pl.BlockSpec((1, inner), const)],
        out_specs=[pl.BlockSpec((lc, inner), lambda b, c: (row(b, c), 0)),
                   pl.BlockSpec((1, npairs, LANES, LANES), lambda b, c: (b, 0, 0, 0))],
        out_shape=[jax.ShapeDtypeStruct((m, inner), BF16),
                   jax.ShapeDtypeStruct(h0.shape, F32)],
        scratch_shapes=[pltpu.VMEM((CONV_PAD_ROWS + lc, conv_ch), F32),
                        pltpu.VMEM((npairs, LANES, LANES), F32),
                        pltpu.VMEM((lc, conv_ch), F32),
                        pltpu.VMEM((lc, inner), F32)],
        compiler_params=_cparams(("arbitrary", "arbitrary")),
        name="ssd",
    )(u, u, dtr, conv_prev, h0, cw, cb, dtb, alog, dsk, nw)


def _merge_kernel(osb_ref, ossm_ref, gsb_ref, gssm_ref, x_ref, ga_ref, lg_ref, lb_ref,
                  wsb_ref, wssm_ref, wout_ref, o_ref, *, alpha):
    merged = (jax.nn.sigmoid(gsb_ref[...]) * _dot(osb_ref[...], wsb_ref[...])
              + jax.nn.sigmoid(gssm_ref[...]) * _dot(ossm_ref[...], wssm_ref[...]))
    res = alpha * x_ref[...] + ga_ref[...] * _dot(merged.astype(BF16), wout_ref[...])
    o_ref[...] = _ln(res) * lg_ref[...] + lb_ref[...]


def _resident(shape):
    return pl.BlockSpec(shape, lambda i: (0, 0), pipeline_mode=pl.Buffered(1))


def _merge_call(osb, ossm, u, x, gate_a, ln_g, ln_b, wsb, wssm, wout, *, alpha):
    m, d = x.shape
    sbw = osb.shape[1]
    inner = ossm.shape[1]
    tm = min(m, 512)
    gcol = (u.shape[1] - 2 * d) // d
    return pl.pallas_call(
        functools.partial(_merge_kernel, alpha=alpha),
        grid=(m // tm,),
        in_specs=[pl.BlockSpec((tm, sbw), lambda i: (i, 0)),
                  pl.BlockSpec((tm, inner), lambda i: (i, 0)),
                  pl.BlockSpec((tm, d), lambda i: (i, gcol)),
                  pl.BlockSpec((tm, d), lambda i: (i, gcol + 1)),
                  pl.BlockSpec((tm, d), lambda i: (i, 0)),
                  _row_spec(tm, d, gate_a.shape[0]),
                  pl.BlockSpec((1, d), lambda i: (0, 0)),
                  pl.BlockSpec((1, d), lambda i: (0, 0)),
                  _resident(wsb.shape), _resident(wssm.shape), _resident(wout.shape)],
        out_specs=pl.BlockSpec((tm, d), lambda i: (i, 0)),
        out_shape=jax.ShapeDtypeStruct((m, d), F32),
        compiler_params=_cparams(("arbitrary",)),
        name="merge",
    )(osb, ossm, u, u, x, gate_a, ln_g, ln_b, wsb, wssm, wout)


def _ffn_kernel(x1_ref, sc_ref, sh_ref, gf_ref, lg_ref, lb_ref, wup_ref, wdn_ref, o_ref, *, alpha, nchunk):
    x1 = x1_ref[...]
    h2 = (_ln(x1) * (1.0 + sc_ref[...]) + sh_ref[...]).astype(BF16)
    dff = wup_ref.shape[1]
    cw = dff // nchunk
    f = None
    for ck in range(nchunk):
        a = jnp.maximum(_dot(h2, wup_ref[:, ck * cw:(ck + 1) * cw]), 0.0)
        t = _dot((a * a).astype(BF16), wdn_ref[ck * cw:(ck + 1) * cw, :])
        f = t if f is None else f + t
    o_ref[...] = _ln(alpha * x1 + gf_ref[...] * f) * lg_ref[...] + lb_ref[...]


def _ffn_call(x1, scale, shift, gate, ln_g, ln_b, wup, wdn, *, alpha):
    m, d = x1.shape
    tm = min(m, 512)
    return pl.pallas_call(
        functools.partial(_ffn_kernel, alpha=alpha, nchunk=4),
        grid=(m // tm,),
        in_specs=[pl.BlockSpec((tm, d), lambda i: (i, 0)),
                  _row_spec(tm, d, scale.shape[0]),
                  _row_spec(tm, d, shift.shape[0]),
                  _row_spec(tm, d, gate.shape[0]),
                  pl.BlockSpec((1, d), lambda i: (0, 0)),
                  pl.BlockSpec((1, d), lambda i: (0, 0)),
                  _resident(wup.shape), _resident(wdn.shape)],
        out_specs=pl.BlockSpec((tm, d), lambda i: (i, 0)),
        out_shape=jax.ShapeDtypeStruct((m, d), F32),
        compiler_params=_cparams(("arbitrary",)),
        name="ffn",
    )(x1, scale, shift, gate, ln_g, ln_b, wup, wdn)


def _layer(x, mod, cache_k, cache_v, conv_prev, ssm_prev, wts, *, batch, alpha):
    m, d = x.shape
    t = m // batch
    shift_a, scale_a, gate_a, shift_f, scale_f, gate_f = [mod[:, i * d:(i + 1) * d] for i in range(6)]
    h = _ln_mod_call(x, scale_a, shift_a)
    (q,) = _matmul_call(h, wts["w_q"], [(BF16, SB_HEAD_DIM ** -0.5)], "proj_q")
    k32, kb = _matmul_call(h, wts["w_k"], [(F32, 1.0), (BF16, 1.0)], "proj_k")
    v32, vb = _matmul_call(h, wts["w_v"], [(F32, 1.0), (BF16, 1.0)], "proj_v")
    (u,) = _matmul_call(h, wts["w_rest"], [(F32, 1.0)], "proj_rest")
    (dtr,) = _matmul_call(h, wts["w_dt"], [(F32, 1.0)], "proj_dt")
    if cache_k is None:
        o_sb = _sb_prompt_call(q, kb, vb)
    else:
        o_sb = _sb_sample_call(q, kb, vb, cache_k, cache_v, t)
    conv_ch = wts["conv_w"].shape[1]
    cprev = jnp.pad(conv_prev, ((0, 0), (CONV_PAD_ROWS - (CONV_WIDTH - 1), 0), (0, 0)))
    npairs = ssm_prev.shape[1] // 2
    h0 = ssm_prev.reshape(batch, npairs, LANES, SSM_STATE)
    o_ssm, h_fin = _ssd_call(u, dtr, cprev, h0, wts["conv_w"], wts["conv_b"], wts["dt_bias"], wts["a_log"],
                             wts["d_skip"], wts["ssm_norm_w"], batch=batch, lc=min(t, 128))
    x1 = _merge_call(o_sb, o_ssm, u, x, gate_a, wts["ln_attn_g"], wts["ln_attn_b"],
                     wts["w_branch_sb"], wts["w_branch_ssm"], wts["w_out"], alpha=alpha)
    y = _ffn_call(x1, scale_f, shift_f, gate_f, wts["ln_ffn_g"], wts["ln_ffn_b"],
                  wts["w_up"], wts["w_down"], alpha=alpha)
    conv_new = u.reshape(batch, t, -1)[:, t - (CONV_WIDTH - 1):, :conv_ch]
    return y, k32, v32, conv_new, h_fin.reshape(ssm_prev.shape)


def _prep_weights(l, w_in, conv_w, conv_b, dt_bias, a_log, d_skip, ssm_norm_w, w_branch_sb, w_branch_ssm,
                  w_out, ln_attn_g, ln_attn_b, w_up, w_down, ln_ffn_g, ln_ffn_b, sbw):
    d = w_in.shape[1]
    inner = ssm_norm_w.shape[1]
    conv_ch = conv_w.shape[2]
    nh = a_log.shape[1]
    o_z = 3 * sbw
    o_xbc = o_z + inner
    o_dt = o_xbc + conv_ch
    o_g = o_dt + nh
    wi = w_in[l]
    pad_h = lambda v: jnp.pad(v, (0, LANES - nh))[None, :]
    return {
        "w_q": wi[:, :sbw].astype(BF16),
        "w_k": wi[:, sbw:2 * sbw].astype(BF16),
        "w_v": wi[:, 2 * sbw:3 * sbw].astype(BF16),
        "w_rest": jnp.concatenate([wi[:, o_xbc:o_dt], wi[:, o_z:o_xbc], wi[:, o_g:]], axis=1).astype(BF16),
        "w_dt": jnp.pad(wi[:, o_dt:o_g], ((0, 0), (0, LANES - nh))).astype(BF16),
        "conv_w": conv_w[l], "conv_b": conv_b[l][None, :],
        "dt_bias": pad_h(dt_bias[l]), "a_log": pad_h(a_log[l]),
        "d_skip": jnp.repeat(d_skip[l], SSM_HEAD_DIM)[None, :],
        "ssm_norm_w": ssm_norm_w[l][None, :],
        "w_branch_sb": w_branch_sb[l].astype(BF16), "w_branch_ssm": w_branch_ssm[l].astype(BF16),
        "w_out": w_out[l].astype(BF16),
        "ln_attn_g": ln_attn_g[l][None, :], "ln_attn_b": ln_attn_b[l][None, :],
        "w_up": w_up[l].astype(BF16), "w_down": w_down[l].astype(BF16),
        "ln_ffn_g": ln_ffn_g[l][None, :], "ln_ffn_b": ln_ffn_b[l][None, :],
    }


def kernel(x_prompt, x_sample, cache_sb_k, cache_sb_v, state_conv, state_ssm, c_prompt, c_sample, w_ada, b_ada, w_in, conv_w, conv_b, dt_bias, a_log, d_skip, ssm_norm_w, w_branch_sb, w_branch_ssm, w_out, ln_attn_g, ln_attn_b, w_up, w_down, ln_ffn_g, ln_ffn_b):
    depth = w_ada.shape[0]
    alpha = (2 * depth) ** 0.25
    n_p, t_p, d = x_prompt.shape
    n_s, t_s, _ = x_sample.shape
    past = cache_sb_k.shape[2]
    sbw = cache_sb_k.shape[3] * cache_sb_k.shape[4]
    conv_ch = conv_w.shape[2]
    y_p = x_prompt.reshape(n_p * t_p, d)
    y_s = x_sample.reshape(n_s * t_s, d)
    c_all = jnp.concatenate([c_prompt, c_sample], axis=0)
    rows = c_all.shape[0]
    c_all = jnp.pad(c_all, ((0, -rows % 8), (0, 0)))
    new_p, new_s = [], []
    for l in range(depth):
        wts = _prep_weights(l, w_in, conv_w, conv_b, dt_bias, a_log, d_skip, ssm_norm_w, w_branch_sb,
                            w_branch_ssm, w_out, ln_attn_g, ln_attn_b, w_up, w_down, ln_ffn_g, ln_ffn_b, sbw)
        mod = _mod_call(c_all, w_ada[l], b_ada[l][None, :])
        mod_p = mod[:n_p] if n_p == 1 else jnp.repeat(mod[:n_p], t_p, axis=0)
        mod_s = jnp.repeat(mod[n_p:n_p + n_s], t_s, axis=0)
        conv0 = jnp.zeros((n_p, CONV_WIDTH - 1, conv_ch), F32)
        ssm0 = jnp.zeros((n_p,) + state_ssm.shape[2:], F32)
        y_p, k_p, v_p, cv_p, s_p = _layer(y_p, mod_p, None, None, conv0, ssm0, wts, batch=n_p, alpha=alpha)
        y_s, k_s, v_s, cv_s, s_s = _layer(y_s, mod_s, cache_sb_k[l].reshape(n_s, past, sbw),
                                          cache_sb_v[l].reshape(n_s, past, sbw), state_conv[l], state_ssm[l],
                                          wts, batch=n_s, alpha=alpha)
        hshape = cache_sb_k.shape[3:]
        new_p.append((k_p.reshape((n_p, t_p) + hshape), v_p.reshape((n_p, t_p) + hshape), cv_p, s_p))
        new_s.append((k_s.reshape((n_s, t_s) + hshape), v_s.reshape((n_s, t_s) + hshape), cv_s, s_s))
    stack = lambda lst, i: jnp.stack([e[i] for e in lst])
    return (y_p.reshape(n_p, t_p, d), y_s.reshape(n_s, t_s, d),
            stack(new_p, 0), stack(new_p, 1), stack(new_p, 2), stack(new_p, 3),
            stack(new_s, 0), stack(new_s, 1), stack(new_s, 2), stack(new_s, 3))
```

```python
import functools

import jax
import jax.numpy as jnp
from jax import lax
from jax.experimental import pallas as pl
from jax.experimental.pallas import tpu as pltpu

F32 = jnp.float32
BF16 = jnp.bfloat16

SB_HEAD_DIM = 64
SSM_HEAD_DIM = 64
SSM_STATE = 128
SSM_GROUPS = 8
CONV_WIDTH = 4
LN_EPS = 1e-5
LANES = 128
CONV_PAD_ROWS = 8
VMEM_LIMIT = 56 * 1024 * 1024
LOG2E = 1.4426950408889634
SB_DEAD_LOG2 = 150.0


def _cparams(sem):
    return pltpu.CompilerParams(dimension_semantics=sem, vmem_limit_bytes=VMEM_LIMIT)


def _dot(a, b):
    return jnp.dot(a, b, preferred_element_type=F32)


def _dot_nt(a, b):
    return lax.dot_general(a, b, (((1,), (1,)), ((), ())), preferred_element_type=F32)


def _split_bf16(x, n):
    parts, r = [], x
    for _ in range(n):
        p = r.astype(BF16)
        parts.append(p)
        r = r - p.astype(F32)
    return parts


def _dot_exact_rhs(a_bf16, b_f32, n=3):
    out = None
    for p in _split_bf16(b_f32, n):
        t = _dot(a_bf16, p)
        out = t if out is None else out + t
    return out


def _dot_exact_lhs(a_f32, b_bf16, n):
    out = None
    for p in _split_bf16(a_f32, n):
        t = _dot(p, b_bf16)
        out = t if out is None else out + t
    return out


def _transpose_exact(eye_bf16, b_f32, n=3):
    out = None
    for p in _split_bf16(b_f32, n):
        t = _dot_nt(eye_bf16, p)
        out = t if out is None else out + t
    return out


def _softplus(x):
    return jnp.maximum(x, 0.0) + jnp.log(1.0 + jnp.exp(-jnp.abs(x)))


def _silu(x):
    return x * jax.nn.sigmoid(x)


def _ln(x):
    mu = jnp.mean(x, axis=-1, keepdims=True)
    xc = x - mu
    var = jnp.mean(xc * xc, axis=-1, keepdims=True)
    return xc * lax.rsqrt(var + LN_EPS)


def _eye(n, dtype):
    r = lax.broadcasted_iota(jnp.int32, (n, n), 0)
    c = lax.broadcasted_iota(jnp.int32, (n, n), 1)
    return jnp.where(r == c, 1.0, 0.0).astype(dtype)


def _row_spec(tm, width, rows):
    if rows == 1:
        return pl.BlockSpec((1, width), lambda i: (0, 0))
    return pl.BlockSpec((tm, width), lambda i: (i, 0))


def _mod_kernel(c_ref, w_ref, b_ref, o_ref):
    s = _silu(c_ref[...])
    s_hi, s_lo = _split_bf16(s, 2)
    w_hi, w_lo = _split_bf16(w_ref[...], 2)
    o_ref[...] = _dot(s_hi, w_hi) + _dot(s_hi, w_lo) + _dot(s_lo, w_hi) + b_ref[...]


def _mod_call(c, w, b):
    r, d = c.shape
    n = w.shape[1]
    tn = 1024
    return pl.pallas_call(
        _mod_kernel,
        grid=(n // tn,),
        in_specs=[pl.BlockSpec((r, d), lambda j: (0, 0)),
                  pl.BlockSpec((d, tn), lambda j: (0, j)),
                  pl.BlockSpec((1, tn), lambda j: (0, j))],
        out_specs=pl.BlockSpec((r, tn), lambda j: (0, j)),
        out_shape=jax.ShapeDtypeStruct((r, n), F32),
        compiler_params=_cparams(("arbitrary",)),
        name="ada_mod",
    )(c, w, b)


def _ln_mod_kernel(x_ref, sc_ref, sh_ref, o_ref):
    o_ref[...] = (_ln(x_ref[...]) * (1.0 + sc_ref[...]) + sh_ref[...]).astype(o_ref.dtype)


def _ln_mod_call(x, scale, shift):
    m, d = x.shape
    tm = min(m, 512)
    return pl.pallas_call(
        _ln_mod_kernel,
        grid=(m // tm,),
        in_specs=[pl.BlockSpec((tm, d), lambda i: (i, 0)),
                  _row_spec(tm, d, scale.shape[0]),
                  _row_spec(tm, d, shift.shape[0])],
        out_specs=pl.BlockSpec((tm, d), lambda i: (i, 0)),
        out_shape=jax.ShapeDtypeStruct((m, d), BF16),
        compiler_params=_cparams(("arbitrary",)),
        name="ln_mod",
    )(x, scale, shift)


def _matmul_kernel(x_ref, w_ref, *o_refs, scales):
    acc = _dot(x_ref[...], w_ref[...])
    for o_ref, s in zip(o_refs, scales):
        o_ref[...] = (acc if s == 1.0 else acc * s).astype(o_ref.dtype)


def _matmul_call(x, w, outs, name):
    m, k = x.shape
    n = w.shape[1]
    tm = min(m, 512)
    tn = min(n, 1024)
    res = pl.pallas_call(
        functools.partial(_matmul_kernel, scales=tuple(s for _, s in outs)),
        grid=(n // tn, m // tm),
        in_specs=[pl.BlockSpec((tm, k), lambda j, i: (i, 0)),
                  pl.BlockSpec((k, tn), lambda j, i: (0, j))],
        out_specs=[pl.BlockSpec((tm, tn), lambda j, i: (i, j)) for _ in outs],
        out_shape=[jax.ShapeDtypeStruct((m, n), dt) for dt, _ in outs],
        compiler_params=_cparams(("arbitrary", "arbitrary")),
        name=name,
    )(x, w)
    return res


def _suffix_matrix(tk):
    r = lax.broadcasted_iota(jnp.int32, (tk, tk), 0)
    c = lax.broadcasted_iota(jnp.int32, (tk, tk), 1)
    return jnp.where(r >= c, 1.0, 0.0).astype(BF16)


def _stacked_row_ids(n, tq, tk):
    one = lax.broadcasted_iota(jnp.int32, (tq, tk), 0)
    return jnp.concatenate([one] * n, axis=0)


def _sb_weights(z, c_run, suffix, visible):
    neg_abs = lax.bitcast_convert_type(lax.bitcast_convert_type(z, jnp.int32) | jnp.int32(-2 ** 31), F32)
    sp = jnp.maximum(z, 0.0) + jnp.log(1.0 + jnp.exp2(neg_abs)) * LOG2E
    if visible is not None:
        sp = jnp.where(visible, sp, 0.0)
    s = _dot(sp.astype(BF16), suffix) + c_run
    w = jnp.exp2(z - s)
    if visible is not None:
        w = jnp.where(visible, w, 0.0)
    return w.astype(BF16), s[:, 0:1]


def _sb_live(c_run):
    return (jnp.min(c_run) <= SB_DEAD_LOG2).astype(jnp.int32)


def _sb_prompt_kernel(q_ref, k_ref, v_ref, o_ref, *, tq, npair):
    i = pl.program_id(1)
    low = lax.broadcasted_iota(jnp.int32, (tq, LANES), 1) < SB_HEAD_DIM
    suffix = _suffix_matrix(tq)
    rows = _stacked_row_ids(2, tq, tq)
    cols = lax.broadcasted_iota(jnp.int32, (2 * tq, tq), 1)
    lanes = [slice(p * LANES, (p + 1) * LANES) for p in range(npair)]
    q_ab = []
    for sl in lanes:
        qf = q_ref[:, sl].astype(F32)
        q_ab.append(jnp.concatenate([jnp.where(low, qf, 0.0), jnp.where(low, 0.0, qf)], axis=0).astype(BF16))

    def block(at, carry, visible):
        out = []
        for p, (c_run, acc) in enumerate(carry):
            w, c_run = _sb_weights(_dot_nt(q_ab[p], k_ref[pl.ds(at, tq), lanes[p]]), c_run, suffix, visible)
            pv = _dot(w, v_ref[pl.ds(at, tq), lanes[p]])
            out.append((c_run, acc + jnp.where(low, pv[:tq], pv[tq:])))
        return out

    def live(carry):
        return _sb_live(functools.reduce(jnp.minimum, [c for c, _ in carry]))

    carry = [(jnp.zeros((2 * tq, 1), F32), jnp.zeros((tq, LANES), F32)) for _ in range(npair)]
    carry = block(pl.multiple_of(i * tq, tq), carry, cols < rows)
    carry = block(pl.multiple_of(jnp.maximum(i - 1, 0) * tq, tq), carry, rows >= jnp.where(i > 0, 0, tq))

    def cond(st):
        return jnp.logical_and(st[0] >= 0, st[1] > 0)

    def body(st):
        j, _, carry = st
        carry = block(pl.multiple_of(j * tq, tq), carry, None)
        return j - 1, live(carry), carry

    _, _, carry = lax.while_loop(cond, body, (i - 2, live(carry), carry))
    for p, (_, acc) in enumerate(carry):
        o_ref[:, lanes[p]] = acc.astype(o_ref.dtype)


def _sb_prompt_call(q, k, v):
    t, w = q.shape
    tq = min(t, 256)
    npair = 2 if w % (2 * LANES) == 0 else 1
    wb = npair * LANES
    return pl.pallas_call(
        functools.partial(_sb_prompt_kernel, tq=tq, npair=npair),
        grid=(w // wb, t // tq),
        in_specs=[pl.BlockSpec((tq, wb), lambda p, i: (i, p)),
                  pl.BlockSpec((t, wb), lambda p, i: (0, p)),
                  pl.BlockSpec((t, wb), lambda p, i: (0, p))],
        out_specs=pl.BlockSpec((tq, wb), lambda p, i: (i, p)),
        out_shape=jax.ShapeDtypeStruct((t, w), BF16),
        compiler_params=_cparams(("arbitrary", "arbitrary")),
        name="sb_prompt",
    )(q, k, v)


def _sb_sample_kernel(q_ref, kn_ref, vn_ref, kc_ref, vc_ref, o_ref, c_ref, acc_ref, live_ref, *, heads, tq, tk):
    j = pl.program_id(1)
    hd = SB_HEAD_DIM

    def stacked_block(k_of, v_of, suffix, visible):
        q = q_ref[...]
        z = jnp.concatenate([_dot_nt(q[:, h * hd:(h + 1) * hd], k_of(h)) for h in range(heads)], axis=0)
        w, c_run = _sb_weights(z, c_ref[...], suffix, visible)
        acc_ref[...] += jnp.concatenate([_dot(w[h * tq:(h + 1) * tq], v_of(h)) for h in range(heads)], axis=0)
        c_ref[...] = c_run
        live_ref[0] = _sb_live(c_run)

    @pl.when(j == 0)
    def _():
        c_ref[...] = jnp.zeros_like(c_ref)
        acc_ref[...] = jnp.zeros_like(acc_ref)
        rows = _stacked_row_ids(heads, tq, tq)
        cols = lax.broadcasted_iota(jnp.int32, (heads * tq, tq), 1)
        stacked_block(lambda h: kn_ref[:, h * hd:(h + 1) * hd], lambda h: vn_ref[:, h * hd:(h + 1) * hd],
                      _suffix_matrix(tq), cols < rows)

    @pl.when(jnp.logical_and(j > 0, live_ref[0] > 0))
    def _():
        stacked_block(lambda h: kc_ref[0, pl.ds(h, tk, stride=heads), :].astype(BF16),
                      lambda h: vc_ref[0, pl.ds(h, tk, stride=heads), :].astype(BF16),
                      _suffix_matrix(tk), None)

    @pl.when(j == pl.num_programs(1) - 1)
    def _():
        acc = acc_ref[...]
        o_ref[...] = jnp.concatenate([acc[h * tq:(h + 1) * tq] for h in range(heads)], axis=1).astype(o_ref.dtype)


def _sb_sample_call(q, kn, vn, kc, vc, *, tq, heads):
    m, w = q.shape
    b = kc.shape[0]
    past = kc.shape[1] // heads
    tk = min(past, 256)
    nblk = past // tk
    new_spec = pl.BlockSpec((tq, w), lambda bi, j: (bi, 0))
    cache_spec = pl.BlockSpec((1, tk * heads, SB_HEAD_DIM), lambda bi, j: (bi, jnp.minimum(nblk - j, nblk - 1), 0))
    return pl.pallas_call(
        functools.partial(_sb_sample_kernel, heads=heads, tq=tq, tk=tk),
        grid=(b, nblk + 1),
        in_specs=[new_spec, new_spec, new_spec, cache_spec, cache_spec],
        out_specs=new_spec,
        out_shape=jax.ShapeDtypeStruct((m, w), BF16),
        scratch_shapes=[pltpu.VMEM((heads * tq, 1), F32),
                        pltpu.VMEM((heads * tq, SB_HEAD_DIM), F32),
                        pltpu.SMEM((1,), jnp.int32)],
        compiler_params=_cparams(("arbitrary", "arbitrary")),
        name="sb_sample",
    )(q, kn, vn, kc, vc)


def _ssd_kernel(xbc_ref, z_ref, dt_ref, cprev_ref, h0_ref, cw_ref, cb_ref, dtb_ref, alog_ref, dsk_ref,
                nw_ref, o_ref, hout_ref, xpad_ref, st_ref, act_ref, y_ref, *, lc, inner, npairs):
    ci = pl.program_id(1)
    eye = _eye(LANES, BF16)

    @pl.when(ci == 0)
    def _():
        xpad_ref[0:CONV_PAD_ROWS, :] = cprev_ref[0]
        for p in range(npairs):
            st_ref[p] = _transpose_exact(eye, h0_ref[0, p])

    @pl.when(ci > 0)
    def _():
        xpad_ref[0:CONV_PAD_ROWS, :] = xpad_ref[lc:lc + CONV_PAD_ROWS, :]

    xpad_ref[CONV_PAD_ROWS:CONV_PAD_ROWS + lc, :] = xbc_ref[...]
    conv = cb_ref[...]
    first = CONV_PAD_ROWS - (CONV_WIDTH - 1)
    for w in range(CONV_WIDTH):
        conv = conv + cw_ref[w:w + 1, :] * xpad_ref[first + w:first + w + lc, :]
    act_ref[...] = _silu(conv)

    dtv = _softplus(dt_ref[...] + dtb_ref[...])
    a_neg = -jnp.exp(alog_ref[...])
    d_a = dtv * a_neg
    r = lax.broadcasted_iota(jnp.int32, (lc, lc), 0)
    c = lax.broadcasted_iota(jnp.int32, (lc, lc), 1)
    causal = r >= c
    prefix = jnp.where(causal, 1.0, 0.0).astype(BF16)
    acausal_off = jnp.where(causal, 0.0, -1e30)
    a_cs = _dot_exact_rhs(prefix, d_a)
    a_cs_t = _transpose_exact(eye, a_cs)
    total = a_cs[lc - 1:lc, :]
    ea = jnp.exp(a_cs)
    to_end = jnp.exp(total - a_cs)
    chunk_decay = jnp.exp(total)
    low = lax.broadcasted_iota(jnp.int32, (lc, LANES), 1) < SSM_HEAD_DIM
    low1 = lax.broadcasted_iota(jnp.int32, (1, LANES), 1) < SSM_HEAD_DIM

    def pair_cols(arr, ha):
        return jnp.where(low, arr[:, ha:ha + 1], arr[:, ha + 1:ha + 2])

    pairs_per_group = npairs // SSM_GROUPS
    for g in range(SSM_GROUPS):
        b_bf = act_ref[:, inner + g * SSM_STATE:inner + (g + 1) * SSM_STATE].astype(BF16)
        c_off = inner + SSM_GROUPS * SSM_STATE
        c_bf = act_ref[:, c_off + g * SSM_STATE:c_off + (g + 1) * SSM_STATE].astype(BF16)
        cb = _dot_nt(c_bf, b_bf)
        b_t = _dot_nt(eye, b_bf).astype(BF16)
        for kk in range(pairs_per_group):
            p = g * pairs_per_group + kk
            ha = 2 * p
            x = act_ref[:, p * LANES:(p + 1) * LANES]
            xdt = x * pair_cols(dtv, ha)
            xdt_bf = xdt.astype(BF16)
            ys = []
            for h in (ha, ha + 1):
                seg = a_cs[:, h:h + 1] - a_cs_t[h:h + 1, :]
                decay = jnp.exp(seg + acausal_off)
                ys.append(_dot((cb * decay).astype(BF16), xdt_bf))
            st = st_ref[p]
            y_off = _dot(c_bf, st.astype(BF16)) * pair_cols(ea, ha)
            y_ref[:, p * LANES:(p + 1) * LANES] = (
                jnp.where(low, ys[0], ys[1]) + y_off + dsk_ref[:, p * LANES:(p + 1) * LANES] * x)
            xw = (xdt * pair_cols(to_end, ha)).astype(BF16)
            dec = jnp.where(low1, chunk_decay[:, ha:ha + 1], chunk_decay[:, ha + 1:ha + 2])
            st_ref[p] = st * dec + _dot(b_t, xw)

    gy = y_ref[...] * _silu(z_ref[...].astype(F32))
    gw = inner // SSM_GROUPS
    for g in range(SSM_GROUPS):
        seg = gy[:, g * gw:(g + 1) * gw]
        ms = jnp.mean(seg * seg, axis=-1, keepdims=True)
        o_ref[:, g * gw:(g + 1) * gw] = (seg * lax.rsqrt(ms + LN_EPS) * nw_ref[:, g * gw:(g + 1) * gw]).astype(o_ref.dtype)

    @pl.when(ci == pl.num_programs(1) - 1)
    def _():
        for p in range(npairs):
            hout_ref[0, p] = _transpose_exact(eye, st_ref[p])


def _ssd_call(xbc, zg, dtr, conv_prev, h0, cw, cb, dtb, alog, dsk, nw, *, batch, lc):
    m = xbc.shape[0]
    conv_ch = cw.shape[1]
    inner = nw.shape[1]
    npairs = h0.shape[1]
    nc = m // batch // lc
    row = lambda b, c: b * nc + c
    const = lambda b, c: (0, 0)
    return pl.pallas_call(
        functools.partial(_ssd_kernel, lc=lc, inner=inner, npairs=npairs),
        grid=(batch, nc),
        in_specs=[pl.BlockSpec((lc, conv_ch), lambda b, c: (row(b, c), 0)),
                  pl.BlockSpec((lc, inner), lambda b, c: (row(b, c), 0)),
                  pl.BlockSpec((lc, LANES), lambda b, c: (row(b, c), 0)),
                  pl.BlockSpec((1, CONV_PAD_ROWS, conv_ch), lambda b, c: (b, 0, 0)),
                  pl.BlockSpec((1, npairs, LANES, LANES), lambda b, c: (b, 0, 0, 0)),
                  pl.BlockSpec((CONV_WIDTH, conv_ch), const),
                  pl.BlockSpec((1, conv_ch), const),
                  pl.BlockSpec((1, LANES), const),
                  pl.BlockSpec((1, LANES), const),
                  pl.BlockSpec((1, inner), const),
                  pl.BlockSpec((1, inner), const)],
        out_specs=[pl.BlockSpec((lc, inner), lambda b, c: (row(b, c), 0)),
                   pl.BlockSpec((1, npairs, LANES, LANES), lambda b, c: (b, 0, 0, 0))],
        out_shape=[jax.ShapeDtypeStruct((m, inner), BF16),
                   jax.ShapeDtypeStruct(h0.shape, F32)],
        scratch_shapes=[pltpu.VMEM((CONV_PAD_ROWS + lc, conv_ch), F32),
                        pltpu.VMEM((npairs, LANES, LANES), F32),
                        pltpu.VMEM((lc, conv_ch), F32),
                        pltpu.VMEM((lc, inner), F32)],
        compiler_params=_cparams(("arbitrary", "arbitrary")),
        name="ssd",
    )(xbc, zg, dtr, conv_prev, h0, cw, cb, dtb, alog, dsk, nw)


def _merge_kernel(osb_ref, ossm_ref, gsb_ref, gssm_ref, x_ref, ga_ref, lg_ref, lb_ref,
                  wsb_ref, wssm_ref, wout_ref, o_ref, *, alpha):
    merged = (jax.nn.sigmoid(gsb_ref[...].astype(F32)) * _dot(osb_ref[...], wsb_ref[...])
              + jax.nn.sigmoid(gssm_ref[...].astype(F32)) * _dot(ossm_ref[...], wssm_ref[...]))
    res = alpha * x_ref[...] + ga_ref[...] * _dot(merged.astype(BF16), wout_ref[...])
    o_ref[...] = _ln(res) * lg_ref[...] + lb_ref[...]


def _resident(shape):
    return pl.BlockSpec(shape, lambda i: (0, 0), pipeline_mode=pl.Buffered(1))


def _merge_call(osb, ossm, zg, x, gate_a, ln_g, ln_b, wsb, wssm, wout, *, alpha):
    m, d = x.shape
    sbw = osb.shape[1]
    inner = ossm.shape[1]
    tm = min(m, 512)
    gcol = inner // d
    return pl.pallas_call(
        functools.partial(_merge_kernel, alpha=alpha),
        grid=(m // tm,),
        in_specs=[pl.BlockSpec((tm, sbw), lambda i: (i, 0)),
                  pl.BlockSpec((tm, inner), lambda i: (i, 0)),
                  pl.BlockSpec((tm, d), lambda i: (i, gcol)),
                  pl.BlockSpec((tm, d), lambda i: (i, gcol + 1)),
                  pl.BlockSpec((tm, d), lambda i: (i, 0)),
                  _row_spec(tm, d, gate_a.shape[0]),
                  pl.BlockSpec((1, d), lambda i: (0, 0)),
                  pl.BlockSpec((1, d), lambda i: (0, 0)),
                  _resident(wsb.shape), _resident(wssm.shape), _resident(wout.shape)],
        out_specs=pl.BlockSpec((tm, d), lambda i: (i, 0)),
        out_shape=jax.ShapeDtypeStruct((m, d), F32),
        compiler_params=_cparams(("arbitrary",)),
        name="merge",
    )(osb, ossm, zg, zg, x, gate_a, ln_g, ln_b, wsb, wssm, wout)


def _ffn_kernel(x1_ref, sc_ref, sh_ref, gf_ref, lg_ref, lb_ref, wup_ref, wdn_ref, o_ref, *, alpha, nchunk):
    x1 = x1_ref[...]
    h2 = (_ln(x1) * (1.0 + sc_ref[...]) + sh_ref[...]).astype(BF16)
    dff = wup_ref.shape[1]
    cw = dff // nchunk
    f = None
    for ck in range(nchunk):
        a = jnp.maximum(_dot(h2, wup_ref[:, ck * cw:(ck + 1) * cw]), 0.0)
        t = _dot((a * a).astype(BF16), wdn_ref[ck * cw:(ck + 1) * cw, :])
        f = t if f is None else f + t
    o_ref[...] = _ln(alpha * x1 + gf_ref[...] * f) * lg_ref[...] + lb_ref[...]


def _ffn_call(x1, scale, shift, gate, ln_g, ln_b, wup, wdn, *, alpha):
    m, d = x1.shape
    tm = min(m, 512)
    return pl.pallas_call(
        functools.partial(_ffn_kernel, alpha=alpha, nchunk=4),
        grid=(m // tm,),
        in_specs=[pl.BlockSpec((tm, d), lambda i: (i, 0)),
                  _row_spec(tm, d, scale.shape[0]),
                  _row_spec(tm, d, shift.shape[0]),
                  _row_spec(tm, d, gate.shape[0]),
                  pl.BlockSpec((1, d), lambda i: (0, 0)),
                  pl.BlockSpec((1, d), lambda i: (0, 0)),
                  _resident(wup.shape), _resident(wdn.shape)],
        out_specs=pl.BlockSpec((tm, d), lambda i: (i, 0)),
        out_shape=jax.ShapeDtypeStruct((m, d), F32),
        compiler_params=_cparams(("arbitrary",)),
        name="ffn",
    )(x1, scale, shift, gate, ln_g, ln_b, wup, wdn)


def _layer(x, mod, cache_k, cache_v, conv_prev, ssm_prev, wts, *, batch, alpha):
    m, d = x.shape
    t = m // batch
    shift_a, scale_a, gate_a, shift_f, scale_f, gate_f = [mod[:, i * d:(i + 1) * d] for i in range(6)]
    h = _ln_mod_call(x, scale_a, shift_a)
    (q,) = _matmul_call(h, wts["w_q"], [(BF16, SB_HEAD_DIM ** -0.5 * LOG2E)], "proj_q")
    k32, kb = _matmul_call(h, wts["w_k"], [(F32, 1.0), (BF16, 1.0)], "proj_k")
    v32, vb = _matmul_call(h, wts["w_v"], [(F32, 1.0), (BF16, 1.0)], "proj_v")
    (xbc,) = _matmul_call(h, wts["w_xbc"], [(F32, 1.0)], "proj_xbc")
    (zg,) = _matmul_call(h, wts["w_zg"], [(BF16, 1.0)], "proj_zg")
    (dtr,) = _matmul_call(h, wts["w_dt"], [(F32, 1.0)], "proj_dt")
    if cache_k is None:
        o_sb = _sb_prompt_call(q, kb, vb)
    else:
        o_sb = _sb_sample_call(q, kb, vb, cache_k, cache_v, tq=t, heads=q.shape[1] // SB_HEAD_DIM)
    conv_ch = wts["conv_w"].shape[1]
    cprev = jnp.pad(conv_prev, ((0, 0), (CONV_PAD_ROWS - (CONV_WIDTH - 1), 0), (0, 0)))
    npairs = ssm_prev.shape[1] // 2
    h0 = ssm_prev.reshape(batch, npairs, LANES, SSM_STATE)
    o_ssm, h_fin = _ssd_call(xbc, zg, dtr, cprev, h0, wts["conv_w"], wts["conv_b"], wts["dt_bias"], wts["a_log"],
                             wts["d_skip"], wts["ssm_norm_w"], batch=batch, lc=min(t, 128))
    x1 = _merge_call(o_sb, o_ssm, zg, x, gate_a, wts["ln_attn_g"], wts["ln_attn_b"],
                     wts["w_branch_sb"], wts["w_branch_ssm"], wts["w_out"], alpha=alpha)
    y = _ffn_call(x1, scale_f, shift_f, gate_f, wts["ln_ffn_g"], wts["ln_ffn_b"],
                  wts["w_up"], wts["w_down"], alpha=alpha)
    conv_new = xbc.reshape(batch, t, conv_ch)[:, t - (CONV_WIDTH - 1):]
    return y, k32, v32, conv_new, h_fin.reshape(ssm_prev.shape)


def _prep_weights(l, w_in, conv_w, conv_b, dt_bias, a_log, d_skip, ssm_norm_w, w_branch_sb, w_branch_ssm,
                  w_out, ln_attn_g, ln_attn_b, w_up, w_down, ln_ffn_g, ln_ffn_b, sbw):
    d = w_in.shape[1]
    inner = ssm_norm_w.shape[1]
    conv_ch = conv_w.shape[2]
    nh = a_log.shape[1]
    o_z = 3 * sbw
    o_xbc = o_z + inner
    o_dt = o_xbc + conv_ch
    o_g = o_dt + nh
    wi = w_in[l]
    pad_h = lambda v: jnp.pad(v, (0, LANES - nh))[None, :]
    return {
        "w_q": wi[:, :sbw].astype(BF16),
        "w_k": wi[:, sbw:2 * sbw].astype(BF16),
        "w_v": wi[:, 2 * sbw:3 * sbw].astype(BF16),
        "w_xbc": wi[:, o_xbc:o_dt].astype(BF16),
        "w_zg": jnp.concatenate([wi[:, o_z:o_xbc], wi[:, o_g:]], axis=1).astype(BF16),
        "w_dt": jnp.pad(wi[:, o_dt:o_g], ((0, 0), (0, LANES - nh))).astype(BF16),
        "conv_w": conv_w[l], "conv_b": conv_b[l][None, :],
        "dt_bias": pad_h(dt_bias[l]), "a_log": pad_h(a_log[l]),
        "d_skip": jnp.repeat(d_skip[l], SSM_HEAD_DIM)[None, :],
        "ssm_norm_w": ssm_norm_w[l][None, :],
        "w_branch_sb": w_branch_sb[l].astype(BF16), "w_branch_ssm": w_branch_ssm[l].astype(BF16),
        "w_out": w_out[l].astype(BF16),
        "ln_attn_g": ln_attn_g[l][None, :], "ln_attn_b": ln_attn_b[l][None, :],
        "w_up": w_up[l].astype(BF16), "w_down": w_down[l].astype(BF16),
        "ln_ffn_g": ln_ffn_g[l][None, :], "ln_ffn_b": ln_ffn_b[l][None, :],
    }


def kernel(x_prompt, x_sample, cache_sb_k, cache_sb_v, state_conv, state_ssm, c_prompt, c_sample, w_ada, b_ada, w_in, conv_w, conv_b, dt_bias, a_log, d_skip, ssm_norm_w, w_branch_sb, w_branch_ssm, w_out, ln_attn_g, ln_attn_b, w_up, w_down, ln_ffn_g, ln_ffn_b):
    depth = w_ada.shape[0]
    alpha = (2 * depth) ** 0.25
    n_p, t_p, d = x_prompt.shape
    n_s, t_s, _ = x_sample.shape
    past = cache_sb_k.shape[2]
    sbw = cache_sb_k.shape[3] * cache_sb_k.shape[4]
    conv_ch = conv_w.shape[2]
    y_p = x_prompt.reshape(n_p * t_p, d)
    y_s = x_sample.reshape(n_s * t_s, d)
    c_all = jnp.concatenate([c_prompt, c_sample], axis=0)
    rows = c_all.shape[0]
    c_all = jnp.pad(c_all, ((0, -rows % 8), (0, 0)))
    new_p, new_s = [], []
    for l in range(depth):
        wts = _prep_weights(l, w_in, conv_w, conv_b, dt_bias, a_log, d_skip, ssm_norm_w, w_branch_sb,
                            w_branch_ssm, w_out, ln_attn_g, ln_attn_b, w_up, w_down, ln_ffn_g, ln_ffn_b, sbw)
        mod = _mod_call(c_all, w_ada[l], b_ada[l][None, :])
        mod_p = mod[:n_p] if n_p == 1 else jnp.repeat(mod[:n_p], t_p, axis=0)
        mod_s = jnp.repeat(mod[n_p:n_p + n_s], t_s, axis=0)
        conv0 = jnp.zeros((n_p, CONV_WIDTH - 1, conv_ch), F32)
        ssm0 = jnp.zeros((n_p,) + state_ssm.shape[2:], F32)
        y_p, k_p, v_p, cv_p, s_p = _layer(y_p, mod_p, None, None, conv0, ssm0, wts, batch=n_p, alpha=alpha)
        y_s, k_s, v_s, cv_s, s_s = _layer(y_s, mod_s, cache_sb_k[l].reshape(n_s, -1, SB_HEAD_DIM),
                                          cache_sb_v[l].reshape(n_s, -1, SB_HEAD_DIM), state_conv[l], state_ssm[l],
                                          wts, batch=n_s, alpha=alpha)
        hshape = cache_sb_k.shape[3:]
        new_p.append((k_p.reshape((n_p, t_p) + hshape), v_p.reshape((n_p, t_p) + hshape), cv_p, s_p))
        new_s.append((k_s.reshape((n_s, t_s) + hshape), v_s.reshape((n_s, t_s) + hshape), cv_s, s_s))
    stack = lambda lst, i: jnp.stack([e[i] for e in lst])
    return (y_p.reshape(n_p, t_p, d), y_s.reshape(n_s, t_s, d),
            stack(new_p, 0), stack(new_p, 1), stack(new_p, 2), stack(new_p, 3),
            stack(new_s, 0), stack(new_s, 1), stack(new_s, 2), stack(new_s, 3))
```

```python
import functools

import jax
import jax.numpy as jnp
from jax import lax
from jax.experimental import pallas as pl
from jax.experimental.pallas import tpu as pltpu

F32 = jnp.float32
BF16 = jnp.bfloat16

SB_HEAD_DIM = 64
SSM_HEAD_DIM = 64
SSM_STATE = 128
SSM_GROUPS = 8
CONV_WIDTH = 4
LN_EPS = 1e-5
LANES = 128
CONV_PAD_ROWS = 8
VMEM_LIMIT = 56 * 1024 * 1024
LOG2E = 1.4426950408889634
SB_DEAD_LOG2 = 150.0


def _cparams(sem):
    return pltpu.CompilerParams(dimension_semantics=sem, vmem_limit_bytes=VMEM_LIMIT)


def _dot(a, b):
    return jnp.dot(a, b, preferred_element_type=F32)


def _dot_nt(a, b):
    return lax.dot_general(a, b, (((1,), (1,)), ((), ())), preferred_element_type=F32)


def _split_bf16(x, n):
    parts, r = [], x
    for _ in range(n):
        p = r.astype(BF16)
        parts.append(p)
        r = r - p.astype(F32)
    return parts


def _dot_exact_rhs(a_bf16, b_f32, n=3):
    out = None
    for p in _split_bf16(b_f32, n):
        t = _dot(a_bf16, p)
        out = t if out is None else out + t
    return out


def _dot_exact_lhs(a_f32, b_bf16, n):
    out = None
    for p in _split_bf16(a_f32, n):
        t = _dot(p, b_bf16)
        out = t if out is None else out + t
    return out


def _transpose_exact(eye_bf16, b_f32, n=3):
    out = None
    for p in _split_bf16(b_f32, n):
        t = _dot_nt(eye_bf16, p)
        out = t if out is None else out + t
    return out


def _softplus(x):
    return jnp.maximum(x, 0.0) + jnp.log(1.0 + jnp.exp(-jnp.abs(x)))


def _silu(x):
    h = 0.5 * x
    return h + h * jnp.tanh(h)


def _ln(x):
    mu = jnp.mean(x, axis=-1, keepdims=True)
    xc = x - mu
    var = jnp.mean(xc * xc, axis=-1, keepdims=True)
    return xc * lax.rsqrt(var + LN_EPS)


def _eye(n, dtype):
    r = lax.broadcasted_iota(jnp.int32, (n, n), 0)
    c = lax.broadcasted_iota(jnp.int32, (n, n), 1)
    return jnp.where(r == c, 1.0, 0.0).astype(dtype)


def _row_spec(tm, width, rows):
    if rows == 1:
        return pl.BlockSpec((1, width), lambda i: (0, 0))
    return pl.BlockSpec((tm, width), lambda i: (i, 0))


def _mod_kernel(c_ref, w_ref, b_ref, o_ref):
    s = _silu(c_ref[...])
    s_hi, s_lo = _split_bf16(s, 2)
    w_hi, w_lo = _split_bf16(w_ref[...], 2)
    o_ref[...] = _dot(s_hi, w_hi) + _dot(s_hi, w_lo) + _dot(s_lo, w_hi) + b_ref[...]


def _mod_call(c, w, b):
    r, d = c.shape
    n = w.shape[1]
    tn = 1024
    return pl.pallas_call(
        _mod_kernel,
        grid=(n // tn,),
        in_specs=[pl.BlockSpec((r, d), lambda j: (0, 0)),
                  pl.BlockSpec((d, tn), lambda j: (0, j)),
                  pl.BlockSpec((1, tn), lambda j: (0, j))],
        out_specs=pl.BlockSpec((r, tn), lambda j: (0, j)),
        out_shape=jax.ShapeDtypeStruct((r, n), F32),
        compiler_params=_cparams(("arbitrary",)),
        name="ada_mod",
    )(c, w, b)


def _ln_mod_kernel(x_ref, sc_ref, sh_ref, o_ref):
    o_ref[...] = (_ln(x_ref[...]) * (1.0 + sc_ref[...]) + sh_ref[...]).astype(o_ref.dtype)


def _ln_mod_call(x, scale, shift):
    m, d = x.shape
    tm = min(m, 512)
    return pl.pallas_call(
        _ln_mod_kernel,
        grid=(m // tm,),
        in_specs=[pl.BlockSpec((tm, d), lambda i: (i, 0)),
                  _row_spec(tm, d, scale.shape[0]),
                  _row_spec(tm, d, shift.shape[0])],
        out_specs=pl.BlockSpec((tm, d), lambda i: (i, 0)),
        out_shape=jax.ShapeDtypeStruct((m, d), BF16),
        compiler_params=_cparams(("arbitrary",)),
        name="ln_mod",
    )(x, scale, shift)


def _matmul_kernel(x_ref, w_ref, *o_refs, scales):
    acc = _dot(x_ref[...], w_ref[...])
    for o_ref, s in zip(o_refs, scales):
        o_ref[...] = (acc if s == 1.0 else acc * s).astype(o_ref.dtype)


def _matmul_call(x, w, outs, name):
    m, k = x.shape
    n = w.shape[1]
    tm = min(m, 1024)
    tn = min(n, 1024)
    res = pl.pallas_call(
        functools.partial(_matmul_kernel, scales=tuple(s for _, s in outs)),
        grid=(n // tn, m // tm),
        in_specs=[pl.BlockSpec((tm, k), lambda j, i: (i, 0)),
                  pl.BlockSpec((k, tn), lambda j, i: (0, j))],
        out_specs=[pl.BlockSpec((tm, tn), lambda j, i: (i, j)) for _ in outs],
        out_shape=[jax.ShapeDtypeStruct((m, n), dt) for dt, _ in outs],
        compiler_params=_cparams(("arbitrary", "arbitrary")),
        name=name,
    )(x, w)
    return res


def _suffix_matrix(tk):
    r = lax.broadcasted_iota(jnp.int32, (tk, tk), 0)
    c = lax.broadcasted_iota(jnp.int32, (tk, tk), 1)
    return jnp.where(r >= c, 1.0, 0.0).astype(BF16)


def _stacked_row_ids(n, tq, tk):
    one = lax.broadcasted_iota(jnp.int32, (tq, tk), 0)
    return jnp.concatenate([one] * n, axis=0)


def _sb_weights(z, c_run, suffix, visible):
    neg_abs = lax.bitcast_convert_type(lax.bitcast_convert_type(z, jnp.int32) | jnp.int32(-2 ** 31), F32)
    sp = jnp.maximum(z, 0.0) + jnp.log(1.0 + jnp.exp2(neg_abs)) * LOG2E
    if visible is not None:
        sp = jnp.where(visible, sp, 0.0)
    s = _dot(sp.astype(BF16), suffix) + c_run
    w = jnp.exp2(z - s)
    if visible is not None:
        w = jnp.where(visible, w, 0.0)
    return w.astype(BF16), s[:, 0:1]


def _sb_live(c_run):
    return (jnp.min(c_run) <= SB_DEAD_LOG2).astype(jnp.int32)


def _sb_prompt_kernel(q_ref, k_ref, v_ref, o_ref, *, tq, npair):
    i = pl.program_id(1)
    low = lax.broadcasted_iota(jnp.int32, (tq, LANES), 1) < SB_HEAD_DIM
    suffix = _suffix_matrix(tq)
    rows = _stacked_row_ids(2, tq, tq)
    cols = lax.broadcasted_iota(jnp.int32, (2 * tq, tq), 1)
    lanes = [slice(p * LANES, (p + 1) * LANES) for p in range(npair)]
    q_ab = []
    for sl in lanes:
        qf = q_ref[:, sl].astype(F32)
        q_ab.append(jnp.concatenate([jnp.where(low, qf, 0.0), jnp.where(low, 0.0, qf)], axis=0).astype(BF16))

    def block(at, carry, visible):
        out = []
        for p, (c_run, acc) in enumerate(carry):
            w, c_run = _sb_weights(_dot_nt(q_ab[p], k_ref[pl.ds(at, tq), lanes[p]]), c_run, suffix, visible)
            pv = _dot(w, v_ref[pl.ds(at, tq), lanes[p]])
            out.append((c_run, acc + jnp.where(low, pv[:tq], pv[tq:])))
        return out

    def live(carry):
        return _sb_live(functools.reduce(jnp.minimum, [c for c, _ in carry]))

    carry = [(jnp.zeros((2 * tq, 1), F32), jnp.zeros((tq, LANES), F32)) for _ in range(npair)]
    carry = block(pl.multiple_of(i * tq, tq), carry, cols < rows)
    carry = block(pl.multiple_of(jnp.maximum(i - 1, 0) * tq, tq), carry, rows >= jnp.where(i > 0, 0, tq))

    def cond(st):
        return jnp.logical_and(st[0] >= 0, st[1] > 0)

    def body(st):
        j, _, carry = st
        carry = block(pl.multiple_of(j * tq, tq), carry, None)
        return j - 1, live(carry), carry

    _, _, carry = lax.while_loop(cond, body, (i - 2, live(carry), carry))
    for p, (_, acc) in enumerate(carry):
        o_ref[:, lanes[p]] = acc.astype(o_ref.dtype)


def _sb_prompt_call(q, k, v):
    t, w = q.shape
    tq = min(t, 256)
    npair = 2 if w % (2 * LANES) == 0 else 1
    wb = npair * LANES
    return pl.pallas_call(
        functools.partial(_sb_prompt_kernel, tq=tq, npair=npair),
        grid=(w // wb, t // tq),
        in_specs=[pl.BlockSpec((tq, wb), lambda p, i: (i, p)),
                  pl.BlockSpec((t, wb), lambda p, i: (0, p)),
                  pl.BlockSpec((t, wb), lambda p, i: (0, p))],
        out_specs=pl.BlockSpec((tq, wb), lambda p, i: (i, p)),
        out_shape=jax.ShapeDtypeStruct((t, w), BF16),
        compiler_params=_cparams(("arbitrary", "arbitrary")),
        name="sb_prompt",
    )(q, k, v)


def _sb_sample_kernel(q_ref, kn_ref, vn_ref, kc_ref, vc_ref, o_ref, c_ref, acc_ref, live_ref, *, heads, tq, tk):
    j = pl.program_id(1)
    hd = SB_HEAD_DIM

    def stacked_block(qk_of, pv_of, suffix, visible):
        q = q_ref[...]
        z = jnp.concatenate([qk_of(q[:, h * hd:(h + 1) * hd], h) for h in range(heads)], axis=0)
        w, c_run = _sb_weights(z, c_ref[...], suffix, visible)
        acc_ref[...] += jnp.concatenate([pv_of(w[h * tq:(h + 1) * tq], h) for h in range(heads)], axis=0)
        c_ref[...] = c_run
        live_ref[0] = _sb_live(c_run)

    @pl.when(j == 0)
    def _():
        c_ref[...] = jnp.zeros_like(c_ref)
        acc_ref[...] = jnp.zeros_like(acc_ref)
        rows = _stacked_row_ids(heads, tq, tq)
        cols = lax.broadcasted_iota(jnp.int32, (heads * tq, tq), 1)
        stacked_block(lambda qh, h: _dot_nt(qh, kn_ref[:, h * hd:(h + 1) * hd]),
                      lambda wh, h: _dot(wh, vn_ref[:, h * hd:(h + 1) * hd]),
                      _suffix_matrix(tq), cols < rows)

    @pl.when(jnp.logical_and(j > 0, live_ref[0] > 0))
    def _():
        stacked_block(lambda qh, h: _dot(qh, kc_ref[0, h].astype(BF16)),
                      lambda wh, h: _dot_nt(wh, vc_ref[0, h].astype(BF16)),
                      _suffix_matrix(tk), None)

    @pl.when(j == pl.num_programs(1) - 1)
    def _():
        acc = acc_ref[...]
        o_ref[...] = jnp.concatenate([acc[h * tq:(h + 1) * tq] for h in range(heads)], axis=1).astype(o_ref.dtype)


def _sb_sample_call(q, kn, vn, kc, vc, *, tq, heads):
    m, w = q.shape
    b, _, _, past = kc.shape
    tk = min(past, 256)
    nblk = past // tk
    new_spec = pl.BlockSpec((tq, w), lambda bi, j: (bi, 0))
    cache_spec = pl.BlockSpec((1, heads, SB_HEAD_DIM, tk),
                              lambda bi, j: (bi, 0, 0, jnp.minimum(nblk - j, nblk - 1)))
    return pl.pallas_call(
        functools.partial(_sb_sample_kernel, heads=heads, tq=tq, tk=tk),
        grid=(b, nblk + 1),
        in_specs=[new_spec, new_spec, new_spec, cache_spec, cache_spec],
        out_specs=new_spec,
        out_shape=jax.ShapeDtypeStruct((m, w), BF16),
        scratch_shapes=[pltpu.VMEM((heads * tq, 1), F32),
                        pltpu.VMEM((heads * tq, SB_HEAD_DIM), F32),
                        pltpu.SMEM((1,), jnp.int32)],
        compiler_params=_cparams(("arbitrary", "arbitrary")),
        name="sb_sample",
    )(q, kn, vn, kc, vc)


def _ssd_kernel(xbc_ref, z_ref, dt_ref, cprev_ref, h0_ref, cw_ref, cb_ref, dtb_ref, alog_ref, dsk_ref,
                nw_ref, o_ref, hout_ref, xpad_ref, st_ref, act_ref, y_ref, *, lc, inner, npairs):
    ci = pl.program_id(1)
    eye = _eye(LANES, BF16)

    @pl.when(ci == 0)
    def _():
        xpad_ref[0:CONV_PAD_ROWS, :] = cprev_ref[0]
        for p in range(npairs):
            st_ref[p] = h0_ref[0, p].T

    @pl.when(ci > 0)
    def _():
        xpad_ref[0:CONV_PAD_ROWS, :] = xpad_ref[lc:lc + CONV_PAD_ROWS, :]

    xpad_ref[CONV_PAD_ROWS:CONV_PAD_ROWS + lc, :] = xbc_ref[...]
    xp = xpad_ref[...]
    conv = cb_ref[...] + cw_ref[CONV_WIDTH - 1:CONV_WIDTH, :] * xp[CONV_PAD_ROWS:, :]
    for back in range(1, CONV_WIDTH):
        w = CONV_WIDTH - 1 - back
        conv = conv + cw_ref[w:w + 1, :] * pltpu.roll(xp, back, axis=0)[CONV_PAD_ROWS:, :]
    act_ref[...] = _silu(conv)

    dtv = _softplus(dt_ref[...] + dtb_ref[...])
    a_neg = -jnp.exp(alog_ref[...])
    d_a = dtv * a_neg
    r = lax.broadcasted_iota(jnp.int32, (lc, lc), 0)
    c = lax.broadcasted_iota(jnp.int32, (lc, lc), 1)
    causal = r >= c
    prefix = jnp.where(causal, 1.0, 0.0).astype(BF16)
    acausal_off = jnp.where(causal, 0.0, -1e30)
    a_cs = _dot_exact_rhs(prefix, d_a)
    a_cs_t = _transpose_exact(eye, a_cs)
    total = a_cs[lc - 1:lc, :]
    ea = jnp.exp(a_cs)
    to_end = jnp.exp(total - a_cs)
    chunk_decay = jnp.exp(total)
    low = lax.broadcasted_iota(jnp.int32, (lc, LANES), 1) < SSM_HEAD_DIM
    low1 = lax.broadcasted_iota(jnp.int32, (1, LANES), 1) < SSM_HEAD_DIM

    def pair_cols(arr, ha):
        return jnp.where(low, arr[:, ha:ha + 1], arr[:, ha + 1:ha + 2])

    pairs_per_group = npairs // SSM_GROUPS
    for g in range(SSM_GROUPS):
        b_bf = act_ref[:, inner + g * SSM_STATE:inner + (g + 1) * SSM_STATE].astype(BF16)
        c_off = inner + SSM_GROUPS * SSM_STATE
        c_bf = act_ref[:, c_off + g * SSM_STATE:c_off + (g + 1) * SSM_STATE].astype(BF16)
        cb = _dot_nt(c_bf, b_bf)
        b_t = _dot_nt(eye, b_bf).astype(BF16)
        for kk in range(pairs_per_group):
            p = g * pairs_per_group + kk
            ha = 2 * p
            x = act_ref[:, p * LANES:(p + 1) * LANES]
            xdt = x * pair_cols(dtv, ha)
            xdt_bf = xdt.astype(BF16)
            ys = []
            for h in (ha, ha + 1):
                seg = a_cs[:, h:h + 1] - a_cs_t[h:h + 1, :]
                decay = jnp.exp(seg + acausal_off)
                ys.append(_dot((cb * decay).astype(BF16), xdt_bf))
            st = st_ref[p]
            y_off = _dot(c_bf, st.astype(BF16)) * pair_cols(ea, ha)
            y_ref[:, p * LANES:(p + 1) * LANES] = (
                jnp.where(low, ys[0], ys[1]) + y_off + dsk_ref[:, p * LANES:(p + 1) * LANES] * x)
            xw = (xdt * pair_cols(to_end, ha)).astype(BF16)
            dec = jnp.where(low1, chunk_decay[:, ha:ha + 1], chunk_decay[:, ha + 1:ha + 2])
            st_ref[p] = st * dec + _dot(b_t, xw)

    gy = y_ref[...] * _silu(z_ref[...].astype(F32))
    gw = inner // SSM_GROUPS
    for g in range(SSM_GROUPS):
        seg = gy[:, g * gw:(g + 1) * gw]
        ms = jnp.mean(seg * seg, axis=-1, keepdims=True)
        o_ref[:, g * gw:(g + 1) * gw] = (seg * lax.rsqrt(ms + LN_EPS) * nw_ref[:, g * gw:(g + 1) * gw]).astype(o_ref.dtype)

    @pl.when(ci == pl.num_programs(1) - 1)
    def _():
        for p in range(npairs):
            hout_ref[0, p] = st_ref[p].T


def _ssd_call(xbc, zg, dtr, conv_prev, h0, cw, cb, dtb, alog, dsk, nw, *, batch, lc):
    m = xbc.shape[0]
    conv_ch = cw.shape[1]
    inner = nw.shape[1]
    npairs = h0.shape[1]
    nc = m // batch // lc
    row = lambda b, c: b * nc + c
    const = lambda b, c: (0, 0)
    return pl.pallas_call(
        functools.partial(_ssd_kernel, lc=lc, inner=inner, npairs=npairs),
        grid=(batch, nc),
        in_specs=[pl.BlockSpec((lc, conv_ch), lambda b, c: (row(b, c), 0)),
                  pl.BlockSpec((lc, inner), lambda b, c: (row(b, c), 0)),
                  pl.BlockSpec((lc, LANES), lambda b, c: (row(b, c), 0)),
                  pl.BlockSpec((1, CONV_PAD_ROWS, conv_ch), lambda b, c: (b, 0, 0)),
                  pl.BlockSpec((1, npairs, LANES, LANES), lambda b, c: (b, 0, 0, 0)),
                  pl.BlockSpec((CONV_WIDTH, conv_ch), const),
                  pl.BlockSpec((1, conv_ch), const),
                  pl.BlockSpec((1, LANES), const),
                  pl.BlockSpec((1, LANES), const),
                  pl.BlockSpec((1, inner), const),
                  pl.BlockSpec((1, inner), const)],
        out_specs=[pl.BlockSpec((lc, inner), lambda b, c: (row(b, c), 0)),
                   pl.BlockSpec((1, npairs, LANES, LANES), lambda b, c: (b, 0, 0, 0))],
        out_shape=[jax.ShapeDtypeStruct((m, inner), BF16),
                   jax.ShapeDtypeStruct(h0.shape, F32)],
        scratch_shapes=[pltpu.VMEM((CONV_PAD_ROWS + lc, conv_ch), F32),
                        pltpu.VMEM((npairs, LANES, LANES), F32),
                        pltpu.VMEM((lc, conv_ch), F32),
                        pltpu.VMEM((lc, inner), F32)],
        compiler_params=_cparams(("arbitrary", "arbitrary")),
        name="ssd",
    )(xbc, zg, dtr, conv_prev, h0, cw, cb, dtb, alog, dsk, nw)


def _merge_kernel(osb_ref, ossm_ref, gsb_ref, gssm_ref, x_ref, ga_ref, lg_ref, lb_ref,
                  wsb_ref, wssm_ref, wout_ref, o_ref, *, alpha):
    merged = (jax.nn.sigmoid(gsb_ref[...].astype(F32)) * _dot(osb_ref[...], wsb_ref[...])
              + jax.nn.sigmoid(gssm_ref[...].astype(F32)) * _dot(ossm_ref[...], wssm_ref[...]))
    res = alpha * x_ref[...] + ga_ref[...] * _dot(merged.astype(BF16), wout_ref[...])
    o_ref[...] = _ln(res) * lg_ref[...] + lb_ref[...]


def _resident(shape):
    return pl.BlockSpec(shape, lambda i: (0, 0), pipeline_mode=pl.Buffered(1))


def _merge_call(osb, ossm, zg, x, gate_a, ln_g, ln_b, wsb, wssm, wout, *, alpha):
    m, d = x.shape
    sbw = osb.shape[1]
    inner = ossm.shape[1]
    tm = min(m, 512)
    gcol = inner // d
    return pl.pallas_call(
        functools.partial(_merge_kernel, alpha=alpha),
        grid=(m // tm,),
        in_specs=[pl.BlockSpec((tm, sbw), lambda i: (i, 0)),
                  pl.BlockSpec((tm, inner), lambda i: (i, 0)),
                  pl.BlockSpec((tm, d), lambda i: (i, gcol)),
                  pl.BlockSpec((tm, d), lambda i: (i, gcol + 1)),
                  pl.BlockSpec((tm, d), lambda i: (i, 0)),
                  _row_spec(tm, d, gate_a.shape[0]),
                  pl.BlockSpec((1, d), lambda i: (0, 0)),
                  pl.BlockSpec((1, d), lambda i: (0, 0)),
                  _resident(wsb.shape), _resident(wssm.shape), _resident(wout.shape)],
        out_specs=pl.BlockSpec((tm, d), lambda i: (i, 0)),
        out_shape=jax.ShapeDtypeStruct((m, d), F32),
        compiler_params=_cparams(("arbitrary",)),
        name="merge",
    )(osb, ossm, zg, zg, x, gate_a, ln_g, ln_b, wsb, wssm, wout)


def _ffn_kernel(x1_ref, sc_ref, sh_ref, gf_ref, lg_ref, lb_ref, wup_ref, wdn_ref, o_ref, *, alpha, nchunk):
    x1 = x1_ref[...]
    h2 = (_ln(x1) * (1.0 + sc_ref[...]) + sh_ref[...]).astype(BF16)
    dff = wup_ref.shape[1]
    cw = dff // nchunk
    f = None
    for ck in range(nchunk):
        a = jnp.maximum(_dot(h2, wup_ref[:, ck * cw:(ck + 1) * cw]), 0.0)
        t = _dot((a * a).astype(BF16), wdn_ref[ck * cw:(ck + 1) * cw, :])
        f = t if f is None else f + t
    o_ref[...] = _ln(alpha * x1 + gf_ref[...] * f) * lg_ref[...] + lb_ref[...]


def _ffn_call(x1, scale, shift, gate, ln_g, ln_b, wup, wdn, *, alpha):
    m, d = x1.shape
    tm = min(m, 512)
    return pl.pallas_call(
        functools.partial(_ffn_kernel, alpha=alpha, nchunk=4),
        grid=(m // tm,),
        in_specs=[pl.BlockSpec((tm, d), lambda i: (i, 0)),
                  _row_spec(tm, d, scale.shape[0]),
                  _row_spec(tm, d, shift.shape[0]),
                  _row_spec(tm, d, gate.shape[0]),
                  pl.BlockSpec((1, d), lambda i: (0, 0)),
                  pl.BlockSpec((1, d), lambda i: (0, 0)),
                  _resident(wup.shape), _resident(wdn.shape)],
        out_specs=pl.BlockSpec((tm, d), lambda i: (i, 0)),
        out_shape=jax.ShapeDtypeStruct((m, d), F32),
        compiler_params=_cparams(("arbitrary",)),
        name="ffn",
    )(x1, scale, shift, gate, ln_g, ln_b, wup, wdn)


def _layer(x, mod, cache_k, cache_v, conv_prev, ssm_prev, wts, *, batch, alpha):
    m, d = x.shape
    t = m // batch
    shift_a, scale_a, gate_a, shift_f, scale_f, gate_f = [mod[:, i * d:(i + 1) * d] for i in range(6)]
    h = _ln_mod_call(x, scale_a, shift_a)
    (q,) = _matmul_call(h, wts["w_q"], [(BF16, SB_HEAD_DIM ** -0.5 * LOG2E)], "proj_q")
    k32, kb = _matmul_call(h, wts["w_k"], [(F32, 1.0), (BF16, 1.0)], "proj_k")
    v32, vb = _matmul_call(h, wts["w_v"], [(F32, 1.0), (BF16, 1.0)], "proj_v")
    (xbc,) = _matmul_call(h, wts["w_xbc"], [(F32, 1.0)], "proj_xbc")
    (zg,) = _matmul_call(h, wts["w_zg"], [(BF16, 1.0)], "proj_zg")
    (dtr,) = _matmul_call(h, wts["w_dt"], [(F32, 1.0)], "proj_dt")
    if cache_k is None:
        o_sb = _sb_prompt_call(q, kb, vb)
    else:
        o_sb = _sb_sample_call(q, kb, vb, cache_k, cache_v, tq=t, heads=q.shape[1] // SB_HEAD_DIM)
    conv_ch = wts["conv_w"].shape[1]
    cprev = jnp.pad(conv_prev, ((0, 0), (CONV_PAD_ROWS - (CONV_WIDTH - 1), 0), (0, 0)))
    npairs = ssm_prev.shape[1] // 2
    h0 = ssm_prev.reshape(batch, npairs, LANES, SSM_STATE)
    o_ssm, h_fin = _ssd_call(xbc, zg, dtr, cprev, h0, wts["conv_w"], wts["conv_b"], wts["dt_bias"], wts["a_log"],
                             wts["d_skip"], wts["ssm_norm_w"], batch=batch, lc=min(t, 128))
    x1 = _merge_call(o_sb, o_ssm, zg, x, gate_a, wts["ln_attn_g"], wts["ln_attn_b"],
                     wts["w_branch_sb"], wts["w_branch_ssm"], wts["w_out"], alpha=alpha)
    y = _ffn_call(x1, scale_f, shift_f, gate_f, wts["ln_ffn_g"], wts["ln_ffn_b"],
                  wts["w_up"], wts["w_down"], alpha=alpha)
    conv_new = xbc.reshape(batch, t, conv_ch)[:, t - (CONV_WIDTH - 1):]
    return y, k32, v32, conv_new, h_fin.reshape(ssm_prev.shape)


def _prep_weights(l, w_in, conv_w, conv_b, dt_bias, a_log, d_skip, ssm_norm_w, w_branch_sb, w_branch_ssm,
                  w_out, ln_attn_g, ln_attn_b, w_up, w_down, ln_ffn_g, ln_ffn_b, sbw):
    d = w_in.shape[1]
    inner = ssm_norm_w.shape[1]
    conv_ch = conv_w.shape[2]
    nh = a_log.shape[1]
    o_z = 3 * sbw
    o_xbc = o_z + inner
    o_dt = o_xbc + conv_ch
    o_g = o_dt + nh
    wi = w_in[l]
    pad_h = lambda v: jnp.pad(v, (0, LANES - nh))[None, :]
    return {
        "w_q": wi[:, :sbw].astype(BF16),
        "w_k": wi[:, sbw:2 * sbw].astype(BF16),
        "w_v": wi[:, 2 * sbw:3 * sbw].astype(BF16),
        "w_xbc": wi[:, o_xbc:o_dt].astype(BF16),
        "w_zg": jnp.concatenate([wi[:, o_z:o_xbc], wi[:, o_g:]], axis=1).astype(BF16),
        "w_dt": jnp.pad(wi[:, o_dt:o_g], ((0, 0), (0, LANES - nh))).astype(BF16),
        "conv_w": conv_w[l], "conv_b": conv_b[l][None, :],
        "dt_bias": pad_h(dt_bias[l]), "a_log": pad_h(a_log[l]),
        "d_skip": jnp.repeat(d_skip[l], SSM_HEAD_DIM)[None, :],
        "ssm_norm_w": ssm_norm_w[l][None, :],
        "w_branch_sb": w_branch_sb[l].astype(BF16), "w_branch_ssm": w_branch_ssm[l].astype(BF16),
        "w_out": w_out[l].astype(BF16),
        "ln_attn_g": ln_attn_g[l][None, :], "ln_attn_b": ln_attn_b[l][None, :],
        "w_up": w_up[l].astype(BF16), "w_down": w_down[l].astype(BF16),
        "ln_ffn_g": ln_ffn_g[l][None, :], "ln_ffn_b": ln_ffn_b[l][None, :],
    }


def kernel(x_prompt, x_sample, cache_sb_k, cache_sb_v, state_conv, state_ssm, c_prompt, c_sample, w_ada, b_ada, w_in, conv_w, conv_b, dt_bias, a_log, d_skip, ssm_norm_w, w_branch_sb, w_branch_ssm, w_out, ln_attn_g, ln_attn_b, w_up, w_down, ln_ffn_g, ln_ffn_b):
    depth = w_ada.shape[0]
    alpha = (2 * depth) ** 0.25
    n_p, t_p, d = x_prompt.shape
    n_s, t_s, _ = x_sample.shape
    past = cache_sb_k.shape[2]
    sbw = cache_sb_k.shape[3] * cache_sb_k.shape[4]
    conv_ch = conv_w.shape[2]
    y_p = x_prompt.reshape(n_p * t_p, d)
    y_s = x_sample.reshape(n_s * t_s, d)
    c_all = jnp.concatenate([c_prompt, c_sample], axis=0)
    rows = c_all.shape[0]
    c_all = jnp.pad(c_all, ((0, -rows % 8), (0, 0)))
    new_p, new_s = [], []
    for l in range(depth):
        wts = _prep_weights(l, w_in, conv_w, conv_b, dt_bias, a_log, d_skip, ssm_norm_w, w_branch_sb,
                            w_branch_ssm, w_out, ln_attn_g, ln_attn_b, w_up, w_down, ln_ffn_g, ln_ffn_b, sbw)
        mod = _mod_call(c_all, w_ada[l], b_ada[l][None, :])
        mod_p = mod[:n_p] if n_p == 1 else jnp.repeat(mod[:n_p], t_p, axis=0)
        mod_s = jnp.repeat(mod[n_p:n_p + n_s], t_s, axis=0)
        conv0 = jnp.zeros((n_p, CONV_WIDTH - 1, conv_ch), F32)
        ssm0 = jnp.zeros((n_p,) + state_ssm.shape[2:], F32)
        y_p, k_p, v_p, cv_p, s_p = _layer(y_p, mod_p, None, None, conv0, ssm0, wts, batch=n_p, alpha=alpha)
        y_s, k_s, v_s, cv_s, s_s = _layer(y_s, mod_s, jnp.transpose(cache_sb_k[l], (0, 2, 3, 1)),
                                          jnp.transpose(cache_sb_v[l], (0, 2, 3, 1)), state_conv[l], state_ssm[l],
                                          wts, batch=n_s, alpha=alpha)
        hshape = cache_sb_k.shape[3:]
        new_p.append((k_p.reshape((n_p, t_p) + hshape), v_p.reshape((n_p, t_p) + hshape), cv_p, s_p))
        new_s.append((k_s.reshape((n_s, t_s) + hshape), v_s.reshape((n_s, t_s) + hshape), cv_s, s_s))
    stack = lambda lst, i: jnp.stack([e[i] for e in lst])
    return (y_p.reshape(n_p, t_p, d), y_s.reshape(n_s, t_s, d),
            stack(new_p, 0), stack(new_p, 1), stack(new_p, 2), stack(new_p, 3),
            stack(new_s, 0), stack(new_s, 1), stack(new_s, 2), stack(new_s, 3))
```

```python
import functools

import jax
import jax.numpy as jnp
from jax import lax
from jax.experimental import pallas as pl
from jax.experimental.pallas import tpu as pltpu

F32 = jnp.float32
BF16 = jnp.bfloat16

SB_HEAD_DIM = 64
SSM_HEAD_DIM = 64
SSM_STATE = 128
SSM_GROUPS = 8
CONV_WIDTH = 4
LN_EPS = 1e-5
LANES = 128
SUBLANES = 8
CONV_PAD_ROWS = 8
VMEM_LIMIT = 56 * 1024 * 1024
LOG2E = 1.4426950408889634
SB_DEAD_LOG2 = 150.0


def _cparams(sem):
    return pltpu.CompilerParams(dimension_semantics=sem, vmem_limit_bytes=VMEM_LIMIT)


def _dot(a, b):
    return jnp.dot(a, b, preferred_element_type=F32)


def _dot_nt(a, b):
    return lax.dot_general(a, b, (((1,), (1,)), ((), ())), preferred_element_type=F32)


def _split_bf16(x, n):
    parts, r = [], x
    for _ in range(n):
        p = r.astype(BF16)
        parts.append(p)
        r = r - p.astype(F32)
    return parts


def _dot_exact_rhs(a_bf16, b_f32, n=3):
    out = None
    for p in _split_bf16(b_f32, n):
        t = _dot(a_bf16, p)
        out = t if out is None else out + t
    return out


def _dot_exact_lhs(a_f32, b_bf16, n):
    out = None
    for p in _split_bf16(a_f32, n):
        t = _dot(p, b_bf16)
        out = t if out is None else out + t
    return out


def _transpose_exact(eye_bf16, b_f32, n=3):
    out = None
    for p in _split_bf16(b_f32, n):
        t = _dot_nt(eye_bf16, p)
        out = t if out is None else out + t
    return out


def _softplus(x):
    return jnp.maximum(x, 0.0) + jnp.log(1.0 + jnp.exp(-jnp.abs(x)))


def _silu(x):
    h = 0.5 * x
    return h + h * jnp.tanh(h)


def _ln(x):
    mu = jnp.mean(x, axis=-1, keepdims=True)
    xc = x - mu
    var = jnp.mean(xc * xc, axis=-1, keepdims=True)
    return xc * lax.rsqrt(var + LN_EPS)


def _causal_conv_silu(xp, cw_ref, cb_ref):
    conv = cb_ref[...] + cw_ref[CONV_WIDTH - 1:CONV_WIDTH, :] * xp[CONV_PAD_ROWS:, :]
    for back in range(1, CONV_WIDTH):
        w = CONV_WIDTH - 1 - back
        conv = conv + cw_ref[w:w + 1, :] * pltpu.roll(xp, back, axis=0)[CONV_PAD_ROWS:, :]
    return _silu(conv)


def _eye(n, dtype):
    r = lax.broadcasted_iota(jnp.int32, (n, n), 0)
    c = lax.broadcasted_iota(jnp.int32, (n, n), 1)
    return jnp.where(r == c, 1.0, 0.0).astype(dtype)


MOD_PIECES = ("shift_a", "scale_a", "gate_a", "shift_f", "scale_f", "gate_f")


def _mod_spec(d, piece, batch, row0):
    col = MOD_PIECES.index(piece)
    if batch == 1:
        assert row0 % SUBLANES == 0
        return pl.BlockSpec((SUBLANES, d), lambda i: (row0 // SUBLANES, col))
    assert row0 == 0
    return pl.BlockSpec((batch, d), lambda i: (0, col))


def _mod_rows(ref, batch, m):
    if batch == 1:
        return ref[0:1, :]
    v = ref[...]
    return jnp.broadcast_to(v[:, None, :], (batch, m // batch, v.shape[1])).reshape(m, v.shape[1])


def _row_tile(m, batch, cap):
    tm = min(m, cap)
    assert batch == 1 or tm == m
    return tm


def _mod_kernel(c_ref, w_ref, b_ref, o_ref):
    s = _silu(c_ref[...])
    s_hi, s_lo = _split_bf16(s, 2)
    w_hi, w_lo = _split_bf16(w_ref[...], 2)
    o_ref[...] = _dot(s_hi, w_hi) + _dot(s_hi, w_lo) + _dot(s_lo, w_hi) + b_ref[...]


def _mod_call(c, w, b):
    r, d = c.shape
    n = w.shape[1]
    tn = 1024
    return pl.pallas_call(
        _mod_kernel,
        grid=(n // tn,),
        in_specs=[pl.BlockSpec((r, d), lambda j: (0, 0)),
                  pl.BlockSpec((d, tn), lambda j: (0, j)),
                  pl.BlockSpec((1, tn), lambda j: (0, j))],
        out_specs=pl.BlockSpec((r, tn), lambda j: (0, j)),
        out_shape=jax.ShapeDtypeStruct((r, n), F32),
        compiler_params=_cparams(("arbitrary",)),
        name="ada_mod",
    )(c, w, b)


def _ln_mod_kernel(x_ref, sc_ref, sh_ref, o_ref, *, batch):
    m = x_ref.shape[0]
    o_ref[...] = (_ln(x_ref[...]) * (1.0 + _mod_rows(sc_ref, batch, m))
                  + _mod_rows(sh_ref, batch, m)).astype(o_ref.dtype)


def _ln_mod_call(x, mod, *, batch, row0):
    m, d = x.shape
    tm = _row_tile(m, batch, 512)
    return pl.pallas_call(
        functools.partial(_ln_mod_kernel, batch=batch),
        grid=(m // tm,),
        in_specs=[pl.BlockSpec((tm, d), lambda i: (i, 0)),
                  _mod_spec(d, "scale_a", batch, row0),
                  _mod_spec(d, "shift_a", batch, row0)],
        out_specs=pl.BlockSpec((tm, d), lambda i: (i, 0)),
        out_shape=jax.ShapeDtypeStruct((m, d), BF16),
        compiler_params=_cparams(("arbitrary",)),
        name="ln_mod",
    )(x, mod, mod)


def _matmul_kernel(x_ref, w_ref, *refs, scales):
    o_refs = refs[:len(scales)]
    if w_ref.dtype == BF16:
        w = w_ref[...]
    else:
        wb_ref = refs[len(scales)]

        @pl.when(pl.program_id(1) == 0)
        def _():
            wb_ref[...] = w_ref[...].astype(BF16)

        w = wb_ref[...]
    acc = _dot_nt(x_ref[...], w)
    for o_ref, s in zip(o_refs, scales):
        o_ref[...] = (acc if s == 1.0 else acc * s).astype(o_ref.dtype)


def _matmul_call(x, wt, outs, name, *, row0=0, nrows=None):
    m, k = x.shape
    n = wt.shape[0] - row0 if nrows is None else nrows
    tm = min(m, 1024)
    tn = min(n, 1024)
    assert row0 % tn == 0 and n % tn == 0
    rb0 = row0 // tn
    res = pl.pallas_call(
        functools.partial(_matmul_kernel, scales=tuple(s for _, s in outs)),
        grid=(n // tn, m // tm),
        in_specs=[pl.BlockSpec((tm, k), lambda j, i: (i, 0)),
                  pl.BlockSpec((tn, k), lambda j, i: (rb0 + j, 0))],
        out_specs=[pl.BlockSpec((tm, tn), lambda j, i: (i, j)) for _ in outs],
        out_shape=[jax.ShapeDtypeStruct((m, n), dt) for dt, _ in outs],
        scratch_shapes=[] if wt.dtype == BF16 else [pltpu.VMEM((tn, k), BF16)],
        compiler_params=_cparams(("arbitrary", "arbitrary")),
        name=name,
    )(x, wt)
    return res


def _suffix_matrix(tk):
    r = lax.broadcasted_iota(jnp.int32, (tk, tk), 0)
    c = lax.broadcasted_iota(jnp.int32, (tk, tk), 1)
    return jnp.where(r >= c, 1.0, 0.0).astype(BF16)


def _stacked_row_ids(n, tq, tk):
    one = lax.broadcasted_iota(jnp.int32, (tq, tk), 0)
    return jnp.concatenate([one] * n, axis=0)


def _sb_weights(z, c_run, suffix, visible):
    neg_abs = lax.bitcast_convert_type(lax.bitcast_convert_type(z, jnp.int32) | jnp.int32(-2 ** 31), F32)
    sp = jnp.maximum(z, 0.0) + jnp.log(1.0 + jnp.exp2(neg_abs)) * LOG2E
    if visible is not None:
        sp = jnp.where(visible, sp, 0.0)
    s = _dot(sp.astype(BF16), suffix) + c_run
    w = jnp.exp2(z - s)
    if visible is not None:
        w = jnp.where(visible, w, 0.0)
    return w.astype(BF16), s[:, 0:1]


def _sb_live(c_run):
    return (jnp.min(c_run) <= SB_DEAD_LOG2).astype(jnp.int32)


def _sb_prompt_kernel(q_ref, k_ref, v_ref, o_ref, *, tq, npair):
    i = pl.program_id(1)
    low = lax.broadcasted_iota(jnp.int32, (tq, LANES), 1) < SB_HEAD_DIM
    suffix = _suffix_matrix(tq)
    rows = _stacked_row_ids(2, tq, tq)
    cols = lax.broadcasted_iota(jnp.int32, (2 * tq, tq), 1)
    lanes = [slice(p * LANES, (p + 1) * LANES) for p in range(npair)]
    q_ab = []
    for sl in lanes:
        qf = q_ref[:, sl].astype(F32)
        q_ab.append(jnp.concatenate([jnp.where(low, qf, 0.0), jnp.where(low, 0.0, qf)], axis=0).astype(BF16))

    def block(at, carry, visible):
        out = []
        for p, (c_run, acc) in enumerate(carry):
            w, c_run = _sb_weights(_dot_nt(q_ab[p], k_ref[pl.ds(at, tq), lanes[p]]), c_run, suffix, visible)
            pv = _dot(w, v_ref[pl.ds(at, tq), lanes[p]])
            out.append((c_run, acc + jnp.where(low, pv[:tq], pv[tq:])))
        return out

    def live(carry):
        return _sb_live(functools.reduce(jnp.minimum, [c for c, _ in carry]))

    carry = [(jnp.zeros((2 * tq, 1), F32), jnp.zeros((tq, LANES), F32)) for _ in range(npair)]
    carry = block(pl.multiple_of(i * tq, tq), carry, cols < rows)
    carry = block(pl.multiple_of(jnp.maximum(i - 1, 0) * tq, tq), carry, rows >= jnp.where(i > 0, 0, tq))

    def cond(st):
        return jnp.logical_and(st[0] >= 0, st[1] > 0)

    def body(st):
        j, _, carry = st
        carry = block(pl.multiple_of(j * tq, tq), carry, None)
        return j - 1, live(carry), carry

    _, _, carry = lax.while_loop(cond, body, (i - 2, live(carry), carry))
    for p, (_, acc) in enumerate(carry):
        o_ref[:, lanes[p]] = acc.astype(o_ref.dtype)


def _sb_prompt_call(q, k, v):
    t, w = q.shape
    tq = min(t, 256)
    npair = 2 if w % (2 * LANES) == 0 else 1
    wb = npair * LANES
    return pl.pallas_call(
        functools.partial(_sb_prompt_kernel, tq=tq, npair=npair),
        grid=(w // wb, t // tq),
        in_specs=[pl.BlockSpec((tq, wb), lambda p, i: (i, p)),
                  pl.BlockSpec((t, wb), lambda p, i: (0, p)),
                  pl.BlockSpec((t, wb), lambda p, i: (0, p))],
        out_specs=pl.BlockSpec((tq, wb), lambda p, i: (i, p)),
        out_shape=jax.ShapeDtypeStruct((t, w), BF16),
        compiler_params=_cparams(("arbitrary", "arbitrary")),
        name="sb_prompt",
    )(q, k, v)


def _sb_sample_kernel(q_ref, kn_ref, vn_ref, kc_ref, vc_ref, o_ref, c_ref, acc_ref, live_ref, *, heads, tq, tk):
    j = pl.program_id(1)
    hd = SB_HEAD_DIM

    def stacked_block(qk_of, pv_of, suffix, visible):
        q = q_ref[...]
        z = jnp.concatenate([qk_of(q[:, h * hd:(h + 1) * hd], h) for h in range(heads)], axis=0)
        w, c_run = _sb_weights(z, c_ref[...], suffix, visible)
        acc_ref[...] += jnp.concatenate([pv_of(w[h * tq:(h + 1) * tq], h) for h in range(heads)], axis=0)
        c_ref[...] = c_run
        live_ref[0] = _sb_live(c_run)

    @pl.when(j == 0)
    def _():
        c_ref[...] = jnp.zeros_like(c_ref)
        acc_ref[...] = jnp.zeros_like(acc_ref)
        rows = _stacked_row_ids(heads, tq, tq)
        cols = lax.broadcasted_iota(jnp.int32, (heads * tq, tq), 1)
        stacked_block(lambda qh, h: _dot_nt(qh, kn_ref[:, h * hd:(h + 1) * hd]),
                      lambda wh, h: _dot(wh, vn_ref[:, h * hd:(h + 1) * hd]),
                      _suffix_matrix(tq), cols < rows)

    @pl.when(jnp.logical_and(j > 0, live_ref[0] > 0))
    def _():
        stacked_block(lambda qh, h: _dot(qh, kc_ref[0, h].astype(BF16)),
                      lambda wh, h: _dot_nt(wh, vc_ref[0, h].astype(BF16)),
                      _suffix_matrix(tk), None)

    @pl.when(j == pl.num_programs(1) - 1)
    def _():
        acc = acc_ref[...]
        o_ref[...] = jnp.concatenate([acc[h * tq:(h + 1) * tq] for h in range(heads)], axis=1).astype(o_ref.dtype)


def _sb_sample_call(q, kn, vn, kc, vc, *, tq, heads):
    m, w = q.shape
    b, _, _, past = kc.shape
    tk = min(past, 256)
    nblk = past // tk
    new_spec = pl.BlockSpec((tq, w), lambda bi, j: (bi, 0))
    cache_spec = pl.BlockSpec((1, heads, SB_HEAD_DIM, tk),
                              lambda bi, j: (bi, 0, 0, jnp.minimum(nblk - j, nblk - 1)))
    return pl.pallas_call(
        functools.partial(_sb_sample_kernel, heads=heads, tq=tq, tk=tk),
        grid=(b, nblk + 1),
        in_specs=[new_spec, new_spec, new_spec, cache_spec, cache_spec],
        out_specs=new_spec,
        out_shape=jax.ShapeDtypeStruct((m, w), BF16),
        scratch_shapes=[pltpu.VMEM((heads * tq, 1), F32),
                        pltpu.VMEM((heads * tq, SB_HEAD_DIM), F32),
                        pltpu.SMEM((1,), jnp.int32)],
        compiler_params=_cparams(("arbitrary", "arbitrary")),
        name="sb_sample",
    )(q, kn, vn, kc, vc)


def _ssd_kernel(xbc_ref, z_ref, dt_ref, cprev_ref, h0_ref, cw_ref, cb_ref, dtb_ref, alog_ref, dsk_ref,
                nw_ref, o_ref, hout_ref, xpad_ref, st_ref, act_ref, y_ref, *, lc, inner, npairs):
    ci = pl.program_id(1)
    eye = _eye(LANES, BF16)

    @pl.when(ci == 0)
    def _():
        xpad_ref[0:CONV_PAD_ROWS, :] = cprev_ref[0]
        for p in range(npairs):
            st_ref[p] = h0_ref[0, p].T

    @pl.when(ci > 0)
    def _():
        xpad_ref[0:CONV_PAD_ROWS, :] = xpad_ref[lc:lc + CONV_PAD_ROWS, :]

    xpad_ref[CONV_PAD_ROWS:CONV_PAD_ROWS + lc, :] = xbc_ref[...]
    act_ref[...] = _causal_conv_silu(xpad_ref[...], cw_ref, cb_ref)

    dtv = _softplus(dt_ref[...] + dtb_ref[...])
    a_neg = -jnp.exp(alog_ref[...])
    d_a = dtv * a_neg
    r = lax.broadcasted_iota(jnp.int32, (lc, lc), 0)
    c = lax.broadcasted_iota(jnp.int32, (lc, lc), 1)
    causal = r >= c
    prefix = jnp.where(causal, 1.0, 0.0).astype(BF16)
    acausal_off = jnp.where(causal, 0.0, -1e30)
    a_cs = _dot_exact_rhs(prefix, d_a)
    a_cs_t = _transpose_exact(eye, a_cs)
    total = a_cs[lc - 1:lc, :]
    ea = jnp.exp(a_cs)
    to_end = jnp.exp(total - a_cs)
    chunk_decay = jnp.exp(total)
    low = lax.broadcasted_iota(jnp.int32, (lc, LANES), 1) < SSM_HEAD_DIM
    low1 = lax.broadcasted_iota(jnp.int32, (1, LANES), 1) < SSM_HEAD_DIM

    def pair_cols(arr, ha):
        return jnp.where(low, arr[:, ha:ha + 1], arr[:, ha + 1:ha + 2])

    pairs_per_group = npairs // SSM_GROUPS
    for g in range(SSM_GROUPS):
        b_bf = act_ref[:, inner + g * SSM_STATE:inner + (g + 1) * SSM_STATE].astype(BF16)
        c_off = inner + SSM_GROUPS * SSM_STATE
        c_bf = act_ref[:, c_off + g * SSM_STATE:c_off + (g + 1) * SSM_STATE].astype(BF16)
        cb = _dot_nt(c_bf, b_bf)
        b_t = _dot_nt(eye, b_bf).astype(BF16)
        for kk in range(pairs_per_group):
            p = g * pairs_per_group + kk
            ha = 2 * p
            sl = slice(p * LANES, (p + 1) * LANES)
            x = act_ref[:, sl]
            xdt = x * pair_cols(dtv, ha)
            xdt_bf = xdt.astype(BF16)
            ys = []
            for h in (ha, ha + 1):
                seg = a_cs[:, h:h + 1] - a_cs_t[h:h + 1, :]
                decay = jnp.exp(seg + acausal_off)
                ys.append(_dot((cb * decay).astype(BF16), xdt_bf))
            st = st_ref[p]
            y_off = _dot(c_bf, st.astype(BF16)) * pair_cols(ea, ha)
            y_ref[:, sl] = jnp.where(low, ys[0], ys[1]) + y_off + dsk_ref[:, sl] * x
            xw = (xdt * pair_cols(to_end, ha)).astype(BF16)
            dec = jnp.where(low1, chunk_decay[:, ha:ha + 1], chunk_decay[:, ha + 1:ha + 2])
            st_ref[p] = st * dec + _dot(b_t, xw)

    gy = y_ref[...] * _silu(z_ref[...].astype(F32))
    gw = inner // SSM_GROUPS
    for g in range(SSM_GROUPS):
        seg = gy[:, g * gw:(g + 1) * gw]
        ms = jnp.mean(seg * seg, axis=-1, keepdims=True)
        o_ref[:, g * gw:(g + 1) * gw] = (seg * lax.rsqrt(ms + LN_EPS) * nw_ref[:, g * gw:(g + 1) * gw]).astype(o_ref.dtype)

    @pl.when(ci == pl.num_programs(1) - 1)
    def _():
        for p in range(npairs):
            hout_ref[0, p] = st_ref[p].T


def _ssd_call(xbc, zg, dtr, conv_prev, h0, cw, cb, dtb, alog, dsk, nw, *, batch, lc):
    m = xbc.shape[0]
    conv_ch = cw.shape[1]
    inner = nw.shape[1]
    npairs = h0.shape[1]
    nc = m // batch // lc
    row = lambda b, c: b * nc + c
    const = lambda b, c: (0, 0)
    return pl.pallas_call(
        functools.partial(_ssd_kernel, lc=lc, inner=inner, npairs=npairs),
        grid=(batch, nc),
        in_specs=[pl.BlockSpec((lc, conv_ch), lambda b, c: (row(b, c), 0)),
                  pl.BlockSpec((lc, inner), lambda b, c: (row(b, c), 0)),
                  pl.BlockSpec((lc, LANES), lambda b, c: (row(b, c), 0)),
                  pl.BlockSpec((1, CONV_PAD_ROWS, conv_ch), lambda b, c: (b, 0, 0)),
                  pl.BlockSpec((1, npairs, LANES, LANES), lambda b, c: (b, 0, 0, 0)),
                  pl.BlockSpec((CONV_WIDTH, conv_ch), const),
                  pl.BlockSpec((1, conv_ch), const),
                  pl.BlockSpec((1, LANES), const),
                  pl.BlockSpec((1, LANES), const),
                  pl.BlockSpec((1, inner), const),
                  pl.BlockSpec((1, inner), const)],
        out_specs=[pl.BlockSpec((lc, inner), lambda b, c: (row(b, c), 0)),
                   pl.BlockSpec((1, npairs, LANES, LANES), lambda b, c: (b, 0, 0, 0))],
        out_shape=[jax.ShapeDtypeStruct((m, inner), BF16),
                   jax.ShapeDtypeStruct(h0.shape, F32)],
        scratch_shapes=[pltpu.VMEM((CONV_PAD_ROWS + lc, conv_ch), F32),
                        pltpu.VMEM((npairs, LANES, LANES), F32),
                        pltpu.VMEM((lc, conv_ch), F32),
                        pltpu.VMEM((lc, inner), F32)],
        compiler_params=_cparams(("arbitrary", "arbitrary")),
        name="ssd",
    )(xbc, zg, dtr, conv_prev, h0, cw, cb, dtb, alog, dsk, nw)


def _merge_kernel(osb_ref, ossm_ref, gsb_ref, gssm_ref, x_ref, ga_ref, lg_ref, lb_ref,
                  wsb_ref, wssm_ref, wout_ref, o_ref, *, alpha, batch):
    merged = (jax.nn.sigmoid(gsb_ref[...].astype(F32)) * _dot(osb_ref[...], wsb_ref[...])
              + jax.nn.sigmoid(gssm_ref[...].astype(F32)) * _dot(ossm_ref[...], wssm_ref[...]))
    gate = _mod_rows(ga_ref, batch, x_ref.shape[0])
    res = alpha * x_ref[...] + gate * _dot(merged.astype(BF16), wout_ref[...])
    o_ref[...] = _ln(res) * lg_ref[...] + lb_ref[...]


def _resident(shape):
    return pl.BlockSpec(shape, lambda i: (0, 0), pipeline_mode=pl.Buffered(1))


def _merge_call(osb, ossm, zg, x, mod, ln_g, ln_b, wsb, wssm, wout, *, alpha, batch, row0):
    m, d = x.shape
    sbw = osb.shape[1]
    inner = ossm.shape[1]
    tm = _row_tile(m, batch, 512)
    gcol = inner // d
    return pl.pallas_call(
        functools.partial(_merge_kernel, alpha=alpha, batch=batch),
        grid=(m // tm,),
        in_specs=[pl.BlockSpec((tm, sbw), lambda i: (i, 0)),
                  pl.BlockSpec((tm, inner), lambda i: (i, 0)),
                  pl.BlockSpec((tm, d), lambda i: (i, gcol)),
                  pl.BlockSpec((tm, d), lambda i: (i, gcol + 1)),
                  pl.BlockSpec((tm, d), lambda i: (i, 0)),
                  _mod_spec(d, "gate_a", batch, row0),
                  pl.BlockSpec((1, d), lambda i: (0, 0)),
                  pl.BlockSpec((1, d), lambda i: (0, 0)),
                  _resident(wsb.shape), _resident(wssm.shape), _resident(wout.shape)],
        out_specs=pl.BlockSpec((tm, d), lambda i: (i, 0)),
        out_shape=jax.ShapeDtypeStruct((m, d), F32),
        compiler_params=_cparams(("arbitrary",)),
        name="merge",
    )(osb, ossm, zg, zg, x, mod, ln_g, ln_b, wsb, wssm, wout)


def _ffn_kernel(x1_ref, sc_ref, sh_ref, gf_ref, lg_ref, lb_ref, wup_ref, wdn_ref, o_ref, *, alpha, nchunk, batch):
    x1 = x1_ref[...]
    m = x1.shape[0]
    h2 = (_ln(x1) * (1.0 + _mod_rows(sc_ref, batch, m)) + _mod_rows(sh_ref, batch, m)).astype(BF16)
    dff = wup_ref.shape[1]
    cw = dff // nchunk
    f = None
    for ck in range(nchunk):
        a = jnp.maximum(_dot(h2, wup_ref[:, ck * cw:(ck + 1) * cw]), 0.0)
        t = _dot((a * a).astype(BF16), wdn_ref[ck * cw:(ck + 1) * cw, :])
        f = t if f is None else f + t
    o_ref[...] = _ln(alpha * x1 + _mod_rows(gf_ref, batch, m) * f) * lg_ref[...] + lb_ref[...]


def _ffn_call(x1, mod, ln_g, ln_b, wup, wdn, *, alpha, batch, row0):
    m, d = x1.shape
    tm = _row_tile(m, batch, 1024)
    return pl.pallas_call(
        functools.partial(_ffn_kernel, alpha=alpha, nchunk=4, batch=batch),
        grid=(m // tm,),
        in_specs=[pl.BlockSpec((tm, d), lambda i: (i, 0)),
                  _mod_spec(d, "scale_f", batch, row0),
                  _mod_spec(d, "shift_f", batch, row0),
                  _mod_spec(d, "gate_f", batch, row0),
                  pl.BlockSpec((1, d), lambda i: (0, 0)),
                  pl.BlockSpec((1, d), lambda i: (0, 0)),
                  _resident(wup.shape), _resident(wdn.shape)],
        out_specs=pl.BlockSpec((tm, d), lambda i: (i, 0)),
        out_shape=jax.ShapeDtypeStruct((m, d), F32),
        compiler_params=_cparams(("arbitrary",)),
        name="ffn",
    )(x1, mod, mod, mod, ln_g, ln_b, wup, wdn)


def _layer(x, mod, cache_k, cache_v, conv_prev, ssm_prev, wts, *, batch, row0, alpha):
    m, d = x.shape
    t = m // batch
    h = _ln_mod_call(x, mod, batch=batch, row0=row0)
    def proj(cols, outs, name):
        return _matmul_call(h, wts["w_in_t"], outs, name, row0=wts[cols][0], nrows=wts[cols][1])

    (q,) = proj("q_cols", [(BF16, SB_HEAD_DIM ** -0.5 * LOG2E)], "proj_q")
    k32, kb = proj("k_cols", [(F32, 1.0), (BF16, 1.0)], "proj_k")
    v32, vb = proj("v_cols", [(F32, 1.0), (BF16, 1.0)], "proj_v")
    conv_ch = wts["conv_w"].shape[1]
    cprev = jnp.pad(conv_prev, ((0, 0), (CONV_PAD_ROWS - (CONV_WIDTH - 1), 0), (0, 0)))
    (xbc,) = proj("xbc_cols", [(F32, 1.0)], "proj_xbc")
    conv_new = xbc.reshape(batch, t, conv_ch)[:, t - (CONV_WIDTH - 1):]
    (zg,) = _matmul_call(h, wts["w_zg"], [(BF16, 1.0)], "proj_zg")
    (dtr,) = _matmul_call(h, wts["w_dt"], [(F32, 1.0)], "proj_dt")
    if cache_k is None:
        o_sb = _sb_prompt_call(q, kb, vb)
    else:
        o_sb = _sb_sample_call(q, kb, vb, cache_k, cache_v, tq=t, heads=q.shape[1] // SB_HEAD_DIM)
    npairs = ssm_prev.shape[1] // 2
    h0 = ssm_prev.reshape(batch, npairs, LANES, SSM_STATE)
    o_ssm, h_fin = _ssd_call(xbc, zg, dtr, cprev, h0, wts["conv_w"], wts["conv_b"], wts["dt_bias"], wts["a_log"],
                             wts["d_skip"], wts["ssm_norm_w"], batch=batch, lc=min(t, 128))
    x1 = _merge_call(o_sb, o_ssm, zg, x, mod, wts["ln_attn_g"], wts["ln_attn_b"],
                     wts["w_branch_sb"], wts["w_branch_ssm"], wts["w_out"], alpha=alpha, batch=batch, row0=row0)
    y = _ffn_call(x1, mod, wts["ln_ffn_g"], wts["ln_ffn_b"], wts["w_up"], wts["w_down"],
                  alpha=alpha, batch=batch, row0=row0)
    return y, k32, v32, conv_new, h_fin.reshape(ssm_prev.shape)


def _prep_weights(l, w_in, conv_w, conv_b, dt_bias, a_log, d_skip, ssm_norm_w, w_branch_sb, w_branch_ssm,
                  w_out, ln_attn_g, ln_attn_b, w_up, w_down, ln_ffn_g, ln_ffn_b, sbw):
    d = w_in.shape[1]
    inner = ssm_norm_w.shape[1]
    conv_ch = conv_w.shape[2]
    nh = a_log.shape[1]
    o_z = 3 * sbw
    o_xbc = o_z + inner
    o_dt = o_xbc + conv_ch
    o_g = o_dt + nh
    wit = jnp.transpose(w_in[l])
    pad_h = lambda v: jnp.pad(v, (0, LANES - nh))[None, :]
    return {
        "w_in_t": wit, "q_cols": (0, sbw), "k_cols": (sbw, sbw), "v_cols": (2 * sbw, sbw),
        "xbc_cols": (o_xbc, conv_ch),
        "w_zg": jnp.concatenate([wit[o_z:o_xbc], wit[o_g:]], axis=0),
        "w_dt": jnp.pad(wit[o_dt:o_g], ((0, LANES - nh), (0, 0))),
        "conv_w": conv_w[l], "conv_b": conv_b[l][None, :],
        "dt_bias": pad_h(dt_bias[l]), "a_log": pad_h(a_log[l]),
        "d_skip": jnp.repeat(d_skip[l], SSM_HEAD_DIM)[None, :],
        "ssm_norm_w": ssm_norm_w[l][None, :],
        "w_branch_sb": w_branch_sb[l].astype(BF16), "w_branch_ssm": w_branch_ssm[l].astype(BF16),
        "w_out": w_out[l].astype(BF16),
        "ln_attn_g": ln_attn_g[l][None, :], "ln_attn_b": ln_attn_b[l][None, :],
        "w_up": w_up[l].astype(BF16), "w_down": w_down[l].astype(BF16),
        "ln_ffn_g": ln_ffn_g[l][None, :], "ln_ffn_b": ln_ffn_b[l][None, :],
    }


def kernel(x_prompt, x_sample, cache_sb_k, cache_sb_v, state_conv, state_ssm, c_prompt, c_sample, w_ada, b_ada, w_in, conv_w, conv_b, dt_bias, a_log, d_skip, ssm_norm_w, w_branch_sb, w_branch_ssm, w_out, ln_attn_g, ln_attn_b, w_up, w_down, ln_ffn_g, ln_ffn_b):
    depth = w_ada.shape[0]
    alpha = (2 * depth) ** 0.25
    n_p, t_p, d = x_prompt.shape
    n_s, t_s, _ = x_sample.shape
    past = cache_sb_k.shape[2]
    sbw = cache_sb_k.shape[3] * cache_sb_k.shape[4]
    conv_ch = conv_w.shape[2]
    y_p = x_prompt.reshape(n_p * t_p, d)
    y_s = x_sample.reshape(n_s * t_s, d)
    assert n_p == 1
    row0_p = -(-n_s // SUBLANES) * SUBLANES
    c_all = jnp.concatenate([jnp.pad(c_sample, ((0, row0_p - n_s), (0, 0))),
                             jnp.pad(c_prompt, ((0, SUBLANES - n_p), (0, 0)))], axis=0)
    new_p, new_s = [], []
    for l in range(depth):
        wts = _prep_weights(l, w_in, conv_w, conv_b, dt_bias, a_log, d_skip, ssm_norm_w, w_branch_sb,
                            w_branch_ssm, w_out, ln_attn_g, ln_attn_b, w_up, w_down, ln_ffn_g, ln_ffn_b, sbw)
        mod = _mod_call(c_all, w_ada[l], b_ada[l][None, :])
        conv0 = jnp.zeros((n_p, CONV_WIDTH - 1, conv_ch), F32)
        ssm0 = jnp.zeros((n_p,) + state_ssm.shape[2:], F32)
        y_p, k_p, v_p, cv_p, s_p = _layer(y_p, mod, None, None, conv0, ssm0, wts,
                                          batch=n_p, row0=row0_p, alpha=alpha)
        y_s, k_s, v_s, cv_s, s_s = _layer(y_s, mod, jnp.transpose(cache_sb_k[l], (0, 2, 3, 1)),
                                          jnp.transpose(cache_sb_v[l], (0, 2, 3, 1)), state_conv[l], state_ssm[l],
                                          wts, batch=n_s, row0=0, alpha=alpha)
        hshape = cache_sb_k.shape[3:]
        new_p.append((k_p.reshape((n_p, t_p) + hshape), v_p.reshape((n_p, t_p) + hshape), cv_p, s_p))
        new_s.append((k_s.reshape((n_s, t_s) + hshape), v_s.reshape((n_s, t_s) + hshape), cv_s, s_s))
    stack = lambda lst, i: jnp.stack([e[i] for e in lst])
    return (y_p.reshape(n_p, t_p, d), y_s.reshape(n_s, t_s, d),
            stack(new_p, 0), stack(new_p, 1), stack(new_p, 2), stack(new_p, 3),
            stack(new_s, 0), stack(new_s, 1), stack(new_s, 2), stack(new_s, 3))
```

```python
import functools

import jax
import jax.numpy as jnp
from jax import lax
from jax.experimental import pallas as pl
from jax.experimental.pallas import tpu as pltpu

F32 = jnp.float32
BF16 = jnp.bfloat16

SB_HEAD_DIM = 64
SSM_HEAD_DIM = 64
SSM_STATE = 128
SSM_GROUPS = 8
CONV_WIDTH = 4
LN_EPS = 1e-5
LANES = 128
SUBLANES = 8
CONV_PAD_ROWS = 8
VMEM_LIMIT = 56 * 1024 * 1024
LOG2E = 1.4426950408889634
SB_DEAD_LOG2 = 150.0


def _cparams(sem):
    return pltpu.CompilerParams(dimension_semantics=sem, vmem_limit_bytes=VMEM_LIMIT)


def _dot(a, b):
    return jnp.dot(a, b, preferred_element_type=F32)


def _dot_nt(a, b):
    return lax.dot_general(a, b, (((1,), (1,)), ((), ())), preferred_element_type=F32)


def _split_bf16(x, n):
    parts, r = [], x
    for _ in range(n):
        p = r.astype(BF16)
        parts.append(p)
        r = r - p.astype(F32)
    return parts


def _dot_exact_rhs(a_bf16, b_f32, n=3):
    out = None
    for p in _split_bf16(b_f32, n):
        t = _dot(a_bf16, p)
        out = t if out is None else out + t
    return out


def _dot_exact_lhs(a_f32, b_bf16, n):
    out = None
    for p in _split_bf16(a_f32, n):
        t = _dot(p, b_bf16)
        out = t if out is None else out + t
    return out


def _transpose_exact(eye_bf16, b_f32, n=3):
    out = None
    for p in _split_bf16(b_f32, n):
        t = _dot_nt(eye_bf16, p)
        out = t if out is None else out + t
    return out


def _softplus(x):
    return jnp.maximum(x, 0.0) + jnp.log(1.0 + jnp.exp(-jnp.abs(x)))


def _silu(x):
    h = 0.5 * x
    return h + h * jnp.tanh(h)


def _ln(x):
    mu = jnp.mean(x, axis=-1, keepdims=True)
    xc = x - mu
    var = jnp.mean(xc * xc, axis=-1, keepdims=True)
    return xc * lax.rsqrt(var + LN_EPS)


def _causal_conv_silu(xp, cw_ref, cb_ref):
    conv = cb_ref[...] + cw_ref[CONV_WIDTH - 1:CONV_WIDTH, :] * xp[CONV_PAD_ROWS:, :]
    for back in range(1, CONV_WIDTH):
        w = CONV_WIDTH - 1 - back
        conv = conv + cw_ref[w:w + 1, :] * pltpu.roll(xp, back, axis=0)[CONV_PAD_ROWS:, :]
    return _silu(conv)


def _eye(n, dtype):
    r = lax.broadcasted_iota(jnp.int32, (n, n), 0)
    c = lax.broadcasted_iota(jnp.int32, (n, n), 1)
    return jnp.where(r == c, 1.0, 0.0).astype(dtype)


MOD_PIECES = ("shift_a", "scale_a", "gate_a", "shift_f", "scale_f", "gate_f")


def _mod_spec(d, piece, batch, row0):
    col = MOD_PIECES.index(piece)
    if batch == 1:
        assert row0 % SUBLANES == 0
        return pl.BlockSpec((SUBLANES, d), lambda i: (row0 // SUBLANES, col))
    assert row0 == 0
    return pl.BlockSpec((batch, d), lambda i: (0, col))


def _mod_rows(ref, batch, m):
    if batch == 1:
        return ref[0:1, :]
    v = ref[...]
    return jnp.broadcast_to(v[:, None, :], (batch, m // batch, v.shape[1])).reshape(m, v.shape[1])


def _row_tile(m, batch, cap):
    tm = min(m, cap)
    assert batch == 1 or tm == m
    return tm


def _mod_kernel(c_ref, w_ref, b_ref, o_ref):
    s = _silu(c_ref[...])
    s_hi, s_lo = _split_bf16(s, 2)
    w_hi, w_lo = _split_bf16(w_ref[...], 2)
    o_ref[...] = _dot(s_hi, w_hi) + _dot(s_hi, w_lo) + _dot(s_lo, w_hi) + b_ref[...]


def _mod_call(c, w, b):
    r, d = c.shape
    n = w.shape[1]
    tn = 1024
    return pl.pallas_call(
        _mod_kernel,
        grid=(n // tn,),
        in_specs=[pl.BlockSpec((r, d), lambda j: (0, 0)),
                  pl.BlockSpec((d, tn), lambda j: (0, j)),
                  pl.BlockSpec((1, tn), lambda j: (0, j))],
        out_specs=pl.BlockSpec((r, tn), lambda j: (0, j)),
        out_shape=jax.ShapeDtypeStruct((r, n), F32),
        compiler_params=_cparams(("arbitrary",)),
        name="ada_mod",
    )(c, w, b)


def _ln_mod_kernel(x_ref, sc_ref, sh_ref, o_ref, *, batch):
    m = x_ref.shape[0]
    o_ref[...] = (_ln(x_ref[...]) * (1.0 + _mod_rows(sc_ref, batch, m))
                  + _mod_rows(sh_ref, batch, m)).astype(o_ref.dtype)


def _ln_mod_call(x, mod, *, batch, row0):
    m, d = x.shape
    tm = _row_tile(m, batch, 512)
    return pl.pallas_call(
        functools.partial(_ln_mod_kernel, batch=batch),
        grid=(m // tm,),
        in_specs=[pl.BlockSpec((tm, d), lambda i: (i, 0)),
                  _mod_spec(d, "scale_a", batch, row0),
                  _mod_spec(d, "shift_a", batch, row0)],
        out_specs=pl.BlockSpec((tm, d), lambda i: (i, 0)),
        out_shape=jax.ShapeDtypeStruct((m, d), BF16),
        compiler_params=_cparams(("arbitrary",)),
        name="ln_mod",
    )(x, mod, mod)


def _matmul_kernel(x_ref, w_ref, *refs, scales, feature_major):
    o_refs = refs[:len(scales)]
    if w_ref.dtype == BF16:
        w = w_ref[...]
    else:
        wb_ref = refs[len(scales)]

        @pl.when(pl.program_id(1) == 0)
        def _():
            wb_ref[...] = w_ref[...].astype(BF16)

        w = wb_ref[...]
    acc = _dot_nt(w, x_ref[...]) if feature_major else _dot_nt(x_ref[...], w)
    for o_ref, s in zip(o_refs, scales):
        o_ref[...] = (acc if s == 1.0 else acc * s).astype(o_ref.dtype)


def _matmul_call(x, wt, outs, name, *, row0=0, nrows=None, feature_major=False):
    m, k = x.shape
    n = wt.shape[0] - row0 if nrows is None else nrows
    tm = min(m, 1024)
    tn = min(n, 1024)
    assert row0 % tn == 0 and n % tn == 0
    rb0 = row0 // tn
    if feature_major:
        out_spec, out_dims = pl.BlockSpec((tn, tm), lambda j, i: (j, i)), (n, m)
    else:
        out_spec, out_dims = pl.BlockSpec((tm, tn), lambda j, i: (i, j)), (m, n)
    res = pl.pallas_call(
        functools.partial(_matmul_kernel, scales=tuple(s for _, s in outs), feature_major=feature_major),
        grid=(n // tn, m // tm),
        in_specs=[pl.BlockSpec((tm, k), lambda j, i: (i, 0)),
                  pl.BlockSpec((tn, k), lambda j, i: (rb0 + j, 0))],
        out_specs=[out_spec for _ in outs],
        out_shape=[jax.ShapeDtypeStruct(out_dims, dt) for dt, _ in outs],
        scratch_shapes=[] if wt.dtype == BF16 else [pltpu.VMEM((tn, k), BF16)],
        compiler_params=_cparams(("arbitrary", "arbitrary")),
        name=name,
    )(x, wt)
    return res


def _suffix_matrix(tk):
    r = lax.broadcasted_iota(jnp.int32, (tk, tk), 0)
    c = lax.broadcasted_iota(jnp.int32, (tk, tk), 1)
    return jnp.where(r >= c, 1.0, 0.0).astype(BF16)


def _stacked_row_ids(n, tq, tk):
    one = lax.broadcasted_iota(jnp.int32, (tq, tk), 0)
    return jnp.concatenate([one] * n, axis=0)


def _sb_weights(z, c_run, suffix, visible):
    neg_abs = lax.bitcast_convert_type(lax.bitcast_convert_type(z, jnp.int32) | jnp.int32(-2 ** 31), F32)
    sp = jnp.maximum(z, 0.0) + jnp.log(1.0 + jnp.exp2(neg_abs)) * LOG2E
    if visible is not None:
        sp = jnp.where(visible, sp, 0.0)
    s = _dot(sp.astype(BF16), suffix) + c_run
    w = jnp.exp2(z - s)
    if visible is not None:
        w = jnp.where(visible, w, 0.0)
    return w.astype(BF16), s[:, 0:1]


def _sb_live(c_run):
    return (jnp.min(c_run) <= SB_DEAD_LOG2).astype(jnp.int32)


def _sb_prompt_kernel(q_ref, k_ref, v_ref, o_ref, *, tq, npair):
    i = pl.program_id(1)
    low = lax.broadcasted_iota(jnp.int32, (tq, LANES), 1) < SB_HEAD_DIM
    suffix = _suffix_matrix(tq)
    rows = _stacked_row_ids(2, tq, tq)
    cols = lax.broadcasted_iota(jnp.int32, (2 * tq, tq), 1)
    lanes = [slice(p * LANES, (p + 1) * LANES) for p in range(npair)]
    q_ab = []
    for sl in lanes:
        qf = q_ref[:, sl].astype(F32)
        q_ab.append(jnp.concatenate([jnp.where(low, qf, 0.0), jnp.where(low, 0.0, qf)], axis=0).astype(BF16))

    def block(at, carry, visible):
        out = []
        for p, (c_run, acc) in enumerate(carry):
            w, c_run = _sb_weights(_dot(q_ab[p], k_ref[lanes[p], pl.ds(at, tq)]), c_run, suffix, visible)
            pv = _dot_nt(w, v_ref[lanes[p], pl.ds(at, tq)])
            out.append((c_run, acc + jnp.where(low, pv[:tq], pv[tq:])))
        return out

    def live(carry):
        return _sb_live(functools.reduce(jnp.minimum, [c for c, _ in carry]))

    carry = [(jnp.zeros((2 * tq, 1), F32), jnp.zeros((tq, LANES), F32)) for _ in range(npair)]
    carry = block(pl.multiple_of(i * tq, tq), carry, cols < rows)
    carry = block(pl.multiple_of(jnp.maximum(i - 1, 0) * tq, tq), carry, rows >= jnp.where(i > 0, 0, tq))

    def cond(st):
        return jnp.logical_and(st[0] >= 0, st[1] > 0)

    def body(st):
        j, _, carry = st
        carry = block(pl.multiple_of(j * tq, tq), carry, None)
        return j - 1, live(carry), carry

    _, _, carry = lax.while_loop(cond, body, (i - 2, live(carry), carry))
    for p, (_, acc) in enumerate(carry):
        o_ref[:, lanes[p]] = acc.astype(o_ref.dtype)


def _sb_prompt_call(q, kt, vt):
    t, w = q.shape
    tq = min(t, 256)
    npair = 2 if w % (2 * LANES) == 0 else 1
    wb = npair * LANES
    return pl.pallas_call(
        functools.partial(_sb_prompt_kernel, tq=tq, npair=npair),
        grid=(w // wb, t // tq),
        in_specs=[pl.BlockSpec((tq, wb), lambda p, i: (i, p)),
                  pl.BlockSpec((wb, t), lambda p, i: (p, 0)),
                  pl.BlockSpec((wb, t), lambda p, i: (p, 0))],
        out_specs=pl.BlockSpec((tq, wb), lambda p, i: (i, p)),
        out_shape=jax.ShapeDtypeStruct((t, w), BF16),
        compiler_params=_cparams(("arbitrary", "arbitrary")),
        name="sb_prompt",
    )(q, kt, vt)


def _sb_sample_kernel(q_ref, kn_ref, vn_ref, kc_ref, vc_ref, o_ref, c_ref, acc_ref, live_ref, *, heads, tq, tk):
    j = pl.program_id(1)
    hd = SB_HEAD_DIM

    def stacked_block(qk_of, pv_of, suffix, visible):
        q = q_ref[...]
        z = jnp.concatenate([qk_of(q[:, h * hd:(h + 1) * hd], h) for h in range(heads)], axis=0)
        w, c_run = _sb_weights(z, c_ref[...], suffix, visible)
        acc_ref[...] += jnp.concatenate([pv_of(w[h * tq:(h + 1) * tq], h) for h in range(heads)], axis=0)
        c_ref[...] = c_run
        live_ref[0] = _sb_live(c_run)

    @pl.when(j == 0)
    def _():
        c_ref[...] = jnp.zeros_like(c_ref)
        acc_ref[...] = jnp.zeros_like(acc_ref)
        rows = _stacked_row_ids(heads, tq, tq)
        cols = lax.broadcasted_iota(jnp.int32, (heads * tq, tq), 1)
        stacked_block(lambda qh, h: _dot_nt(qh, kn_ref[:, h * hd:(h + 1) * hd]),
                      lambda wh, h: _dot(wh, vn_ref[:, h * hd:(h + 1) * hd]),
                      _suffix_matrix(tq), cols < rows)

    @pl.when(jnp.logical_and(j > 0, live_ref[0] > 0))
    def _():
        stacked_block(lambda qh, h: _dot(qh, kc_ref[0, h].astype(BF16)),
                      lambda wh, h: _dot_nt(wh, vc_ref[0, h].astype(BF16)),
                      _suffix_matrix(tk), None)

    @pl.when(j == pl.num_programs(1) - 1)
    def _():
        acc = acc_ref[...]
        o_ref[...] = jnp.concatenate([acc[h * tq:(h + 1) * tq] for h in range(heads)], axis=1).astype(o_ref.dtype)


def _sb_sample_call(q, kn, vn, kc, vc, *, tq, heads):
    m, w = q.shape
    b, _, _, past = kc.shape
    tk = min(past, 256)
    nblk = past // tk
    new_spec = pl.BlockSpec((tq, w), lambda bi, j: (bi, 0))
    cache_spec = pl.BlockSpec((1, heads, SB_HEAD_DIM, tk),
                              lambda bi, j: (bi, 0, 0, jnp.minimum(nblk - j, nblk - 1)))
    return pl.pallas_call(
        functools.partial(_sb_sample_kernel, heads=heads, tq=tq, tk=tk),
        grid=(b, nblk + 1),
        in_specs=[new_spec, new_spec, new_spec, cache_spec, cache_spec],
        out_specs=new_spec,
        out_shape=jax.ShapeDtypeStruct((m, w), BF16),
        scratch_shapes=[pltpu.VMEM((heads * tq, 1), F32),
                        pltpu.VMEM((heads * tq, SB_HEAD_DIM), F32),
                        pltpu.SMEM((1,), jnp.int32)],
        compiler_params=_cparams(("arbitrary", "arbitrary")),
        name="sb_sample",
    )(q, kn, vn, kc, vc)


def _ssd_kernel(xbc_ref, z_ref, dt_ref, cprev_ref, h0_ref, cw_ref, cb_ref, dtb_ref, alog_ref, dsk_ref,
                nw_ref, o_ref, hout_ref, xpad_ref, st_ref, act_ref, y_ref, *, lc, inner, npairs):
    ci = pl.program_id(1)
    eye = _eye(LANES, BF16)

    @pl.when(ci == 0)
    def _():
        xpad_ref[0:CONV_PAD_ROWS, :] = cprev_ref[0]
        for p in range(npairs):
            st_ref[p] = h0_ref[0, p].T

    @pl.when(ci > 0)
    def _():
        xpad_ref[0:CONV_PAD_ROWS, :] = xpad_ref[lc:lc + CONV_PAD_ROWS, :]

    xpad_ref[CONV_PAD_ROWS:CONV_PAD_ROWS + lc, :] = xbc_ref[...]
    act_ref[...] = _causal_conv_silu(xpad_ref[...], cw_ref, cb_ref)

    dtv = _softplus(dt_ref[...] + dtb_ref[...])
    a_neg = -jnp.exp(alog_ref[...])
    d_a = dtv * a_neg
    r = lax.broadcasted_iota(jnp.int32, (lc, lc), 0)
    c = lax.broadcasted_iota(jnp.int32, (lc, lc), 1)
    causal = r >= c
    prefix = jnp.where(causal, 1.0, 0.0).astype(BF16)
    acausal_off = jnp.where(causal, 0.0, -1e30)
    a_cs = _dot_exact_rhs(prefix, d_a)
    a_cs_t = _transpose_exact(eye, a_cs)
    total = a_cs[lc - 1:lc, :]
    ea = jnp.exp(a_cs)
    to_end = jnp.exp(total - a_cs)
    chunk_decay = jnp.exp(total)
    low = lax.broadcasted_iota(jnp.int32, (lc, LANES), 1) < SSM_HEAD_DIM
    low1 = lax.broadcasted_iota(jnp.int32, (1, LANES), 1) < SSM_HEAD_DIM

    def pair_cols(arr, ha):
        return jnp.where(low, arr[:, ha:ha + 1], arr[:, ha + 1:ha + 2])

    pairs_per_group = npairs // SSM_GROUPS
    for g in range(SSM_GROUPS):
        b_bf = act_ref[:, inner + g * SSM_STATE:inner + (g + 1) * SSM_STATE].astype(BF16)
        c_off = inner + SSM_GROUPS * SSM_STATE
        c_bf = act_ref[:, c_off + g * SSM_STATE:c_off + (g + 1) * SSM_STATE].astype(BF16)
        cb = _dot_nt(c_bf, b_bf)
        b_t = _dot_nt(eye, b_bf).astype(BF16)
        for kk in range(pairs_per_group):
            p = g * pairs_per_group + kk
            ha = 2 * p
            sl = slice(p * LANES, (p + 1) * LANES)
            x = act_ref[:, sl]
            xdt = x * pair_cols(dtv, ha)
            xdt_bf = xdt.astype(BF16)
            ys = []
            for h in (ha, ha + 1):
                seg = a_cs[:, h:h + 1] - a_cs_t[h:h + 1, :]
                decay = jnp.exp(seg + acausal_off)
                ys.append(_dot((cb * decay).astype(BF16), xdt_bf))
            st = st_ref[p]
            y_off = _dot(c_bf, st.astype(BF16)) * pair_cols(ea, ha)
            y_ref[:, sl] = jnp.where(low, ys[0], ys[1]) + y_off + dsk_ref[:, sl] * x
            xw = (xdt * pair_cols(to_end, ha)).astype(BF16)
            dec = jnp.where(low1, chunk_decay[:, ha:ha + 1], chunk_decay[:, ha + 1:ha + 2])
            st_ref[p] = st * dec + _dot(b_t, xw)

    gy = y_ref[...] * _silu(z_ref[...].astype(F32))
    gw = inner // SSM_GROUPS
    for g in range(SSM_GROUPS):
        seg = gy[:, g * gw:(g + 1) * gw]
        ms = jnp.mean(seg * seg, axis=-1, keepdims=True)
        o_ref[:, g * gw:(g + 1) * gw] = (seg * lax.rsqrt(ms + LN_EPS) * nw_ref[:, g * gw:(g + 1) * gw]).astype(o_ref.dtype)

    @pl.when(ci == pl.num_programs(1) - 1)
    def _():
        for p in range(npairs):
            hout_ref[0, p] = st_ref[p].T


def _ssd_call(xbc, zg, dtr, conv_prev, h0, cw, cb, dtb, alog, dsk, nw, *, batch, lc):
    m = xbc.shape[0]
    conv_ch = cw.shape[1]
    inner = nw.shape[1]
    npairs = h0.shape[1]
    nc = m // batch // lc
    row = lambda b, c: b * nc + c
    const = lambda b, c: (0, 0)
    return pl.pallas_call(
        functools.partial(_ssd_kernel, lc=lc, inner=inner, npairs=npairs),
        grid=(batch, nc),
        in_specs=[pl.BlockSpec((lc, conv_ch), lambda b, c: (row(b, c), 0)),
                  pl.BlockSpec((lc, inner), lambda b, c: (row(b, c), 0)),
                  pl.BlockSpec((lc, LANES), lambda b, c: (row(b, c), 0)),
                  pl.BlockSpec((1, CONV_PAD_ROWS, conv_ch), lambda b, c: (b, 0, 0)),
                  pl.BlockSpec((1, npairs, LANES, LANES), lambda b, c: (b, 0, 0, 0)),
                  pl.BlockSpec((CONV_WIDTH, conv_ch), const),
                  pl.BlockSpec((1, conv_ch), const),
                  pl.BlockSpec((1, LANES), const),
                  pl.BlockSpec((1, LANES), const),
                  pl.BlockSpec((1, inner), const),
                  pl.BlockSpec((1, inner), const)],
        out_specs=[pl.BlockSpec((lc, inner), lambda b, c: (row(b, c), 0)),
                   pl.BlockSpec((1, npairs, LANES, LANES), lambda b, c: (b, 0, 0, 0))],
        out_shape=[jax.ShapeDtypeStruct((m, inner), BF16),
                   jax.ShapeDtypeStruct(h0.shape, F32)],
        scratch_shapes=[pltpu.VMEM((CONV_PAD_ROWS + lc, conv_ch), F32),
                        pltpu.VMEM((npairs, LANES, LANES), F32),
                        pltpu.VMEM((lc, conv_ch), F32),
                        pltpu.VMEM((lc, inner), F32)],
        compiler_params=_cparams(("arbitrary", "arbitrary")),
        name="ssd",
    )(xbc, zg, dtr, conv_prev, h0, cw, cb, dtb, alog, dsk, nw)


def _merge_kernel(osb_ref, ossm_ref, gsb_ref, gssm_ref, x_ref, ga_ref, lg_ref, lb_ref,
                  wsb_ref, wssm_ref, wout_ref, o_ref, *, alpha, batch):
    merged = (jax.nn.sigmoid(gsb_ref[...].astype(F32)) * _dot(osb_ref[...], wsb_ref[...])
              + jax.nn.sigmoid(gssm_ref[...].astype(F32)) * _dot(ossm_ref[...], wssm_ref[...]))
    gate = _mod_rows(ga_ref, batch, x_ref.shape[0])
    res = alpha * x_ref[...] + gate * _dot(merged.astype(BF16), wout_ref[...])
    o_ref[...] = _ln(res) * lg_ref[...] + lb_ref[...]


def _resident(shape):
    return pl.BlockSpec(shape, lambda i: (0, 0), pipeline_mode=pl.Buffered(1))


def _merge_call(osb, ossm, zg, x, mod, ln_g, ln_b, wsb, wssm, wout, *, alpha, batch, row0):
    m, d = x.shape
    sbw = osb.shape[1]
    inner = ossm.shape[1]
    tm = _row_tile(m, batch, 512)
    gcol = inner // d
    return pl.pallas_call(
        functools.partial(_merge_kernel, alpha=alpha, batch=batch),
        grid=(m // tm,),
        in_specs=[pl.BlockSpec((tm, sbw), lambda i: (i, 0)),
                  pl.BlockSpec((tm, inner), lambda i: (i, 0)),
                  pl.BlockSpec((tm, d), lambda i: (i, gcol)),
                  pl.BlockSpec((tm, d), lambda i: (i, gcol + 1)),
                  pl.BlockSpec((tm, d), lambda i: (i, 0)),
                  _mod_spec(d, "gate_a", batch, row0),
                  pl.BlockSpec((1, d), lambda i: (0, 0)),
                  pl.BlockSpec((1, d), lambda i: (0, 0)),
                  _resident(wsb.shape), _resident(wssm.shape), _resident(wout.shape)],
        out_specs=pl.BlockSpec((tm, d), lambda i: (i, 0)),
        out_shape=jax.ShapeDtypeStruct((m, d), F32),
        compiler_params=_cparams(("arbitrary",)),
        name="merge",
    )(osb, ossm, zg, zg, x, mod, ln_g, ln_b, wsb, wssm, wout)


def _ffn_kernel(x1_ref, sc_ref, sh_ref, gf_ref, lg_ref, lb_ref, wup_ref, wdn_ref, o_ref, *, alpha, nchunk, batch):
    x1 = x1_ref[...]
    m = x1.shape[0]
    h2 = (_ln(x1) * (1.0 + _mod_rows(sc_ref, batch, m)) + _mod_rows(sh_ref, batch, m)).astype(BF16)
    dff = wup_ref.shape[1]
    cw = dff // nchunk
    f = None
    for ck in range(nchunk):
        a = jnp.maximum(_dot(h2, wup_ref[:, ck * cw:(ck + 1) * cw]), 0.0)
        t = _dot((a * a).astype(BF16), wdn_ref[ck * cw:(ck + 1) * cw, :])
        f = t if f is None else f + t
    o_ref[...] = _ln(alpha * x1 + _mod_rows(gf_ref, batch, m) * f) * lg_ref[...] + lb_ref[...]


def _ffn_call(x1, mod, ln_g, ln_b, wup, wdn, *, alpha, batch, row0):
    m, d = x1.shape
    tm = _row_tile(m, batch, 1024)
    return pl.pallas_call(
        functools.partial(_ffn_kernel, alpha=alpha, nchunk=4, batch=batch),
        grid=(m // tm,),
        in_specs=[pl.BlockSpec((tm, d), lambda i: (i, 0)),
                  _mod_spec(d, "scale_f", batch, row0),
                  _mod_spec(d, "shift_f", batch, row0),
                  _mod_spec(d, "gate_f", batch, row0),
                  pl.BlockSpec((1, d), lambda i: (0, 0)),
                  pl.BlockSpec((1, d), lambda i: (0, 0)),
                  _resident(wup.shape), _resident(wdn.shape)],
        out_specs=pl.BlockSpec((tm, d), lambda i: (i, 0)),
        out_shape=jax.ShapeDtypeStruct((m, d), F32),
        compiler_params=_cparams(("arbitrary",)),
        name="ffn",
    )(x1, mod, mod, mod, ln_g, ln_b, wup, wdn)


def _layer(x, mod, cache_k, cache_v, conv_prev, ssm_prev, wts, *, batch, row0, alpha):
    m, d = x.shape
    t = m // batch
    h = _ln_mod_call(x, mod, batch=batch, row0=row0)
    def proj(cols, outs, name, **kw):
        return _matmul_call(h, wts["w_in_t"], outs, name, row0=wts[cols][0], nrows=wts[cols][1], **kw)

    (q,) = proj("q_cols", [(BF16, SB_HEAD_DIM ** -0.5 * LOG2E)], "proj_q")
    kv_t = cache_k is None
    k32, kb = proj("k_cols", [(F32, 1.0), (BF16, 1.0)], "proj_k", feature_major=kv_t)
    v32, vb = proj("v_cols", [(F32, 1.0), (BF16, 1.0)], "proj_v", feature_major=kv_t)
    if kv_t:
        k32, v32 = k32.T, v32.T
    conv_ch = wts["conv_w"].shape[1]
    cprev = jnp.pad(conv_prev, ((0, 0), (CONV_PAD_ROWS - (CONV_WIDTH - 1), 0), (0, 0)))
    (xbc,) = proj("xbc_cols", [(F32, 1.0)], "proj_xbc")
    conv_new = xbc.reshape(batch, t, conv_ch)[:, t - (CONV_WIDTH - 1):]
    (zg,) = _matmul_call(h, wts["w_zg"], [(BF16, 1.0)], "proj_zg")
    (dtr,) = _matmul_call(h, wts["w_dt"], [(F32, 1.0)], "proj_dt")
    if cache_k is None:
        o_sb = _sb_prompt_call(q, kb, vb)
    else:
        o_sb = _sb_sample_call(q, kb, vb, cache_k, cache_v, tq=t, heads=q.shape[1] // SB_HEAD_DIM)
    npairs = ssm_prev.shape[1] // 2
    h0 = ssm_prev.reshape(batch, npairs, LANES, SSM_STATE)
    o_ssm, h_fin = _ssd_call(xbc, zg, dtr, cprev, h0, wts["conv_w"], wts["conv_b"], wts["dt_bias"], wts["a_log"],
                             wts["d_skip"], wts["ssm_norm_w"], batch=batch, lc=min(t, 128))
    x1 = _merge_call(o_sb, o_ssm, zg, x, mod, wts["ln_attn_g"], wts["ln_attn_b"],
                     wts["w_branch_sb"], wts["w_branch_ssm"], wts["w_out"], alpha=alpha, batch=batch, row0=row0)
    y = _ffn_call(x1, mod, wts["ln_ffn_g"], wts["ln_ffn_b"], wts["w_up"], wts["w_down"],
                  alpha=alpha, batch=batch, row0=row0)
    return y, k32, v32, conv_new, h_fin.reshape(ssm_prev.shape)


def _prep_weights(l, w_in, conv_w, conv_b, dt_bias, a_log, d_skip, ssm_norm_w, w_branch_sb, w_branch_ssm,
                  w_out, ln_attn_g, ln_attn_b, w_up, w_down, ln_ffn_g, ln_ffn_b, sbw):
    d = w_in.shape[1]
    inner = ssm_norm_w.shape[1]
    conv_ch = conv_w.shape[2]
    nh = a_log.shape[1]
    o_z = 3 * sbw
    o_xbc = o_z + inner
    o_dt = o_xbc + conv_ch
    o_g = o_dt + nh
    wit = jnp.transpose(w_in[l])
    pad_h = lambda v: jnp.pad(v, (0, LANES - nh))[None, :]
    return {
        "w_in_t": wit, "q_cols": (0, sbw), "k_cols": (sbw, sbw), "v_cols": (2 * sbw, sbw),
        "xbc_cols": (o_xbc, conv_ch),
        "w_zg": jnp.concatenate([wit[o_z:o_xbc], wit[o_g:]], axis=0),
        "w_dt": jnp.pad(wit[o_dt:o_g], ((0, LANES - nh), (0, 0))),
        "conv_w": conv_w[l], "conv_b": conv_b[l][None, :],
        "dt_bias": pad_h(dt_bias[l]), "a_log": pad_h(a_log[l]),
        "d_skip": jnp.repeat(d_skip[l], SSM_HEAD_DIM)[None, :],
        "ssm_norm_w": ssm_norm_w[l][None, :],
        "w_branch_sb": w_branch_sb[l].astype(BF16), "w_branch_ssm": w_branch_ssm[l].astype(BF16),
        "w_out": w_out[l].astype(BF16),
        "ln_attn_g": ln_attn_g[l][None, :], "ln_attn_b": ln_attn_b[l][None, :],
        "w_up": w_up[l].astype(BF16), "w_down": w_down[l].astype(BF16),
        "ln_ffn_g": ln_ffn_g[l][None, :], "ln_ffn_b": ln_ffn_b[l][None, :],
    }


def kernel(x_prompt, x_sample, cache_sb_k, cache_sb_v, state_conv, state_ssm, c_prompt, c_sample, w_ada, b_ada, w_in, conv_w, conv_b, dt_bias, a_log, d_skip, ssm_norm_w, w_branch_sb, w_branch_ssm, w_out, ln_attn_g, ln_attn_b, w_up, w_down, ln_ffn_g, ln_ffn_b):
    depth = w_ada.shape[0]
    alpha = (2 * depth) ** 0.25
    n_p, t_p, d = x_prompt.shape
    n_s, t_s, _ = x_sample.shape
    past = cache_sb_k.shape[2]
    sbw = cache_sb_k.shape[3] * cache_sb_k.shape[4]
    conv_ch = conv_w.shape[2]
    y_p = x_prompt.reshape(n_p * t_p, d)
    y_s = x_sample.reshape(n_s * t_s, d)
    assert n_p == 1
    row0_p = -(-n_s // SUBLANES) * SUBLANES
    c_all = jnp.concatenate([jnp.pad(c_sample, ((0, row0_p - n_s), (0, 0))),
                             jnp.pad(c_prompt, ((0, SUBLANES - n_p), (0, 0)))], axis=0)
    new_p, new_s = [], []
    for l in range(depth):
        wts = _prep_weights(l, w_in, conv_w, conv_b, dt_bias, a_log, d_skip, ssm_norm_w, w_branch_sb,
                            w_branch_ssm, w_out, ln_attn_g, ln_attn_b, w_up, w_down, ln_ffn_g, ln_ffn_b, sbw)
        mod = _mod_call(c_all, w_ada[l], b_ada[l][None, :])
        conv0 = jnp.zeros((n_p, CONV_WIDTH - 1, conv_ch), F32)
        ssm0 = jnp.zeros((n_p,) + state_ssm.shape[2:], F32)
        y_p, k_p, v_p, cv_p, s_p = _layer(y_p, mod, None, None, conv0, ssm0, wts,
                                          batch=n_p, row0=row0_p, alpha=alpha)
        y_s, k_s, v_s, cv_s, s_s = _layer(y_s, mod, jnp.transpose(cache_sb_k[l], (0, 2, 3, 1)),
                                          jnp.transpose(cache_sb_v[l], (0, 2, 3, 1)), state_conv[l], state_ssm[l],
                                          wts, batch=n_s, row0=0, alpha=alpha)
        hshape = cache_sb_k.shape[3:]
        new_p.append((k_p.reshape((n_p, t_p) + hshape), v_p.reshape((n_p, t_p) + hshape), cv_p, s_p))
        new_s.append((k_s.reshape((n_s, t_s) + hshape), v_s.reshape((n_s, t_s) + hshape), cv_s, s_s))
    stack = lambda lst, i: jnp.stack([e[i] for e in lst])
    return (y_p.reshape(n_p, t_p, d), y_s.reshape(n_s, t_s, d),
            stack(new_p, 0), stack(new_p, 1), stack(new_p, 2), stack(new_p, 3),
            stack(new_s, 0), stack(new_s, 1), stack(new_s, 2), stack(new_s, 3))
```

```python
import functools

import jax
import jax.numpy as jnp
from jax import lax
from jax.experimental import pallas as pl
from jax.experimental.pallas import tpu as pltpu

F32 = jnp.float32
BF16 = jnp.bfloat16

SB_HEAD_DIM = 64
SSM_HEAD_DIM = 64
SSM_STATE = 128
SSM_GROUPS = 8
CONV_WIDTH = 4
LN_EPS = 1e-5
LANES = 128
SUBLANES = 8
CONV_PAD_ROWS = 8
VMEM_LIMIT = 56 * 1024 * 1024
LOG2E = 1.4426950408889634
SB_DEAD_LOG2 = 150.0


def _cparams(sem):
    return pltpu.CompilerParams(dimension_semantics=sem, vmem_limit_bytes=VMEM_LIMIT)


def _dot(a, b):
    return jnp.dot(a, b, preferred_element_type=F32)


def _dot_nt(a, b):
    return lax.dot_general(a, b, (((1,), (1,)), ((), ())), preferred_element_type=F32)


def _split_bf16(x, n):
    parts, r = [], x
    for _ in range(n):
        p = r.astype(BF16)
        parts.append(p)
        r = r - p.astype(F32)
    return parts


def _dot_exact_rhs(a_bf16, b_f32, n=3):
    out = None
    for p in _split_bf16(b_f32, n):
        t = _dot(a_bf16, p)
        out = t if out is None else out + t
    return out


def _dot_exact_lhs(a_f32, b_bf16, n):
    out = None
    for p in _split_bf16(a_f32, n):
        t = _dot(p, b_bf16)
        out = t if out is None else out + t
    return out


def _transpose_exact(eye_bf16, b_f32, n=3):
    out = None
    for p in _split_bf16(b_f32, n):
        t = _dot_nt(eye_bf16, p)
        out = t if out is None else out + t
    return out


def _softplus(x):
    return jnp.maximum(x, 0.0) + jnp.log(1.0 + jnp.exp(-jnp.abs(x)))


def _silu(x):
    h = 0.5 * x
    return h + h * jnp.tanh(h)


def _ln(x):
    mu = jnp.mean(x, axis=-1, keepdims=True)
    xc = x - mu
    var = jnp.mean(xc * xc, axis=-1, keepdims=True)
    return xc * lax.rsqrt(var + LN_EPS)


def _causal_conv_silu(xp, cw_ref, cb_ref):
    conv = cb_ref[...] + cw_ref[CONV_WIDTH - 1:CONV_WIDTH, :] * xp[CONV_PAD_ROWS:, :]
    for back in range(1, CONV_WIDTH):
        w = CONV_WIDTH - 1 - back
        conv = conv + cw_ref[w:w + 1, :] * pltpu.roll(xp, back, axis=0)[CONV_PAD_ROWS:, :]
    return _silu(conv)


def _eye(n, dtype):
    r = lax.broadcasted_iota(jnp.int32, (n, n), 0)
    c = lax.broadcasted_iota(jnp.int32, (n, n), 1)
    return jnp.where(r == c, 1.0, 0.0).astype(dtype)


MOD_PIECES = ("shift_a", "scale_a", "gate_a", "shift_f", "scale_f", "gate_f")


def _mod_spec(d, piece, batch, row0):
    col = MOD_PIECES.index(piece)
    if batch == 1:
        assert row0 % SUBLANES == 0
        return pl.BlockSpec((SUBLANES, d), lambda i: (row0 // SUBLANES, col))
    assert row0 == 0
    return pl.BlockSpec((batch, d), lambda i: (0, col))


def _mod_rows(ref, batch, m):
    if batch == 1:
        return ref[0:1, :]
    v = ref[...]
    return jnp.broadcast_to(v[:, None, :], (batch, m // batch, v.shape[1])).reshape(m, v.shape[1])


def _row_tile(m, batch, cap):
    tm = min(m, cap)
    assert batch == 1 or tm == m
    return tm


def _mod_kernel(c_ref, w_ref, b_ref, o_ref):
    s = _silu(c_ref[...])
    s_hi, s_lo = _split_bf16(s, 2)
    w_hi, w_lo = _split_bf16(w_ref[...], 2)
    o_ref[...] = _dot(s_hi, w_hi) + _dot(s_hi, w_lo) + _dot(s_lo, w_hi) + b_ref[...]


def _mod_call(c, w, b):
    r, d = c.shape
    n = w.shape[1]
    tn = 1024
    return pl.pallas_call(
        _mod_kernel,
        grid=(n // tn,),
        in_specs=[pl.BlockSpec((r, d), lambda j: (0, 0)),
                  pl.BlockSpec((d, tn), lambda j: (0, j)),
                  pl.BlockSpec((1, tn), lambda j: (0, j))],
        out_specs=pl.BlockSpec((r, tn), lambda j: (0, j)),
        out_shape=jax.ShapeDtypeStruct((r, n), F32),
        compiler_params=_cparams(("arbitrary",)),
        name="ada_mod",
    )(c, w, b)


def _ln_mod_kernel(x_ref, sc_ref, sh_ref, o_ref, *, batch):
    m = x_ref.shape[0]
    o_ref[...] = (_ln(x_ref[...]) * (1.0 + _mod_rows(sc_ref, batch, m))
                  + _mod_rows(sh_ref, batch, m)).astype(o_ref.dtype)


def _ln_mod_call(x, mod, *, batch, row0):
    m, d = x.shape
    tm = _row_tile(m, batch, 512)
    return pl.pallas_call(
        functools.partial(_ln_mod_kernel, batch=batch),
        grid=(m // tm,),
        in_specs=[pl.BlockSpec((tm, d), lambda i: (i, 0)),
                  _mod_spec(d, "scale_a", batch, row0),
                  _mod_spec(d, "shift_a", batch, row0)],
        out_specs=pl.BlockSpec((tm, d), lambda i: (i, 0)),
        out_shape=jax.ShapeDtypeStruct((m, d), BF16),
        compiler_params=_cparams(("arbitrary",)),
        name="ln_mod",
    )(x, mod, mod)


def _matmul_kernel(x_ref, w_ref, *refs, scales, feature_major):
    o_refs = refs[:len(scales)]
    if w_ref.dtype == BF16:
        w = w_ref[...]
    else:
        wb_ref = refs[len(scales)]

        @pl.when(pl.program_id(1) == 0)
        def _():
            wb_ref[...] = w_ref[...].astype(BF16)

        w = wb_ref[...]
    acc = _dot_nt(w, x_ref[...]) if feature_major else _dot_nt(x_ref[...], w)
    for o_ref, s in zip(o_refs, scales):
        o_ref[...] = (acc if s == 1.0 else acc * s).astype(o_ref.dtype)


def _matmul_call(x, wt, outs, name, *, row0=0, nrows=None, feature_major=False):
    m, k = x.shape
    n = wt.shape[0] - row0 if nrows is None else nrows
    tm = min(m, 1024)
    tn = min(n, 1024)
    assert row0 % tn == 0 and n % tn == 0
    rb0 = row0 // tn
    if feature_major:
        out_spec, out_dims = pl.BlockSpec((tn, tm), lambda j, i: (j, i)), (n, m)
    else:
        out_spec, out_dims = pl.BlockSpec((tm, tn), lambda j, i: (i, j)), (m, n)
    res = pl.pallas_call(
        functools.partial(_matmul_kernel, scales=tuple(s for _, s in outs), feature_major=feature_major),
        grid=(n // tn, m // tm),
        in_specs=[pl.BlockSpec((tm, k), lambda j, i: (i, 0)),
                  pl.BlockSpec((tn, k), lambda j, i: (rb0 + j, 0))],
        out_specs=[out_spec for _ in outs],
        out_shape=[jax.ShapeDtypeStruct(out_dims, dt) for dt, _ in outs],
        scratch_shapes=[] if wt.dtype == BF16 else [pltpu.VMEM((tn, k), BF16)],
        compiler_params=_cparams(("arbitrary", "arbitrary")),
        name=name,
    )(x, wt)
    return res


def _suffix_matrix(tk):
    r = lax.broadcasted_iota(jnp.int32, (tk, tk), 0)
    c = lax.broadcasted_iota(jnp.int32, (tk, tk), 1)
    return jnp.where(r >= c, 1.0, 0.0).astype(BF16)


def _stacked_row_ids(n, tq, tk):
    one = lax.broadcasted_iota(jnp.int32, (tq, tk), 0)
    return jnp.concatenate([one] * n, axis=0)


def _sb_weights(z, c_run, suffix, visible):
    neg_abs = lax.bitcast_convert_type(lax.bitcast_convert_type(z, jnp.int32) | jnp.int32(-2 ** 31), F32)
    sp = jnp.maximum(z, 0.0) + jnp.log(1.0 + jnp.exp2(neg_abs)) * LOG2E
    if visible is not None:
        sp = jnp.where(visible, sp, 0.0)
    s = _dot(sp.astype(BF16), suffix) + c_run
    w = jnp.exp2(z - s)
    if visible is not None:
        w = jnp.where(visible, w, 0.0)
    return w.astype(BF16), s[:, 0:1]


def _sb_live(c_run):
    return (jnp.min(c_run) <= SB_DEAD_LOG2).astype(jnp.int32)


def _sb_prompt_kernel(q_ref, k_ref, v_ref, o_ref, *, tq, npair, nq):
    i = pl.program_id(1)
    low = lax.broadcasted_iota(jnp.int32, (tq, LANES), 1) < SB_HEAD_DIM
    suffix = _suffix_matrix(tq)
    rows = _stacked_row_ids(2, tq, tq)
    cols = lax.broadcasted_iota(jnp.int32, (2 * tq, tq), 1)
    units = [(s, slice(s * tq, (s + 1) * tq), slice(p * LANES, (p + 1) * LANES))
             for s in range(nq) for p in range(npair)]
    q_ab = []
    for _, rsl, lsl in units:
        qf = q_ref[rsl, lsl].astype(F32)
        q_ab.append(jnp.concatenate([jnp.where(low, qf, 0.0), jnp.where(low, 0.0, qf)], axis=0).astype(BF16))

    def block(key_block, carry, visible):
        out = []
        for u, ((s, _, lsl), (c_run, acc)) in enumerate(zip(units, carry)):
            at = pl.multiple_of(key_block(s) * tq, tq)
            w, c_run = _sb_weights(_dot(q_ab[u], k_ref[lsl, pl.ds(at, tq)]), c_run, suffix, visible(s))
            pv = _dot_nt(w, v_ref[lsl, pl.ds(at, tq)])
            out.append((c_run, acc + jnp.where(low, pv[:tq], pv[tq:])))
        return out

    def live(carry):
        return _sb_live(functools.reduce(jnp.minimum, [c for c, _ in carry]))

    def all_or_none(flag):
        return rows >= jnp.where(flag, 0, tq)

    qb = lambda s: i * nq + s
    carry = [(jnp.zeros((2 * tq, 1), F32), jnp.zeros((tq, LANES), F32)) for _ in units]
    carry = block(qb, carry, lambda s: cols < rows)
    carry = block(lambda s: jnp.maximum(qb(s) - 1, 0), carry, lambda s: all_or_none(qb(s) > 0))

    def cond(st):
        return jnp.logical_and(qb(nq - 1) - 2 - st[0] >= 0, st[1] > 0)

    def body(st):
        n, _, carry = st
        kb = lambda s: qb(s) - 2 - n
        carry = block(lambda s: jnp.maximum(kb(s), 0), carry,
                      (lambda s: None) if nq == 1 else (lambda s: all_or_none(kb(s) >= 0)))
        return n + 1, live(carry), carry

    _, _, carry = lax.while_loop(cond, body, (0, live(carry), carry))
    for (_, rsl, lsl), (_, acc) in zip(units, carry):
        o_ref[rsl, lsl] = acc.astype(o_ref.dtype)


def _sb_prompt_call(q, kt, vt):
    t, w = q.shape
    tq = min(t, 256)
    npair = 2 if w % (2 * LANES) == 0 else 1
    nq = 2 if t % (2 * tq) == 0 else 1
    wb = npair * LANES
    return pl.pallas_call(
        functools.partial(_sb_prompt_kernel, tq=tq, npair=npair, nq=nq),
        grid=(w // wb, t // (nq * tq)),
        in_specs=[pl.BlockSpec((nq * tq, wb), lambda p, i: (i, p)),
                  pl.BlockSpec((wb, t), lambda p, i: (p, 0)),
                  pl.BlockSpec((wb, t), lambda p, i: (p, 0))],
        out_specs=pl.BlockSpec((nq * tq, wb), lambda p, i: (i, p)),
        out_shape=jax.ShapeDtypeStruct((t, w), BF16),
        compiler_params=_cparams(("arbitrary", "arbitrary")),
        name="sb_prompt",
    )(q, kt, vt)


def _sb_sample_kernel(q_ref, kn_ref, vn_ref, kc_ref, vc_ref, o_ref, c_ref, acc_ref, live_ref, *, heads, tq, tk):
    j = pl.program_id(1)
    hd = SB_HEAD_DIM

    def stacked_block(qk_of, pv_of, suffix, visible):
        q = q_ref[...]
        z = jnp.concatenate([qk_of(q[:, h * hd:(h + 1) * hd], h) for h in range(heads)], axis=0)
        w, c_run = _sb_weights(z, c_ref[...], suffix, visible)
        acc_ref[...] += jnp.concatenate([pv_of(w[h * tq:(h + 1) * tq], h) for h in range(heads)], axis=0)
        c_ref[...] = c_run
        live_ref[0] = _sb_live(c_run)

    @pl.when(j == 0)
    def _():
        c_ref[...] = jnp.zeros_like(c_ref)
        acc_ref[...] = jnp.zeros_like(acc_ref)
        rows = _stacked_row_ids(heads, tq, tq)
        cols = lax.broadcasted_iota(jnp.int32, (heads * tq, tq), 1)
        stacked_block(lambda qh, h: _dot_nt(qh, kn_ref[:, h * hd:(h + 1) * hd]),
                      lambda wh, h: _dot(wh, vn_ref[:, h * hd:(h + 1) * hd]),
                      _suffix_matrix(tq), cols < rows)

    @pl.when(jnp.logical_and(j > 0, live_ref[0] > 0))
    def _():
        stacked_block(lambda qh, h: _dot(qh, kc_ref[0, h].astype(BF16)),
                      lambda wh, h: _dot_nt(wh, vc_ref[0, h].astype(BF16)),
                      _suffix_matrix(tk), None)

    @pl.when(j == pl.num_programs(1) - 1)
    def _():
        acc = acc_ref[...]
        o_ref[...] = jnp.concatenate([acc[h * tq:(h + 1) * tq] for h in range(heads)], axis=1).astype(o_ref.dtype)


def _sb_sample_call(q, kn, vn, kc, vc, *, tq, heads):
    m, w = q.shape
    b, _, _, past = kc.shape
    tk = min(past, 256)
    nblk = past // tk
    new_spec = pl.BlockSpec((tq, w), lambda bi, j: (bi, 0))
    cache_spec = pl.BlockSpec((1, heads, SB_HEAD_DIM, tk),
                              lambda bi, j: (bi, 0, 0, jnp.minimum(nblk - j, nblk - 1)))
    return pl.pallas_call(
        functools.partial(_sb_sample_kernel, heads=heads, tq=tq, tk=tk),
        grid=(b, nblk + 1),
        in_specs=[new_spec, new_spec, new_spec, cache_spec, cache_spec],
        out_specs=new_spec,
        out_shape=jax.ShapeDtypeStruct((m, w), BF16),
        scratch_shapes=[pltpu.VMEM((heads * tq, 1), F32),
                        pltpu.VMEM((heads * tq, SB_HEAD_DIM), F32),
                        pltpu.SMEM((1,), jnp.int32)],
        compiler_params=_cparams(("arbitrary", "arbitrary")),
        name="sb_sample",
    )(q, kn, vn, kc, vc)


def _ssd_kernel(xbc_ref, z_ref, dt_ref, cprev_ref, h0_ref, cw_ref, cb_ref, dtb_ref, alog_ref, dsk_ref,
                nw_ref, o_ref, hout_ref, xpad_ref, st_ref, act_ref, y_ref, *, lc, inner, npairs):
    ci = pl.program_id(1)
    eye = _eye(LANES, BF16)

    @pl.when(ci == 0)
    def _():
        xpad_ref[0:CONV_PAD_ROWS, :] = cprev_ref[0]
        for p in range(npairs):
            st_ref[p] = h0_ref[0, p].T

    @pl.when(ci > 0)
    def _():
        xpad_ref[0:CONV_PAD_ROWS, :] = xpad_ref[lc:lc + CONV_PAD_ROWS, :]

    xpad_ref[CONV_PAD_ROWS:CONV_PAD_ROWS + lc, :] = xbc_ref[...]
    act_ref[...] = _causal_conv_silu(xpad_ref[...], cw_ref, cb_ref)

    dtv = _softplus(dt_ref[...] + dtb_ref[...])
    a_neg = -jnp.exp(alog_ref[...])
    d_a = dtv * a_neg
    r = lax.broadcasted_iota(jnp.int32, (lc, lc), 0)
    c = lax.broadcasted_iota(jnp.int32, (lc, lc), 1)
    causal = r >= c
    prefix = jnp.where(causal, 1.0, 0.0).astype(BF16)
    acausal_off = jnp.where(causal, 0.0, -1e30)
    a_cs = _dot_exact_rhs(prefix, d_a)
    a_cs_t = _transpose_exact(eye, a_cs)
    total = a_cs[lc - 1:lc, :]
    ea = jnp.exp(a_cs)
    to_end = jnp.exp(total - a_cs)
    chunk_decay = jnp.exp(total)
    low = lax.broadcasted_iota(jnp.int32, (lc, LANES), 1) < SSM_HEAD_DIM
    low1 = lax.broadcasted_iota(jnp.int32, (1, LANES), 1) < SSM_HEAD_DIM

    def pair_cols(arr, ha):
        return jnp.where(low, arr[:, ha:ha + 1], arr[:, ha + 1:ha + 2])

    pairs_per_group = npairs // SSM_GROUPS
    for g in range(SSM_GROUPS):
        b_bf = act_ref[:, inner + g * SSM_STATE:inner + (g + 1) * SSM_STATE].astype(BF16)
        c_off = inner + SSM_GROUPS * SSM_STATE
        c_bf = act_ref[:, c_off + g * SSM_STATE:c_off + (g + 1) * SSM_STATE].astype(BF16)
        cb = _dot_nt(c_bf, b_bf)
        b_t = _dot_nt(eye, b_bf).astype(BF16)
        for kk in range(pairs_per_group):
            p = g * pairs_per_group + kk
            ha = 2 * p
            sl = slice(p * LANES, (p + 1) * LANES)
            x = act_ref[:, sl]
            xdt = x * pair_cols(dtv, ha)
            xdt_bf = xdt.astype(BF16)
            ys = []
            for h in (ha, ha + 1):
                seg = a_cs[:, h:h + 1] - a_cs_t[h:h + 1, :]
                decay = jnp.exp(seg + acausal_off)
                ys.append(_dot((cb * decay).astype(BF16), xdt_bf))
            st = st_ref[p]
            y_off = _dot(c_bf, st.astype(BF16)) * pair_cols(ea, ha)
            y_ref[:, sl] = jnp.where(low, ys[0], ys[1]) + y_off + dsk_ref[:, sl] * x
            xw = (xdt * pair_cols(to_end, ha)).astype(BF16)
            dec = jnp.where(low1, chunk_decay[:, ha:ha + 1], chunk_decay[:, ha + 1:ha + 2])
            st_ref[p] = st * dec + _dot(b_t, xw)

    gy = y_ref[...] * _silu(z_ref[...].astype(F32))
    gw = inner // SSM_GROUPS
    for g in range(SSM_GROUPS):
        seg = gy[:, g * gw:(g + 1) * gw]
        ms = jnp.mean(seg * seg, axis=-1, keepdims=True)
        o_ref[:, g * gw:(g + 1) * gw] = (seg * lax.rsqrt(ms + LN_EPS) * nw_ref[:, g * gw:(g + 1) * gw]).astype(o_ref.dtype)

    @pl.when(ci == pl.num_programs(1) - 1)
    def _():
        for p in range(npairs):
            hout_ref[0, p] = st_ref[p].T


def _ssd_call(xbc, zg, dtr, conv_prev, h0, cw, cb, dtb, alog, dsk, nw, *, batch, lc):
    m = xbc.shape[0]
    conv_ch = cw.shape[1]
    inner = nw.shape[1]
    npairs = h0.shape[1]
    nc = m // batch // lc
    row = lambda b, c: b * nc + c
    const = lambda b, c: (0, 0)
    return pl.pallas_call(
        functools.partial(_ssd_kernel, lc=lc, inner=inner, npairs=npairs),
        grid=(batch, nc),
        in_specs=[pl.BlockSpec((lc, conv_ch), lambda b, c: (row(b, c), 0)),
                  pl.BlockSpec((lc, inner), lambda b, c: (row(b, c), 0)),
                  pl.BlockSpec((lc, LANES), lambda b, c: (row(b, c), 0)),
                  pl.BlockSpec((1, CONV_PAD_ROWS, conv_ch), lambda b, c: (b, 0, 0)),
                  pl.BlockSpec((1, npairs, LANES, LANES), lambda b, c: (b, 0, 0, 0)),
                  pl.BlockSpec((CONV_WIDTH, conv_ch), const),
                  pl.BlockSpec((1, conv_ch), const),
                  pl.BlockSpec((1, LANES), const),
                  pl.BlockSpec((1, LANES), const),
                  pl.BlockSpec((1, inner), const),
                  pl.BlockSpec((1, inner), const)],
        out_specs=[pl.BlockSpec((lc, inner), lambda b, c: (row(b, c), 0)),
                   pl.BlockSpec((1, npairs, LANES, LANES), lambda b, c: (b, 0, 0, 0))],
        out_shape=[jax.ShapeDtypeStruct((m, inner), BF16),
                   jax.ShapeDtypeStruct(h0.shape, F32)],
        scratch_shapes=[pltpu.VMEM((CONV_PAD_ROWS + lc, conv_ch), F32),
                        pltpu.VMEM((npairs, LANES, LANES), F32),
                        pltpu.VMEM((lc, conv_ch), F32),
                        pltpu.VMEM((lc, inner), F32)],
        compiler_params=_cparams(("arbitrary", "arbitrary")),
        name="ssd",
    )(xbc, zg, dtr, conv_prev, h0, cw, cb, dtb, alog, dsk, nw)


def _merge_kernel(osb_ref, ossm_ref, gsb_ref, gssm_ref, x_ref, ga_ref, lg_ref, lb_ref,
                  wsb_ref, wssm_ref, wout_ref, o_ref, *, alpha, batch):
    merged = (jax.nn.sigmoid(gsb_ref[...].astype(F32)) * _dot(osb_ref[...], wsb_ref[...])
              + jax.nn.sigmoid(gssm_ref[...].astype(F32)) * _dot(ossm_ref[...], wssm_ref[...]))
    gate = _mod_rows(ga_ref, batch, x_ref.shape[0])
    res = alpha * x_ref[...] + gate * _dot(merged.astype(BF16), wout_ref[...])
    o_ref[...] = _ln(res) * lg_ref[...] + lb_ref[...]


def _resident(shape):
    return pl.BlockSpec(shape, lambda i: (0, 0), pipeline_mode=pl.Buffered(1))


def _merge_call(osb, ossm, zg, x, mod, ln_g, ln_b, wsb, wssm, wout, *, alpha, batch, row0):
    m, d = x.shape
    sbw = osb.shape[1]
    inner = ossm.shape[1]
    tm = _row_tile(m, batch, 512)
    gcol = inner // d
    return pl.pallas_call(
        functools.partial(_merge_kernel, alpha=alpha, batch=batch),
        grid=(m // tm,),
        in_specs=[pl.BlockSpec((tm, sbw), lambda i: (i, 0)),
                  pl.BlockSpec((tm, inner), lambda i: (i, 0)),
                  pl.BlockSpec((tm, d), lambda i: (i, gcol)),
                  pl.BlockSpec((tm, d), lambda i: (i, gcol + 1)),
                  pl.BlockSpec((tm, d), lambda i: (i, 0)),
                  _mod_spec(d, "gate_a", batch, row0),
                  pl.BlockSpec((1, d), lambda i: (0, 0)),
                  pl.BlockSpec((1, d), lambda i: (0, 0)),
                  _resident(wsb.shape), _resident(wssm.shape), _resident(wout.shape)],
        out_specs=pl.BlockSpec((tm, d), lambda i: (i, 0)),
        out_shape=jax.ShapeDtypeStruct((m, d), F32),
        compiler_params=_cparams(("arbitrary",)),
        name="merge",
    )(osb, ossm, zg, zg, x, mod, ln_g, ln_b, wsb, wssm, wout)


def _ffn_kernel(x1_ref, sc_ref, sh_ref, gf_ref, lg_ref, lb_ref, wup_ref, wdn_ref, o_ref, *, alpha, nchunk, batch):
    x1 = x1_ref[...]
    m = x1.shape[0]
    h2 = (_ln(x1) * (1.0 + _mod_rows(sc_ref, batch, m)) + _mod_rows(sh_ref, batch, m)).astype(BF16)
    dff = wup_ref.shape[1]
    cw = dff // nchunk
    f = None
    for ck in range(nchunk):
        a = jnp.maximum(_dot(h2, wup_ref[:, ck * cw:(ck + 1) * cw]), 0.0)
        t = _dot((a * a).astype(BF16), wdn_ref[ck * cw:(ck + 1) * cw, :])
        f = t if f is None else f + t
    o_ref[...] = _ln(alpha * x1 + _mod_rows(gf_ref, batch, m) * f) * lg_ref[...] + lb_ref[...]


def _ffn_call(x1, mod, ln_g, ln_b, wup, wdn, *, alpha, batch, row0):
    m, d = x1.shape
    tm = _row_tile(m, batch, 1024)
    return pl.pallas_call(
        functools.partial(_ffn_kernel, alpha=alpha, nchunk=4, batch=batch),
        grid=(m // tm,),
        in_specs=[pl.BlockSpec((tm, d), lambda i: (i, 0)),
                  _mod_spec(d, "scale_f", batch, row0),
                  _mod_spec(d, "shift_f", batch, row0),
                  _mod_spec(d, "gate_f", batch, row0),
                  pl.BlockSpec((1, d), lambda i: (0, 0)),
                  pl.BlockSpec((1, d), lambda i: (0, 0)),
                  _resident(wup.shape), _resident(wdn.shape)],
        out_specs=pl.BlockSpec((tm, d), lambda i: (i, 0)),
        out_shape=jax.ShapeDtypeStruct((m, d), F32),
        compiler_params=_cparams(("arbitrary",)),
        name="ffn",
    )(x1, mod, mod, mod, ln_g, ln_b, wup, wdn)


def _layer(x, mod, cache_k, cache_v, conv_prev, ssm_prev, wts, *, batch, row0, alpha):
    m, d = x.shape
    t = m // batch
    h = _ln_mod_call(x, mod, batch=batch, row0=row0)
    def proj(cols, outs, name, **kw):
        return _matmul_call(h, wts["w_in_t"], outs, name, row0=wts[cols][0], nrows=wts[cols][1], **kw)

    (q,) = proj("q_cols", [(BF16, SB_HEAD_DIM ** -0.5 * LOG2E)], "proj_q")
    kv_t = cache_k is None
    k32, kb = proj("k_cols", [(F32, 1.0), (BF16, 1.0)], "proj_k", feature_major=kv_t)
    v32, vb = proj("v_cols", [(F32, 1.0), (BF16, 1.0)], "proj_v", feature_major=kv_t)
    if kv_t:
        k32, v32 = k32.T, v32.T
    conv_ch = wts["conv_w"].shape[1]
    cprev = jnp.pad(conv_prev, ((0, 0), (CONV_PAD_ROWS - (CONV_WIDTH - 1), 0), (0, 0)))
    (xbc,) = proj("xbc_cols", [(F32, 1.0)], "proj_xbc")
    conv_new = xbc.reshape(batch, t, conv_ch)[:, t - (CONV_WIDTH - 1):]
    (zg,) = _matmul_call(h, wts["w_zg"], [(BF16, 1.0)], "proj_zg")
    (dtr,) = _matmul_call(h, wts["w_dt"], [(F32, 1.0)], "proj_dt")
    if cache_k is None:
        o_sb = _sb_prompt_call(q, kb, vb)
    else:
        o_sb = _sb_sample_call(q, kb, vb, cache_k, cache_v, tq=t, heads=q.shape[1] // SB_HEAD_DIM)
    npairs = ssm_prev.shape[1] // 2
    h0 = ssm_prev.reshape(batch, npairs, LANES, SSM_STATE)
    o_ssm, h_fin = _ssd_call(xbc, zg, dtr, cprev, h0, wts["conv_w"], wts["conv_b"], wts["dt_bias"], wts["a_log"],
                             wts["d_skip"], wts["ssm_norm_w"], batch=batch, lc=min(t, 128))
    x1 = _merge_call(o_sb, o_ssm, zg, x, mod, wts["ln_attn_g"], wts["ln_attn_b"],
                     wts["w_branch_sb"], wts["w_branch_ssm"], wts["w_out"], alpha=alpha, batch=batch, row0=row0)
    y = _ffn_call(x1, mod, wts["ln_ffn_g"], wts["ln_ffn_b"], wts["w_up"], wts["w_down"],
                  alpha=alpha, batch=batch, row0=row0)
    return y, k32, v32, conv_new, h_fin.reshape(ssm_prev.shape)


def _prep_weights(l, w_in, conv_w, conv_b, dt_bias, a_log, d_skip, ssm_norm_w, w_branch_sb, w_branch_ssm,
                  w_out, ln_attn_g, ln_attn_b, w_up, w_down, ln_ffn_g, ln_ffn_b, sbw):
    d = w_in.shape[1]
    inner = ssm_norm_w.shape[1]
    conv_ch = conv_w.shape[2]
    nh = a_log.shape[1]
    o_z = 3 * sbw
    o_xbc = o_z + inner
    o_dt = o_xbc + conv_ch
    o_g = o_dt + nh
    wit = jnp.transpose(w_in[l])
    pad_h = lambda v: jnp.pad(v, (0, LANES - nh))[None, :]
    return {
        "w_in_t": wit, "q_cols": (0, sbw), "k_cols": (sbw, sbw), "v_cols": (2 * sbw, sbw),
        "xbc_cols": (o_xbc, conv_ch),
        "w_zg": jnp.concatenate([wit[o_z:o_xbc], wit[o_g:]], axis=0),
        "w_dt": jnp.pad(wit[o_dt:o_g], ((0, LANES - nh), (0, 0))),
        "conv_w": conv_w[l], "conv_b": conv_b[l][None, :],
        "dt_bias": pad_h(dt_bias[l]), "a_log": pad_h(a_log[l]),
        "d_skip": jnp.repeat(d_skip[l], SSM_HEAD_DIM)[None, :],
        "ssm_norm_w": ssm_norm_w[l][None, :],
        "w_branch_sb": w_branch_sb[l].astype(BF16), "w_branch_ssm": w_branch_ssm[l].astype(BF16),
        "w_out": w_out[l].astype(BF16),
        "ln_attn_g": ln_attn_g[l][None, :], "ln_attn_b": ln_attn_b[l][None, :],
        "w_up": w_up[l].astype(BF16), "w_down": w_down[l].astype(BF16),
        "ln_ffn_g": ln_ffn_g[l][None, :], "ln_ffn_b": ln_ffn_b[l][None, :],
    }


def kernel(x_prompt, x_sample, cache_sb_k, cache_sb_v, state_conv, state_ssm, c_prompt, c_sample, w_ada, b_ada, w_in, conv_w, conv_b, dt_bias, a_log, d_skip, ssm_norm_w, w_branch_sb, w_branch_ssm, w_out, ln_attn_g, ln_attn_b, w_up, w_down, ln_ffn_g, ln_ffn_b):
    depth = w_ada.shape[0]
    alpha = (2 * depth) ** 0.25
    n_p, t_p, d = x_prompt.shape
    n_s, t_s, _ = x_sample.shape
    past = cache_sb_k.shape[2]
    sbw = cache_sb_k.shape[3] * cache_sb_k.shape[4]
    conv_ch = conv_w.shape[2]
    y_p = x_prompt.reshape(n_p * t_p, d)
    y_s = x_sample.reshape(n_s * t_s, d)
    assert n_p == 1
    row0_p = -(-n_s // SUBLANES) * SUBLANES
    c_all = jnp.concatenate([jnp.pad(c_sample, ((0, row0_p - n_s), (0, 0))),
                             jnp.pad(c_prompt, ((0, SUBLANES - n_p), (0, 0)))], axis=0)
    new_p, new_s = [], []
    for l in range(depth):
        wts = _prep_weights(l, w_in, conv_w, conv_b, dt_bias, a_log, d_skip, ssm_norm_w, w_branch_sb,
                            w_branch_ssm, w_out, ln_attn_g, ln_attn_b, w_up, w_down, ln_ffn_g, ln_ffn_b, sbw)
        mod = _mod_call(c_all, w_ada[l], b_ada[l][None, :])
        conv0 = jnp.zeros((n_p, CONV_WIDTH - 1, conv_ch), F32)
        ssm0 = jnp.zeros((n_p,) + state_ssm.shape[2:], F32)
        y_p, k_p, v_p, cv_p, s_p = _layer(y_p, mod, None, None, conv0, ssm0, wts,
                                          batch=n_p, row0=row0_p, alpha=alpha)
        y_s, k_s, v_s, cv_s, s_s = _layer(y_s, mod, jnp.transpose(cache_sb_k[l], (0, 2, 3, 1)),
                                          jnp.transpose(cache_sb_v[l], (0, 2, 3, 1)), state_conv[l], state_ssm[l],
                                          wts, batch=n_s, row0=0, alpha=alpha)
        hshape = cache_sb_k.shape[3:]
        new_p.append((k_p.reshape((n_p, t_p) + hshape), v_p.reshape((n_p, t_p) + hshape), cv_p, s_p))
        new_s.append((k_s.reshape((n_s, t_s) + hshape), v_s.reshape((n_s, t_s) + hshape), cv_s, s_s))
    stack = lambda lst, i: jnp.stack([e[i] for e in lst])
    return (y_p.reshape(n_p, t_p, d), y_s.reshape(n_s, t_s, d),
            stack(new_p, 0), stack(new_p, 1), stack(new_p, 2), stack(new_p, 3),
            stack(new_s, 0), stack(new_s, 1), stack(new_s, 2), stack(new_s, 3))
```

```python
import functools

import jax
import jax.numpy as jnp
from jax import lax
from jax.experimental import pallas as pl
from jax.experimental.pallas import tpu as pltpu

F32 = jnp.float32
BF16 = jnp.bfloat16

SB_HEAD_DIM = 64
SSM_HEAD_DIM = 64
SSM_STATE = 128
SSM_GROUPS = 8
CONV_WIDTH = 4
LN_EPS = 1e-5
LANES = 128
SUBLANES = 8
CONV_PAD_ROWS = 8
VMEM_LIMIT = 56 * 1024 * 1024
LOG2E = 1.4426950408889634
SB_DEAD_LOG2 = 150.0


def _cparams(sem):
    return pltpu.CompilerParams(dimension_semantics=sem, vmem_limit_bytes=VMEM_LIMIT)


def _dot(a, b):
    return jnp.dot(a, b, preferred_element_type=F32)


def _dot_nt(a, b):
    return lax.dot_general(a, b, (((1,), (1,)), ((), ())), preferred_element_type=F32)


def _split_bf16(x, n):
    parts, r = [], x
    for _ in range(n):
        p = r.astype(BF16)
        parts.append(p)
        r = r - p.astype(F32)
    return parts


def _dot_exact_rhs(a_bf16, b_f32, n=3):
    out = None
    for p in _split_bf16(b_f32, n):
        t = _dot(a_bf16, p)
        out = t if out is None else out + t
    return out


def _transpose_exact(eye_bf16, b_f32, n=3):
    out = None
    for p in _split_bf16(b_f32, n):
        t = _dot_nt(eye_bf16, p)
        out = t if out is None else out + t
    return out


def _softplus(x):
    return jnp.maximum(x, 0.0) + jnp.log(1.0 + jnp.exp(-jnp.abs(x)))


def _silu(x):
    h = 0.5 * x
    return h + h * jnp.tanh(h)


def _ln(x):
    mu = jnp.mean(x, axis=-1, keepdims=True)
    xc = x - mu
    var = jnp.mean(xc * xc, axis=-1, keepdims=True)
    return xc * lax.rsqrt(var + LN_EPS)


def _causal_conv_silu(xp, cw_ref, cb_ref):
    conv = cb_ref[...] + cw_ref[CONV_WIDTH - 1:CONV_WIDTH, :] * xp[CONV_PAD_ROWS:, :]
    for back in range(1, CONV_WIDTH):
        w = CONV_WIDTH - 1 - back
        conv = conv + cw_ref[w:w + 1, :] * pltpu.roll(xp, back, axis=0)[CONV_PAD_ROWS:, :]
    return _silu(conv)


def _eye(n, dtype):
    r = lax.broadcasted_iota(jnp.int32, (n, n), 0)
    c = lax.broadcasted_iota(jnp.int32, (n, n), 1)
    return jnp.where(r == c, 1.0, 0.0).astype(dtype)


MOD_PIECES = ("shift_a", "scale_a", "gate_a", "shift_f", "scale_f", "gate_f")


def _mod_spec(d, piece, batch, row0):
    col = MOD_PIECES.index(piece)
    if batch == 1:
        assert row0 % SUBLANES == 0
        return pl.BlockSpec((SUBLANES, d), lambda i: (row0 // SUBLANES, col))
    assert row0 == 0
    return pl.BlockSpec((batch, d), lambda i: (0, col))


def _mod_rows(ref, batch, m):
    if batch == 1:
        return ref[0:1, :]
    v = ref[...]
    return jnp.broadcast_to(v[:, None, :], (batch, m // batch, v.shape[1])).reshape(m, v.shape[1])


def _row_tile(m, batch, cap):
    tm = min(m, cap)
    assert batch == 1 or tm == m
    return tm


def _mod_kernel(c_ref, w_ref, b_ref, o_ref):
    s = _silu(c_ref[...])
    s_hi, s_lo = _split_bf16(s, 2)
    w_hi, w_lo = _split_bf16(w_ref[...], 2)
    o_ref[...] = _dot(s_hi, w_hi) + _dot(s_hi, w_lo) + _dot(s_lo, w_hi) + b_ref[...]


def _mod_call(c, w, b):
    r, d = c.shape
    n = w.shape[1]
    tn = 1024
    return pl.pallas_call(
        _mod_kernel,
        grid=(n // tn,),
        in_specs=[pl.BlockSpec((r, d), lambda j: (0, 0)),
                  pl.BlockSpec((d, tn), lambda j: (0, j)),
                  pl.BlockSpec((1, tn), lambda j: (0, j))],
        out_specs=pl.BlockSpec((r, tn), lambda j: (0, j)),
        out_shape=jax.ShapeDtypeStruct((r, n), F32),
        compiler_params=_cparams(("arbitrary",)),
        name="ada_mod",
    )(c, w, b)


def _ln_mod_kernel(x_ref, sc_ref, sh_ref, o_ref, *, batch):
    m = x_ref.shape[0]
    o_ref[...] = (_ln(x_ref[...]) * (1.0 + _mod_rows(sc_ref, batch, m))
                  + _mod_rows(sh_ref, batch, m)).astype(o_ref.dtype)


def _ln_mod_call(x, mod, *, batch, row0):
    m, d = x.shape
    tm = _row_tile(m, batch, 1024)
    return pl.pallas_call(
        functools.partial(_ln_mod_kernel, batch=batch),
        grid=(m // tm,),
        in_specs=[pl.BlockSpec((tm, d), lambda i: (i, 0)),
                  _mod_spec(d, "scale_a", batch, row0),
                  _mod_spec(d, "shift_a", batch, row0)],
        out_specs=pl.BlockSpec((tm, d), lambda i: (i, 0)),
        out_shape=jax.ShapeDtypeStruct((m, d), BF16),
        compiler_params=_cparams(("arbitrary",)),
        name="ln_mod",
    )(x, mod, mod)


def _matmul_kernel(x_ref, w_ref, *refs, scales, feature_major):
    o_refs = refs[:len(scales)]
    if w_ref.dtype == BF16:
        w = w_ref[...]
    else:
        wb_ref = refs[len(scales)]

        @pl.when(pl.program_id(1) == 0)
        def _():
            wb_ref[...] = w_ref[...].astype(BF16)

        w = wb_ref[...]
    acc = _dot_nt(w, x_ref[...]) if feature_major else _dot_nt(x_ref[...], w)
    for o_ref, s in zip(o_refs, scales):
        o_ref[...] = (acc if s == 1.0 else acc * s).astype(o_ref.dtype)


def _matmul_call(x, wt, outs, name, *, row0=0, nrows=None, feature_major=False):
    m, k = x.shape
    n = wt.shape[0] - row0 if nrows is None else nrows
    tm = min(m, 2048)
    tn = min(n, 1024)
    assert row0 % tn == 0 and n % tn == 0
    rb0 = row0 // tn
    if feature_major:
        out_spec, out_dims = pl.BlockSpec((tn, tm), lambda j, i: (j, i)), (n, m)
    else:
        out_spec, out_dims = pl.BlockSpec((tm, tn), lambda j, i: (i, j)), (m, n)
    res = pl.pallas_call(
        functools.partial(_matmul_kernel, scales=tuple(s for _, s in outs), feature_major=feature_major),
        grid=(n // tn, m // tm),
        in_specs=[pl.BlockSpec((tm, k), lambda j, i: (i, 0)),
                  pl.BlockSpec((tn, k), lambda j, i: (rb0 + j, 0))],
        out_specs=[out_spec for _ in outs],
        out_shape=[jax.ShapeDtypeStruct(out_dims, dt) for dt, _ in outs],
        scratch_shapes=[] if wt.dtype == BF16 else [pltpu.VMEM((tn, k), BF16)],
        compiler_params=_cparams(("arbitrary", "arbitrary")),
        name=name,
    )(x, wt)
    return res


def _suffix_matrix(tk):
    r = lax.broadcasted_iota(jnp.int32, (tk, tk), 0)
    c = lax.broadcasted_iota(jnp.int32, (tk, tk), 1)
    return jnp.where(r >= c, 1.0, 0.0).astype(BF16)


def _stacked_row_ids(n, tq, tk):
    one = lax.broadcasted_iota(jnp.int32, (tq, tk), 0)
    return jnp.concatenate([one] * n, axis=0)


def _sb_weights(z, c_run, suffix, visible):
    neg_abs = lax.bitcast_convert_type(lax.bitcast_convert_type(z, jnp.int32) | jnp.int32(-2 ** 31), F32)
    sp = jnp.maximum(z, 0.0) + jnp.log(1.0 + jnp.exp2(neg_abs)) * LOG2E
    if visible is not None:
        sp = jnp.where(visible, sp, 0.0)
    s = _dot(sp.astype(BF16), suffix) + c_run
    w = jnp.exp2(z - s)
    if visible is not None:
        w = jnp.where(visible, w, 0.0)
    return w.astype(BF16), s[:, 0:1]


def _sb_live(c_run):
    return (jnp.min(c_run) <= SB_DEAD_LOG2).astype(jnp.int32)


def _sb_prompt_kernel(q_ref, k_ref, v_ref, o_ref, *, tq, npair, nq):
    i = pl.program_id(1)
    low = lax.broadcasted_iota(jnp.int32, (tq, LANES), 1) < SB_HEAD_DIM
    suffix = _suffix_matrix(tq)
    rows = _stacked_row_ids(2, tq, tq)
    cols = lax.broadcasted_iota(jnp.int32, (2 * tq, tq), 1)
    units = [(s, slice(s * tq, (s + 1) * tq), slice(p * LANES, (p + 1) * LANES))
             for s in range(nq) for p in range(npair)]
    q_ab = []
    for _, rsl, lsl in units:
        qf = q_ref[rsl, lsl].astype(F32)
        q_ab.append(jnp.concatenate([jnp.where(low, qf, 0.0), jnp.where(low, 0.0, qf)], axis=0).astype(BF16))

    def block(key_block, carry, visible):
        out = []
        for u, ((s, _, lsl), (c_run, acc)) in enumerate(zip(units, carry)):
            at = pl.multiple_of(key_block(s) * tq, tq)
            w, c_run = _sb_weights(_dot(q_ab[u], k_ref[lsl, pl.ds(at, tq)]), c_run, suffix, visible(s))
            pv = _dot_nt(w, v_ref[lsl, pl.ds(at, tq)])
            out.append((c_run, acc + jnp.where(low, pv[:tq], pv[tq:])))
        return out

    def live(carry):
        return _sb_live(functools.reduce(jnp.minimum, [c for c, _ in carry]))

    def all_or_none(flag):
        return rows >= jnp.where(flag, 0, tq)

    qb = lambda s: i * nq + s
    carry = [(jnp.zeros((2 * tq, 1), F32), jnp.zeros((tq, LANES), F32)) for _ in units]
    carry = block(qb, carry, lambda s: cols < rows)
    carry = block(lambda s: jnp.maximum(qb(s) - 1, 0), carry, lambda s: all_or_none(qb(s) > 0))

    def cond(st):
        return jnp.logical_and(qb(nq - 1) - 2 - st[0] >= 0, st[1] > 0)

    def body(st):
        n, _, carry = st
        kb = lambda s: qb(s) - 2 - n
        carry = block(lambda s: jnp.maximum(kb(s), 0), carry,
                      (lambda s: None) if nq == 1 else (lambda s: all_or_none(kb(s) >= 0)))
        return n + 1, live(carry), carry

    _, _, carry = lax.while_loop(cond, body, (0, live(carry), carry))
    for (_, rsl, lsl), (_, acc) in zip(units, carry):
        o_ref[rsl, lsl] = acc.astype(o_ref.dtype)


def _sb_prompt_call(q, kt, vt):
    t, w = q.shape
    tq = min(t, 256)
    npair = 2 if w % (2 * LANES) == 0 else 1
    nq = 2 if t % (2 * tq) == 0 else 1
    wb = npair * LANES
    return pl.pallas_call(
        functools.partial(_sb_prompt_kernel, tq=tq, npair=npair, nq=nq),
        grid=(w // wb, t // (nq * tq)),
        in_specs=[pl.BlockSpec((nq * tq, wb), lambda p, i: (i, p)),
                  pl.BlockSpec((wb, t), lambda p, i: (p, 0)),
                  pl.BlockSpec((wb, t), lambda p, i: (p, 0))],
        out_specs=pl.BlockSpec((nq * tq, wb), lambda p, i: (i, p)),
        out_shape=jax.ShapeDtypeStruct((t, w), BF16),
        compiler_params=_cparams(("arbitrary", "arbitrary")),
        name="sb_prompt",
    )(q, kt, vt)


def _sb_sample_kernel(q_ref, kn_ref, vn_ref, kc_ref, vc_ref, o_ref, c_ref, acc_ref, live_ref, *, heads, tq, tk):
    j = pl.program_id(1)
    hd = SB_HEAD_DIM

    def stacked_block(qk_of, pv_of, suffix, visible):
        q = q_ref[...]
        z = jnp.concatenate([qk_of(q[:, h * hd:(h + 1) * hd], h) for h in range(heads)], axis=0)
        w, c_run = _sb_weights(z, c_ref[...], suffix, visible)
        acc_ref[...] += jnp.concatenate([pv_of(w[h * tq:(h + 1) * tq], h) for h in range(heads)], axis=0)
        c_ref[...] = c_run
        live_ref[0] = _sb_live(c_run)

    @pl.when(j == 0)
    def _():
        c_ref[...] = jnp.zeros_like(c_ref)
        acc_ref[...] = jnp.zeros_like(acc_ref)
        rows = _stacked_row_ids(heads, tq, tq)
        cols = lax.broadcasted_iota(jnp.int32, (heads * tq, tq), 1)
        stacked_block(lambda qh, h: _dot_nt(qh, kn_ref[:, h * hd:(h + 1) * hd]),
                      lambda wh, h: _dot(wh, vn_ref[:, h * hd:(h + 1) * hd]),
                      _suffix_matrix(tq), cols < rows)

    @pl.when(jnp.logical_and(j > 0, live_ref[0] > 0))
    def _():
        stacked_block(lambda qh, h: _dot(qh, kc_ref[0, h].astype(BF16)),
                      lambda wh, h: _dot_nt(wh, vc_ref[0, h].astype(BF16)),
                      _suffix_matrix(tk), None)

    @pl.when(j == pl.num_programs(1) - 1)
    def _():
        acc = acc_ref[...]
        o_ref[...] = jnp.concatenate([acc[h * tq:(h + 1) * tq] for h in range(heads)], axis=1).astype(o_ref.dtype)


def _sb_sample_call(q, kn, vn, kc, vc, *, tq, heads):
    m, w = q.shape
    b, _, _, past = kc.shape
    tk = min(past, 256)
    nblk = past // tk
    new_spec = pl.BlockSpec((tq, w), lambda bi, j: (bi, 0))
    cache_spec = pl.BlockSpec((1, heads, SB_HEAD_DIM, tk),
                              lambda bi, j: (bi, 0, 0, jnp.minimum(nblk - j, nblk - 1)))
    return pl.pallas_call(
        functools.partial(_sb_sample_kernel, heads=heads, tq=tq, tk=tk),
        grid=(b, nblk + 1),
        in_specs=[new_spec, new_spec, new_spec, cache_spec, cache_spec],
        out_specs=new_spec,
        out_shape=jax.ShapeDtypeStruct((m, w), BF16),
        scratch_shapes=[pltpu.VMEM((heads * tq, 1), F32),
                        pltpu.VMEM((heads * tq, SB_HEAD_DIM), F32),
                        pltpu.SMEM((1,), jnp.int32)],
        compiler_params=_cparams(("arbitrary", "arbitrary")),
        name="sb_sample",
    )(q, kn, vn, kc, vc)


def _ssd_kernel(xbc_ref, z_ref, dt_ref, cprev_ref, h0_ref, cw_ref, cb_ref, dtb_ref, alog_ref, dsk_ref,
                nw_ref, o_ref, hout_ref, xpad_ref, st_ref, act_ref, y_ref, *, lc, cps, inner, npairs):
    ci = pl.program_id(1)
    eye = _eye(LANES, BF16)
    rows_step = cps * lc

    @pl.when(ci == 0)
    def _():
        xpad_ref[0:CONV_PAD_ROWS, :] = cprev_ref[0]
        for p in range(npairs):
            st_ref[p] = h0_ref[0, p].T

    @pl.when(ci > 0)
    def _():
        xpad_ref[0:CONV_PAD_ROWS, :] = xpad_ref[rows_step:rows_step + CONV_PAD_ROWS, :]

    xpad_ref[CONV_PAD_ROWS:CONV_PAD_ROWS + rows_step, :] = xbc_ref[...]
    act_ref[...] = _causal_conv_silu(xpad_ref[...], cw_ref, cb_ref)

    dt_all = _softplus(dt_ref[...] + dtb_ref[...])
    a_neg = -jnp.exp(alog_ref[...])
    r = lax.broadcasted_iota(jnp.int32, (lc, lc), 0)
    c = lax.broadcasted_iota(jnp.int32, (lc, lc), 1)
    causal = r >= c
    prefix = jnp.where(causal, 1.0, 0.0).astype(BF16)
    acausal_off = jnp.where(causal, 0.0, -1e30)
    low = lax.broadcasted_iota(jnp.int32, (lc, LANES), 1) < SSM_HEAD_DIM
    low1 = lax.broadcasted_iota(jnp.int32, (1, LANES), 1) < SSM_HEAD_DIM

    def pair_cols(arr, ha):
        return jnp.where(low, arr[:, ha:ha + 1], arr[:, ha + 1:ha + 2])

    pairs_per_group = npairs // SSM_GROUPS
    for cc in range(cps):
        rs = slice(cc * lc, (cc + 1) * lc)
        dtv = dt_all[rs]
        a_cs = _dot_exact_rhs(prefix, dtv * a_neg)
        a_cs_t = _transpose_exact(eye, a_cs)
        total = a_cs[lc - 1:lc, :]
        ea = jnp.exp(a_cs)
        to_end = jnp.exp(total - a_cs)
        chunk_decay = jnp.exp(total)
        for g in range(SSM_GROUPS):
            b_bf = act_ref[rs, inner + g * SSM_STATE:inner + (g + 1) * SSM_STATE].astype(BF16)
            c_off = inner + SSM_GROUPS * SSM_STATE
            c_bf = act_ref[rs, c_off + g * SSM_STATE:c_off + (g + 1) * SSM_STATE].astype(BF16)
            cb = _dot_nt(c_bf, b_bf)
            b_t = _dot_nt(eye, b_bf).astype(BF16)
            for kk in range(pairs_per_group):
                p = g * pairs_per_group + kk
                ha = 2 * p
                sl = slice(p * LANES, (p + 1) * LANES)
                x = act_ref[rs, sl]
                xdt = x * pair_cols(dtv, ha)
                xdt_bf = xdt.astype(BF16)
                ys = []
                for h in (ha, ha + 1):
                    seg = a_cs[:, h:h + 1] - a_cs_t[h:h + 1, :]
                    decay = jnp.exp(seg + acausal_off)
                    ys.append(_dot((cb * decay).astype(BF16), xdt_bf))
                st = st_ref[p]
                y_off = _dot(c_bf, st.astype(BF16)) * pair_cols(ea, ha)
                y_ref[rs, sl] = jnp.where(low, ys[0], ys[1]) + y_off + dsk_ref[:, sl] * x
                xw = (xdt * pair_cols(to_end, ha)).astype(BF16)
                dec = jnp.where(low1, chunk_decay[:, ha:ha + 1], chunk_decay[:, ha + 1:ha + 2])
                st_ref[p] = st * dec + _dot(b_t, xw)

    gy = y_ref[...] * _silu(z_ref[...].astype(F32))
    gw = inner // SSM_GROUPS
    for g in range(SSM_GROUPS):
        seg = gy[:, g * gw:(g + 1) * gw]
        ms = jnp.mean(seg * seg, axis=-1, keepdims=True)
        o_ref[:, g * gw:(g + 1) * gw] = (seg * lax.rsqrt(ms + LN_EPS) * nw_ref[:, g * gw:(g + 1) * gw]).astype(o_ref.dtype)

    @pl.when(ci == pl.num_programs(1) - 1)
    def _():
        for p in range(npairs):
            hout_ref[0, p] = st_ref[p].T


def _ssd_call(xbc, zg, dtr, conv_prev, h0, cw, cb, dtb, alog, dsk, nw, *, batch, lc):
    m = xbc.shape[0]
    conv_ch = cw.shape[1]
    inner = nw.shape[1]
    npairs = h0.shape[1]
    chunks = m // batch // lc
    cps = 2 if chunks % 2 == 0 else 1
    rs = cps * lc
    nc = chunks // cps
    row = lambda b, c: b * nc + c
    const = lambda b, c: (0, 0)
    return pl.pallas_call(
        functools.partial(_ssd_kernel, lc=lc, cps=cps, inner=inner, npairs=npairs),
        grid=(batch, nc),
        in_specs=[pl.BlockSpec((rs, conv_ch), lambda b, c: (row(b, c), 0)),
                  pl.BlockSpec((rs, inner), lambda b, c: (row(b, c), 0)),
                  pl.BlockSpec((rs, LANES), lambda b, c: (row(b, c), 0)),
                  pl.BlockSpec((1, CONV_PAD_ROWS, conv_ch), lambda b, c: (b, 0, 0)),
                  pl.BlockSpec((1, npairs, LANES, LANES), lambda b, c: (b, 0, 0, 0)),
                  pl.BlockSpec((CONV_WIDTH, conv_ch), const),
                  pl.BlockSpec((1, conv_ch), const),
                  pl.BlockSpec((1, LANES), const),
                  pl.BlockSpec((1, LANES), const),
                  pl.BlockSpec((1, inner), const),
                  pl.BlockSpec((1, inner), const)],
        out_specs=[pl.BlockSpec((rs, inner), lambda b, c: (row(b, c), 0)),
                   pl.BlockSpec((1, npairs, LANES, LANES), lambda b, c: (b, 0, 0, 0))],
        out_shape=[jax.ShapeDtypeStruct((m, inner), BF16),
                   jax.ShapeDtypeStruct(h0.shape, F32)],
        scratch_shapes=[pltpu.VMEM((CONV_PAD_ROWS + rs, conv_ch), F32),
                        pltpu.VMEM((npairs, LANES, LANES), F32),
                        pltpu.VMEM((rs, conv_ch), F32),
                        pltpu.VMEM((rs, inner), F32)],
        compiler_params=_cparams(("arbitrary", "arbitrary")),
        name="ssd",
    )(xbc, zg, dtr, conv_prev, h0, cw, cb, dtb, alog, dsk, nw)


def _merge_kernel(osb_ref, ossm_ref, gsb_ref, gssm_ref, x_ref, ga_ref, lg_ref, lb_ref,
                  wsb_ref, wssm_ref, wout_ref, o_ref, *, alpha, batch):
    merged = (jax.nn.sigmoid(gsb_ref[...].astype(F32)) * _dot(osb_ref[...], wsb_ref[...])
              + jax.nn.sigmoid(gssm_ref[...].astype(F32)) * _dot(ossm_ref[...], wssm_ref[...]))
    gate = _mod_rows(ga_ref, batch, x_ref.shape[0])
    res = alpha * x_ref[...] + gate * _dot(merged.astype(BF16), wout_ref[...])
    o_ref[...] = _ln(res) * lg_ref[...] + lb_ref[...]


def _resident(shape):
    return pl.BlockSpec(shape, lambda i: (0, 0), pipeline_mode=pl.Buffered(1))


def _merge_call(osb, ossm, zg, x, mod, ln_g, ln_b, wsb, wssm, wout, *, alpha, batch, row0):
    m, d = x.shape
    sbw = osb.shape[1]
    inner = ossm.shape[1]
    tm = _row_tile(m, batch, 512)
    gcol = inner // d
    return pl.pallas_call(
        functools.partial(_merge_kernel, alpha=alpha, batch=batch),
        grid=(m // tm,),
        in_specs=[pl.BlockSpec((tm, sbw), lambda i: (i, 0)),
                  pl.BlockSpec((tm, inner), lambda i: (i, 0)),
                  pl.BlockSpec((tm, d), lambda i: (i, gcol)),
                  pl.BlockSpec((tm, d), lambda i: (i, gcol + 1)),
                  pl.BlockSpec((tm, d), lambda i: (i, 0)),
                  _mod_spec(d, "gate_a", batch, row0),
                  pl.BlockSpec((1, d), lambda i: (0, 0)),
                  pl.BlockSpec((1, d), lambda i: (0, 0)),
                  _resident(wsb.shape), _resident(wssm.shape), _resident(wout.shape)],
        out_specs=pl.BlockSpec((tm, d), lambda i: (i, 0)),
        out_shape=jax.ShapeDtypeStruct((m, d), F32),
        compiler_params=_cparams(("arbitrary",)),
        name="merge",
    )(osb, ossm, zg, zg, x, mod, ln_g, ln_b, wsb, wssm, wout)


def _ffn_kernel(x1_ref, sc_ref, sh_ref, gf_ref, lg_ref, lb_ref, wup_ref, wdn_ref, o_ref, *, alpha, nchunk, batch):
    x1 = x1_ref[...]
    m = x1.shape[0]
    h2 = (_ln(x1) * (1.0 + _mod_rows(sc_ref, batch, m)) + _mod_rows(sh_ref, batch, m)).astype(BF16)
    dff = wup_ref.shape[1]
    cw = dff // nchunk
    f = None
    for ck in range(nchunk):
        a = jnp.maximum(_dot(h2, wup_ref[:, ck * cw:(ck + 1) * cw]), 0.0)
        t = _dot((a * a).astype(BF16), wdn_ref[ck * cw:(ck + 1) * cw, :])
        f = t if f is None else f + t
    o_ref[...] = _ln(alpha * x1 + _mod_rows(gf_ref, batch, m) * f) * lg_ref[...] + lb_ref[...]


def _ffn_call(x1, mod, ln_g, ln_b, wup, wdn, *, alpha, batch, row0):
    m, d = x1.shape
    tm = _row_tile(m, batch, 1024)
    return pl.pallas_call(
        functools.partial(_ffn_kernel, alpha=alpha, nchunk=4, batch=batch),
        grid=(m // tm,),
        in_specs=[pl.BlockSpec((tm, d), lambda i: (i, 0)),
                  _mod_spec(d, "scale_f", batch, row0),
                  _mod_spec(d, "shift_f", batch, row0),
                  _mod_spec(d, "gate_f", batch, row0),
                  pl.BlockSpec((1, d), lambda i: (0, 0)),
                  pl.BlockSpec((1, d), lambda i: (0, 0)),
                  _resident(wup.shape), _resident(wdn.shape)],
        out_specs=pl.BlockSpec((tm, d), lambda i: (i, 0)),
        out_shape=jax.ShapeDtypeStruct((m, d), F32),
        compiler_params=_cparams(("arbitrary",)),
        name="ffn",
    )(x1, mod, mod, mod, ln_g, ln_b, wup, wdn)


def _layer(x, mod, cache_k, cache_v, conv_prev, ssm_prev, wts, *, batch, row0, alpha):
    m, d = x.shape
    t = m // batch
    h = _ln_mod_call(x, mod, batch=batch, row0=row0)
    def proj(cols, outs, name, **kw):
        return _matmul_call(h, wts["w_in_t"], outs, name, row0=wts[cols][0], nrows=wts[cols][1], **kw)

    (q,) = proj("q_cols", [(BF16, SB_HEAD_DIM ** -0.5 * LOG2E)], "proj_q")
    kv_t = cache_k is None
    k32, kb = proj("k_cols", [(F32, 1.0), (BF16, 1.0)], "proj_k", feature_major=kv_t)
    v32, vb = proj("v_cols", [(F32, 1.0), (BF16, 1.0)], "proj_v", feature_major=kv_t)
    if kv_t:
        k32, v32 = k32.T, v32.T
    conv_ch = wts["conv_w"].shape[1]
    cprev = jnp.pad(conv_prev, ((0, 0), (CONV_PAD_ROWS - (CONV_WIDTH - 1), 0), (0, 0)))
    (xbc,) = proj("xbc_cols", [(F32, 1.0)], "proj_xbc")
    conv_new = xbc.reshape(batch, t, conv_ch)[:, t - (CONV_WIDTH - 1):]
    (zg,) = _matmul_call(h, wts["w_zg"], [(BF16, 1.0)], "proj_zg")
    (dtr,) = _matmul_call(h, wts["w_dt"], [(F32, 1.0)], "proj_dt")
    if cache_k is None:
        o_sb = _sb_prompt_call(q, kb, vb)
    else:
        o_sb = _sb_sample_call(q, kb, vb, cache_k, cache_v, tq=t, heads=q.shape[1] // SB_HEAD_DIM)
    npairs = ssm_prev.shape[1] // 2
    h0 = ssm_prev.reshape(batch, npairs, LANES, SSM_STATE)
    o_ssm, h_fin = _ssd_call(xbc, zg, dtr, cprev, h0, wts["conv_w"], wts["conv_b"], wts["dt_bias"], wts["a_log"],
                             wts["d_skip"], wts["ssm_norm_w"], batch=batch, lc=min(t, 128))
    x1 = _merge_call(o_sb, o_ssm, zg, x, mod, wts["ln_attn_g"], wts["ln_attn_b"],
                     wts["w_branch_sb"], wts["w_branch_ssm"], wts["w_out"], alpha=alpha, batch=batch, row0=row0)
    y = _ffn_call(x1, mod, wts["ln_ffn_g"], wts["ln_ffn_b"], wts["w_up"], wts["w_down"],
                  alpha=alpha, batch=batch, row0=row0)
    return y, k32, v32, conv_new, h_fin.reshape(ssm_prev.shape)


def _prep_weights(l, w_in, conv_w, conv_b, dt_bias, a_log, d_skip, ssm_norm_w, w_branch_sb, w_branch_ssm,
                  w_out, ln_attn_g, ln_attn_b, w_up, w_down, ln_ffn_g, ln_ffn_b, sbw):
    d = w_in.shape[1]
    inner = ssm_norm_w.shape[1]
    conv_ch = conv_w.shape[2]
    nh = a_log.shape[1]
    o_z = 3 * sbw
    o_xbc = o_z + inner
    o_dt = o_xbc + conv_ch
    o_g = o_dt + nh
    wit = jnp.transpose(w_in[l])
    pad_h = lambda v: jnp.pad(v, (0, LANES - nh))[None, :]
    return {
        "w_in_t": wit, "q_cols": (0, sbw), "k_cols": (sbw, sbw), "v_cols": (2 * sbw, sbw),
        "xbc_cols": (o_xbc, conv_ch),
        "w_zg": jnp.concatenate([wit[o_z:o_xbc], wit[o_g:]], axis=0),
        "w_dt": jnp.pad(wit[o_dt:o_g], ((0, LANES - nh), (0, 0))),
        "conv_w": conv_w[l], "conv_b": conv_b[l][None, :],
        "dt_bias": pad_h(dt_bias[l]), "a_log": pad_h(a_log[l]),
        "d_skip": jnp.repeat(d_skip[l], SSM_HEAD_DIM)[None, :],
        "ssm_norm_w": ssm_norm_w[l][None, :],
        "w_branch_sb": w_branch_sb[l].astype(BF16), "w_branch_ssm": w_branch_ssm[l].astype(BF16),
        "w_out": w_out[l].astype(BF16),
        "ln_attn_g": ln_attn_g[l][None, :], "ln_attn_b": ln_attn_b[l][None, :],
        "w_up": w_up[l].astype(BF16), "w_down": w_down[l].astype(BF16),
        "ln_ffn_g": ln_ffn_g[l][None, :], "ln_ffn_b": ln_ffn_b[l][None, :],
    }


def kernel(x_prompt, x_sample, cache_sb_k, cache_sb_v, state_conv, state_ssm, c_prompt, c_sample, w_ada, b_ada, w_in, conv_w, conv_b, dt_bias, a_log, d_skip, ssm_norm_w, w_branch_sb, w_branch_ssm, w_out, ln_attn_g, ln_attn_b, w_up, w_down, ln_ffn_g, ln_ffn_b):
    depth = w_ada.shape[0]
    alpha = (2 * depth) ** 0.25
    n_p, t_p, d = x_prompt.shape
    n_s, t_s, _ = x_sample.shape
    past = cache_sb_k.shape[2]
    sbw = cache_sb_k.shape[3] * cache_sb_k.shape[4]
    conv_ch = conv_w.shape[2]
    y_p = x_prompt.reshape(n_p * t_p, d)
    y_s = x_sample.reshape(n_s * t_s, d)
    assert n_p == 1
    row0_p = -(-n_s // SUBLANES) * SUBLANES
    c_all = jnp.concatenate([jnp.pad(c_sample, ((0, row0_p - n_s), (0, 0))),
                             jnp.pad(c_prompt, ((0, SUBLANES - n_p), (0, 0)))], axis=0)
    new_p, new_s = [], []
    for l in range(depth):
        wts = _prep_weights(l, w_in, conv_w, conv_b, dt_bias, a_log, d_skip, ssm_norm_w, w_branch_sb,
                            w_branch_ssm, w_out, ln_attn_g, ln_attn_b, w_up, w_down, ln_ffn_g, ln_ffn_b, sbw)
        mod = _mod_call(c_all, w_ada[l], b_ada[l][None, :])
        conv0 = jnp.zeros((n_p, CONV_WIDTH - 1, conv_ch), F32)
        ssm0 = jnp.zeros((n_p,) + state_ssm.shape[2:], F32)
        y_p, k_p, v_p, cv_p, s_p = _layer(y_p, mod, None, None, conv0, ssm0, wts,
                                          batch=n_p, row0=row0_p, alpha=alpha)
        y_s, k_s, v_s, cv_s, s_s = _layer(y_s, mod, jnp.transpose(cache_sb_k[l], (0, 2, 3, 1)),
                                          jnp.transpose(cache_sb_v[l], (0, 2, 3, 1)), state_conv[l], state_ssm[l],
                                          wts, batch=n_s, row0=0, alpha=alpha)
        hshape = cache_sb_k.shape[3:]
        new_p.append((k_p.reshape((n_p, t_p) + hshape), v_p.reshape((n_p, t_p) + hshape), cv_p, s_p))
        new_s.append((k_s.reshape((n_s, t_s) + hshape), v_s.reshape((n_s, t_s) + hshape), cv_s, s_s))
    stack = lambda lst, i: jnp.stack([e[i] for e in lst])
    return (y_p.reshape(n_p, t_p, d), y_s.reshape(n_s, t_s, d),
            stack(new_p, 0), stack(new_p, 1), stack(new_p, 2), stack(new_p, 3),
            stack(new_s, 0), stack(new_s, 1), stack(new_s, 2), stack(new_s, 3))
```

```python
import functools

import jax
import jax.numpy as jnp
from jax import lax
from jax.experimental import pallas as pl
from jax.experimental.pallas import tpu as pltpu

F32 = jnp.float32
BF16 = jnp.bfloat16

SB_HEAD_DIM = 64
SSM_HEAD_DIM = 64
SSM_STATE = 128
SSM_GROUPS = 8
CONV_WIDTH = 4
LN_EPS = 1e-5
LANES = 128
SUBLANES = 8
CONV_PAD_ROWS = 8
VMEM_LIMIT = 56 * 1024 * 1024
LOG2E = 1.4426950408889634
SB_DEAD_LOG2 = 150.0


def _cparams(sem):
    return pltpu.CompilerParams(dimension_semantics=sem, vmem_limit_bytes=VMEM_LIMIT)


def _dot(a, b):
    return jnp.dot(a, b, preferred_element_type=F32)


def _dot_nt(a, b):
    return lax.dot_general(a, b, (((1,), (1,)), ((), ())), preferred_element_type=F32)


def _split_bf16(x, n):
    parts, r = [], x
    for _ in range(n):
        p = r.astype(BF16)
        parts.append(p)
        r = r - p.astype(F32)
    return parts


def _dot_exact_rhs(a_bf16, b_f32, n=3):
    out = None
    for p in _split_bf16(b_f32, n):
        t = _dot(a_bf16, p)
        out = t if out is None else out + t
    return out


def _transpose_exact(eye_bf16, b_f32, n=3):
    out = None
    for p in _split_bf16(b_f32, n):
        t = _dot_nt(eye_bf16, p)
        out = t if out is None else out + t
    return out


def _softplus(x):
    return jnp.maximum(x, 0.0) + jnp.log(1.0 + jnp.exp(-jnp.abs(x)))


def _silu(x):
    h = 0.5 * x
    return h + h * jnp.tanh(h)


def _ln(x):
    mu = jnp.mean(x, axis=-1, keepdims=True)
    xc = x - mu
    var = jnp.mean(xc * xc, axis=-1, keepdims=True)
    return xc * lax.rsqrt(var + LN_EPS)


def _causal_conv_silu(xp, cw_ref, cb_ref):
    conv = cb_ref[...] + cw_ref[CONV_WIDTH - 1:CONV_WIDTH, :] * xp[CONV_PAD_ROWS:, :]
    for back in range(1, CONV_WIDTH):
        w = CONV_WIDTH - 1 - back
        conv = conv + cw_ref[w:w + 1, :] * pltpu.roll(xp, back, axis=0)[CONV_PAD_ROWS:, :]
    return _silu(conv)


def _eye(n, dtype):
    r = lax.broadcasted_iota(jnp.int32, (n, n), 0)
    c = lax.broadcasted_iota(jnp.int32, (n, n), 1)
    return jnp.where(r == c, 1.0, 0.0).astype(dtype)


MOD_PIECES = ("shift_a", "scale_a", "gate_a", "shift_f", "scale_f", "gate_f")


def _mod_spec(d, piece, batch, row0):
    col = MOD_PIECES.index(piece)
    if batch == 1:
        assert row0 % SUBLANES == 0
        return pl.BlockSpec((SUBLANES, d), lambda i: (row0 // SUBLANES, col))
    assert row0 == 0
    return pl.BlockSpec((batch, d), lambda i: (0, col))


def _mod_rows(ref, batch, m):
    if batch == 1:
        return ref[0:1, :]
    v = ref[...]
    return jnp.broadcast_to(v[:, None, :], (batch, m // batch, v.shape[1])).reshape(m, v.shape[1])


def _row_tile(m, batch, cap):
    tm = min(m, cap)
    assert batch == 1 or tm == m
    return tm


def _mod_kernel(c_ref, w_ref, b_ref, o_ref):
    s = _silu(c_ref[...])
    s_hi, s_lo = _split_bf16(s, 2)
    w_hi, w_lo = _split_bf16(w_ref[...], 2)
    o_ref[...] = _dot(s_hi, w_hi) + _dot(s_hi, w_lo) + _dot(s_lo, w_hi) + b_ref[...]


def _mod_call(c, w, b):
    r, d = c.shape
    n = w.shape[1]
    tn = 1024
    return pl.pallas_call(
        _mod_kernel,
        grid=(n // tn,),
        in_specs=[pl.BlockSpec((r, d), lambda j: (0, 0)),
                  pl.BlockSpec((d, tn), lambda j: (0, j)),
                  pl.BlockSpec((1, tn), lambda j: (0, j))],
        out_specs=pl.BlockSpec((r, tn), lambda j: (0, j)),
        out_shape=jax.ShapeDtypeStruct((r, n), F32),
        compiler_params=_cparams(("arbitrary",)),
        name="ada_mod",
    )(c, w, b)


def _ln_mod_kernel(x_ref, sc_ref, sh_ref, o_ref, *, batch):
    m = x_ref.shape[0]
    o_ref[...] = (_ln(x_ref[...]) * (1.0 + _mod_rows(sc_ref, batch, m))
                  + _mod_rows(sh_ref, batch, m)).astype(o_ref.dtype)


def _ln_mod_call(x, mod, *, batch, row0):
    m, d = x.shape
    tm = _row_tile(m, batch, 1024)
    return pl.pallas_call(
        functools.partial(_ln_mod_kernel, batch=batch),
        grid=(m // tm,),
        in_specs=[pl.BlockSpec((tm, d), lambda i: (i, 0)),
                  _mod_spec(d, "scale_a", batch, row0),
                  _mod_spec(d, "shift_a", batch, row0)],
        out_specs=pl.BlockSpec((tm, d), lambda i: (i, 0)),
        out_shape=jax.ShapeDtypeStruct((m, d), BF16),
        compiler_params=_cparams(("arbitrary",)),
        name="ln_mod",
    )(x, mod, mod)


def _matmul_kernel(x_ref, w_ref, *refs, scales, feature_major):
    o_refs = refs[:len(scales)]
    if w_ref.dtype == BF16:
        w = w_ref[...]
    else:
        wb_ref = refs[len(scales)]

        @pl.when(pl.program_id(1) == 0)
        def _():
            wb_ref[...] = w_ref[...].astype(BF16)

        w = wb_ref[...]
    acc = _dot_nt(w, x_ref[...]) if feature_major else _dot_nt(x_ref[...], w)
    for o_ref, s in zip(o_refs, scales):
        o_ref[...] = (acc if s == 1.0 else acc * s).astype(o_ref.dtype)


def _matmul_call(x, wt, outs, name, *, row0=0, nrows=None, feature_major=False):
    m, k = x.shape
    n = wt.shape[0] - row0 if nrows is None else nrows
    tm = min(m, 2048)
    tn = min(n, 1024)
    assert row0 % tn == 0 and n % tn == 0
    rb0 = row0 // tn
    if feature_major:
        out_spec, out_dims = pl.BlockSpec((tn, tm), lambda j, i: (j, i)), (n, m)
    else:
        out_spec, out_dims = pl.BlockSpec((tm, tn), lambda j, i: (i, j)), (m, n)
    res = pl.pallas_call(
        functools.partial(_matmul_kernel, scales=tuple(s for _, s in outs), feature_major=feature_major),
        grid=(n // tn, m // tm),
        in_specs=[pl.BlockSpec((tm, k), lambda j, i: (i, 0)),
                  pl.BlockSpec((tn, k), lambda j, i: (rb0 + j, 0))],
        out_specs=[out_spec for _ in outs],
        out_shape=[jax.ShapeDtypeStruct(out_dims, dt) for dt, _ in outs],
        scratch_shapes=[] if wt.dtype == BF16 else [pltpu.VMEM((tn, k), BF16)],
        compiler_params=_cparams(("arbitrary", "arbitrary")),
        name=name,
    )(x, wt)
    return res


def _suffix_matrix(tk):
    r = lax.broadcasted_iota(jnp.int32, (tk, tk), 0)
    c = lax.broadcasted_iota(jnp.int32, (tk, tk), 1)
    return jnp.where(r >= c, 1.0, 0.0).astype(BF16)


def _stacked_row_ids(n, tq, tk):
    one = lax.broadcasted_iota(jnp.int32, (tq, tk), 0)
    return jnp.concatenate([one] * n, axis=0)


def _sb_weights(z, c_run, suffix, visible):
    neg_abs = lax.bitcast_convert_type(lax.bitcast_convert_type(z, jnp.int32) | jnp.int32(-2 ** 31), F32)
    sp = jnp.maximum(z, 0.0) + jnp.log(1.0 + jnp.exp2(neg_abs)) * LOG2E
    if visible is not None:
        sp = jnp.where(visible, sp, 0.0)
    s = _dot(sp.astype(BF16), suffix) + c_run
    w = jnp.exp2(z - s)
    if visible is not None:
        w = jnp.where(visible, w, 0.0)
    return w.astype(BF16), s[:, 0:1]


def _sb_live(c_run):
    return (jnp.min(c_run) <= SB_DEAD_LOG2).astype(jnp.int32)


def _sb_prompt_kernel(q_ref, k_ref, v_ref, o_ref, *, tq, npair, nq):
    i = pl.program_id(1)
    low = lax.broadcasted_iota(jnp.int32, (tq, LANES), 1) < SB_HEAD_DIM
    suffix = _suffix_matrix(tq)
    rows = _stacked_row_ids(2, tq, tq)
    cols = lax.broadcasted_iota(jnp.int32, (2 * tq, tq), 1)
    units = [(s, slice(s * tq, (s + 1) * tq), slice(p * LANES, (p + 1) * LANES))
             for s in range(nq) for p in range(npair)]
    q_ab = []
    for _, rsl, lsl in units:
        qf = q_ref[rsl, lsl].astype(F32)
        q_ab.append(jnp.concatenate([jnp.where(low, qf, 0.0), jnp.where(low, 0.0, qf)], axis=0).astype(BF16))

    def block(key_block, carry, visible):
        out = []
        for u, ((s, _, lsl), (c_run, acc)) in enumerate(zip(units, carry)):
            at = pl.multiple_of(key_block(s) * tq, tq)
            w, c_run = _sb_weights(_dot(q_ab[u], k_ref[lsl, pl.ds(at, tq)]), c_run, suffix, visible(s))
            pv = _dot_nt(w, v_ref[lsl, pl.ds(at, tq)])
            out.append((c_run, acc + jnp.where(low, pv[:tq], pv[tq:])))
        return out

    def live(carry):
        return _sb_live(functools.reduce(jnp.minimum, [c for c, _ in carry]))

    def all_or_none(flag):
        return rows >= jnp.where(flag, 0, tq)

    qb = lambda s: i * nq + s
    carry = [(jnp.zeros((2 * tq, 1), F32), jnp.zeros((tq, LANES), F32)) for _ in units]
    carry = block(qb, carry, lambda s: cols < rows)
    carry = block(lambda s: jnp.maximum(qb(s) - 1, 0), carry, lambda s: all_or_none(qb(s) > 0))

    def cond(st):
        return jnp.logical_and(qb(nq - 1) - 2 - st[0] >= 0, st[1] > 0)

    def body(st):
        n, _, carry = st
        kb = lambda s: qb(s) - 2 - n
        carry = block(lambda s: jnp.maximum(kb(s), 0), carry,
                      (lambda s: None) if nq == 1 else (lambda s: all_or_none(kb(s) >= 0)))
        return n + 1, live(carry), carry

    _, _, carry = lax.while_loop(cond, body, (0, live(carry), carry))
    for (_, rsl, lsl), (_, acc) in zip(units, carry):
        o_ref[rsl, lsl] = acc.astype(o_ref.dtype)


def _sb_prompt_call(q, kt, vt):
    t, w = q.shape
    tq = min(t, 256)
    npair = 2 if w % (2 * LANES) == 0 else 1
    nq = 2 if t % (2 * tq) == 0 else 1
    wb = npair * LANES
    return pl.pallas_call(
        functools.partial(_sb_prompt_kernel, tq=tq, npair=npair, nq=nq),
        grid=(w // wb, t // (nq * tq)),
        in_specs=[pl.BlockSpec((nq * tq, wb), lambda p, i: (i, p)),
                  pl.BlockSpec((wb, t), lambda p, i: (p, 0)),
                  pl.BlockSpec((wb, t), lambda p, i: (p, 0))],
        out_specs=pl.BlockSpec((nq * tq, wb), lambda p, i: (i, p)),
        out_shape=jax.ShapeDtypeStruct((t, w), BF16),
        compiler_params=_cparams(("arbitrary", "arbitrary")),
        name="sb_prompt",
    )(q, kt, vt)


def _sb_sample_kernel(q_ref, kn_ref, vn_ref, kc_hbm, vc_hbm, o_ref, kbuf, vbuf, sem, *, heads, tq, tk, nblk):
    b = pl.program_id(0)
    hd = SB_HEAD_DIM

    def copies(j, slot):
        at = pl.multiple_of(j * tk, tk)
        return (pltpu.make_async_copy(kc_hbm.at[b, :, :, pl.ds(at, tk)], kbuf.at[slot], sem.at[0, slot]),
                pltpu.make_async_copy(vc_hbm.at[b, :, :, pl.ds(at, tk)], vbuf.at[slot], sem.at[1, slot]))

    def start(j, slot):
        for cp in copies(j, slot):
            cp.start()

    def wait(j, slot):
        for cp in copies(j, slot):
            cp.wait()

    slot_of = lambda j: lax.rem(nblk - 1 - j, 2)

    def stacked_block(qk_of, pv_of, suffix, visible, c_run, acc):
        q = q_ref[...]
        z = jnp.concatenate([qk_of(q[:, h * hd:(h + 1) * hd], h) for h in range(heads)], axis=0)
        w, c_run = _sb_weights(z, c_run, suffix, visible)
        acc = acc + jnp.concatenate([pv_of(w[h * tq:(h + 1) * tq], h) for h in range(heads)], axis=0)
        return c_run, acc

    start(nblk - 1, 0)
    rows = _stacked_row_ids(heads, tq, tq)
    cols = lax.broadcasted_iota(jnp.int32, (heads * tq, tq), 1)
    c_run, acc = stacked_block(lambda qh, h: _dot_nt(qh, kn_ref[:, h * hd:(h + 1) * hd]),
                               lambda wh, h: _dot(wh, vn_ref[:, h * hd:(h + 1) * hd]),
                               _suffix_matrix(tq), cols < rows,
                               jnp.zeros((heads * tq, 1), F32), jnp.zeros((heads * tq, hd), F32))
    suffix = _suffix_matrix(tk)

    def cond(st):
        return jnp.logical_and(st[0] >= 0, st[1] > 0)

    def body(st):
        j, _, c_run, acc = st
        slot = slot_of(j)
        wait(j, slot)

        @pl.when(j > 0)
        def _():
            start(j - 1, 1 - slot)

        c_run, acc = stacked_block(lambda qh, h: _dot(qh, kbuf[slot, h].astype(BF16)),
                                   lambda wh, h: _dot_nt(wh, vbuf[slot, h].astype(BF16)),
                                   suffix, None, c_run, acc)
        return j - 1, _sb_live(c_run), c_run, acc

    j_end, _, _, acc = lax.while_loop(cond, body, (nblk - 1, _sb_live(c_run), c_run, acc))

    @pl.when(j_end >= 0)
    def _():
        wait(j_end, slot_of(j_end))

    o_ref[...] = jnp.concatenate([acc[h * tq:(h + 1) * tq] for h in range(heads)], axis=1).astype(o_ref.dtype)


def _sb_sample_call(q, kn, vn, kc, vc, *, tq, heads):
    m, w = q.shape
    b, _, _, past = kc.shape
    tk = min(past, 256)
    new_spec = pl.BlockSpec((tq, w), lambda bi: (bi, 0))
    hbm_spec = pl.BlockSpec(memory_space=pl.ANY)
    return pl.pallas_call(
        functools.partial(_sb_sample_kernel, heads=heads, tq=tq, tk=tk, nblk=past // tk),
        grid=(b,),
        in_specs=[new_spec, new_spec, new_spec, hbm_spec, hbm_spec],
        out_specs=new_spec,
        out_shape=jax.ShapeDtypeStruct((m, w), BF16),
        scratch_shapes=[pltpu.VMEM((2, heads, SB_HEAD_DIM, tk), F32),
                        pltpu.VMEM((2, heads, SB_HEAD_DIM, tk), F32),
                        pltpu.SemaphoreType.DMA((2, 2))],
        compiler_params=_cparams(("arbitrary",)),
        name="sb_sample",
    )(q, kn, vn, kc, vc)


def _ssd_kernel(xbc_ref, z_ref, dt_ref, cprev_ref, h0_ref, cw_ref, cb_ref, dtb_ref, alog_ref, dsk_ref,
                nw_ref, o_ref, hout_ref, xpad_ref, st_ref, act_ref, y_ref, *, lc, cps, inner, npairs):
    ci = pl.program_id(1)
    eye = _eye(LANES, BF16)
    rows_step = cps * lc

    @pl.when(ci == 0)
    def _():
        xpad_ref[0:CONV_PAD_ROWS, :] = cprev_ref[0]
        for p in range(npairs):
            st_ref[p] = h0_ref[0, p].T

    @pl.when(ci > 0)
    def _():
        xpad_ref[0:CONV_PAD_ROWS, :] = xpad_ref[rows_step:rows_step + CONV_PAD_ROWS, :]

    xpad_ref[CONV_PAD_ROWS:CONV_PAD_ROWS + rows_step, :] = xbc_ref[...]
    act_ref[...] = _causal_conv_silu(xpad_ref[...], cw_ref, cb_ref)

    dt_all = _softplus(dt_ref[...] + dtb_ref[...])
    a_neg = -jnp.exp(alog_ref[...])
    r = lax.broadcasted_iota(jnp.int32, (lc, lc), 0)
    c = lax.broadcasted_iota(jnp.int32, (lc, lc), 1)
    causal = r >= c
    prefix = jnp.where(causal, 1.0, 0.0).astype(BF16)
    acausal_off = jnp.where(causal, 0.0, -1e30)
    low = lax.broadcasted_iota(jnp.int32, (lc, LANES), 1) < SSM_HEAD_DIM
    low1 = lax.broadcasted_iota(jnp.int32, (1, LANES), 1) < SSM_HEAD_DIM

    def pair_cols(arr, ha):
        return jnp.where(low, arr[:, ha:ha + 1], arr[:, ha + 1:ha + 2])

    pairs_per_group = npairs // SSM_GROUPS
    for cc in range(cps):
        rs = slice(cc * lc, (cc + 1) * lc)
        dtv = dt_all[rs]
        a_cs = _dot_exact_rhs(prefix, dtv * a_neg)
        a_cs_t = _transpose_exact(eye, a_cs)
        total = a_cs[lc - 1:lc, :]
        ea = jnp.exp(a_cs)
        to_end = jnp.exp(total - a_cs)
        chunk_decay = jnp.exp(total)
        for g in range(SSM_GROUPS):
            b_bf = act_ref[rs, inner + g * SSM_STATE:inner + (g + 1) * SSM_STATE].astype(BF16)
            c_off = inner + SSM_GROUPS * SSM_STATE
            c_bf = act_ref[rs, c_off + g * SSM_STATE:c_off + (g + 1) * SSM_STATE].astype(BF16)
            cb = _dot_nt(c_bf, b_bf)
            b_t = _dot_nt(eye, b_bf).astype(BF16)
            for kk in range(pairs_per_group):
                p = g * pairs_per_group + kk
                ha = 2 * p
                sl = slice(p * LANES, (p + 1) * LANES)
                x = act_ref[rs, sl]
                xdt = x * pair_cols(dtv, ha)
                xdt_bf = xdt.astype(BF16)
                ys = []
                for h in (ha, ha + 1):
                    seg = a_cs[:, h:h + 1] - a_cs_t[h:h + 1, :]
                    decay = jnp.exp(seg + acausal_off)
                    ys.append(_dot((cb * decay).astype(BF16), xdt_bf))
                st = st_ref[p]
                y_off = _dot(c_bf, st.astype(BF16)) * pair_cols(ea, ha)
                y_ref[rs, sl] = jnp.where(low, ys[0], ys[1]) + y_off + dsk_ref[:, sl] * x
                xw = (xdt * pair_cols(to_end, ha)).astype(BF16)
                dec = jnp.where(low1, chunk_decay[:, ha:ha + 1], chunk_decay[:, ha + 1:ha + 2])
                st_ref[p] = st * dec + _dot(b_t, xw)

    gy = y_ref[...] * _silu(z_ref[...].astype(F32))
    gw = inner // SSM_GROUPS
    for g in range(SSM_GROUPS):
        seg = gy[:, g * gw:(g + 1) * gw]
        ms = jnp.mean(seg * seg, axis=-1, keepdims=True)
        o_ref[:, g * gw:(g + 1) * gw] = (seg * lax.rsqrt(ms + LN_EPS) * nw_ref[:, g * gw:(g + 1) * gw]).astype(o_ref.dtype)

    @pl.when(ci == pl.num_programs(1) - 1)
    def _():
        for p in range(npairs):
            hout_ref[0, p] = st_ref[p].T


def _ssd_call(xbc, zg, dtr, conv_prev, h0, cw, cb, dtb, alog, dsk, nw, *, batch, lc):
    m = xbc.shape[0]
    conv_ch = cw.shape[1]
    inner = nw.shape[1]
    npairs = h0.shape[1]
    chunks = m // batch // lc
    cps = 2 if chunks % 2 == 0 else 1
    rs = cps * lc
    nc = chunks // cps
    row = lambda b, c: b * nc + c
    const = lambda b, c: (0, 0)
    return pl.pallas_call(
        functools.partial(_ssd_kernel, lc=lc, cps=cps, inner=inner, npairs=npairs),
        grid=(batch, nc),
        in_specs=[pl.BlockSpec((rs, conv_ch), lambda b, c: (row(b, c), 0)),
                  pl.BlockSpec((rs, inner), lambda b, c: (row(b, c), 0)),
                  pl.BlockSpec((rs, LANES), lambda b, c: (row(b, c), 0)),
                  pl.BlockSpec((1, CONV_PAD_ROWS, conv_ch), lambda b, c: (b, 0, 0)),
                  pl.BlockSpec((1, npairs, LANES, LANES), lambda b, c: (b, 0, 0, 0)),
                  pl.BlockSpec((CONV_WIDTH, conv_ch), const),
                  pl.BlockSpec((1, conv_ch), const),
                  pl.BlockSpec((1, LANES), const),
                  pl.BlockSpec((1, LANES), const),
                  pl.BlockSpec((1, inner), const),
                  pl.BlockSpec((1, inner), const)],
        out_specs=[pl.BlockSpec((rs, inner), lambda b, c: (row(b, c), 0)),
                   pl.BlockSpec((1, npairs, LANES, LANES), lambda b, c: (b, 0, 0, 0))],
        out_shape=[jax.ShapeDtypeStruct((m, inner), BF16),
                   jax.ShapeDtypeStruct(h0.shape, F32)],
        scratch_shapes=[pltpu.VMEM((CONV_PAD_ROWS + rs, conv_ch), F32),
                        pltpu.VMEM((npairs, LANES, LANES), F32),
                        pltpu.VMEM((rs, conv_ch), F32),
                        pltpu.VMEM((rs, inner), F32)],
        compiler_params=_cparams(("arbitrary", "arbitrary")),
        name="ssd",
    )(xbc, zg, dtr, conv_prev, h0, cw, cb, dtb, alog, dsk, nw)


def _merge_kernel(osb_ref, ossm_ref, gsb_ref, gssm_ref, x_ref, ga_ref, lg_ref, lb_ref,
                  wsb_ref, wssm_ref, wout_ref, o_ref, *, alpha, batch):
    merged = (jax.nn.sigmoid(gsb_ref[...].astype(F32)) * _dot(osb_ref[...], wsb_ref[...])
              + jax.nn.sigmoid(gssm_ref[...].astype(F32)) * _dot(ossm_ref[...], wssm_ref[...]))
    gate = _mod_rows(ga_ref, batch, x_ref.shape[0])
    res = alpha * x_ref[...] + gate * _dot(merged.astype(BF16), wout_ref[...])
    o_ref[...] = _ln(res) * lg_ref[...] + lb_ref[...]


def _resident(shape):
    return pl.BlockSpec(shape, lambda i: (0, 0), pipeline_mode=pl.Buffered(1))


def _merge_call(osb, ossm, zg, x, mod, ln_g, ln_b, wsb, wssm, wout, *, alpha, batch, row0):
    m, d = x.shape
    sbw = osb.shape[1]
    inner = ossm.shape[1]
    tm = _row_tile(m, batch, 512)
    gcol = inner // d
    return pl.pallas_call(
        functools.partial(_merge_kernel, alpha=alpha, batch=batch),
        grid=(m // tm,),
        in_specs=[pl.BlockSpec((tm, sbw), lambda i: (i, 0)),
                  pl.BlockSpec((tm, inner), lambda i: (i, 0)),
                  pl.BlockSpec((tm, d), lambda i: (i, gcol)),
                  pl.BlockSpec((tm, d), lambda i: (i, gcol + 1)),
                  pl.BlockSpec((tm, d), lambda i: (i, 0)),
                  _mod_spec(d, "gate_a", batch, row0),
                  pl.BlockSpec((1, d), lambda i: (0, 0)),
                  pl.BlockSpec((1, d), lambda i: (0, 0)),
                  _resident(wsb.shape), _resident(wssm.shape), _resident(wout.shape)],
        out_specs=pl.BlockSpec((tm, d), lambda i: (i, 0)),
        out_shape=jax.ShapeDtypeStruct((m, d), F32),
        compiler_params=_cparams(("arbitrary",)),
        name="merge",
    )(osb, ossm, zg, zg, x, mod, ln_g, ln_b, wsb, wssm, wout)


def _ffn_kernel(x1_ref, sc_ref, sh_ref, gf_ref, lg_ref, lb_ref, wup_ref, wdn_ref, o_ref, *, alpha, nchunk, batch):
    x1 = x1_ref[...]
    m = x1.shape[0]
    h2 = (_ln(x1) * (1.0 + _mod_rows(sc_ref, batch, m)) + _mod_rows(sh_ref, batch, m)).astype(BF16)
    dff = wup_ref.shape[1]
    cw = dff // nchunk
    f = None
    for ck in range(nchunk):
        a = jnp.maximum(_dot(h2, wup_ref[:, ck * cw:(ck + 1) * cw]), 0.0)
        t = _dot((a * a).astype(BF16), wdn_ref[ck * cw:(ck + 1) * cw, :])
        f = t if f is None else f + t
    o_ref[...] = _ln(alpha * x1 + _mod_rows(gf_ref, batch, m) * f) * lg_ref[...] + lb_ref[...]


def _ffn_call(x1, mod, ln_g, ln_b, wup, wdn, *, alpha, batch, row0):
    m, d = x1.shape
    tm = _row_tile(m, batch, 1024)
    return pl.pallas_call(
        functools.partial(_ffn_kernel, alpha=alpha, nchunk=4, batch=batch),
        grid=(m // tm,),
        in_specs=[pl.BlockSpec((tm, d), lambda i: (i, 0)),
                  _mod_spec(d, "scale_f", batch, row0),
                  _mod_spec(d, "shift_f", batch, row0),
                  _mod_spec(d, "gate_f", batch, row0),
                  pl.BlockSpec((1, d), lambda i: (0, 0)),
                  pl.BlockSpec((1, d), lambda i: (0, 0)),
                  _resident(wup.shape), _resident(wdn.shape)],
        out_specs=pl.BlockSpec((tm, d), lambda i: (i, 0)),
        out_shape=jax.ShapeDtypeStruct((m, d), F32),
        compiler_params=_cparams(("arbitrary",)),
        name="ffn",
    )(x1, mod, mod, mod, ln_g, ln_b, wup, wdn)


def _layer(x, mod, cache_k, cache_v, conv_prev, ssm_prev, wts, *, batch, row0, alpha):
    m, d = x.shape
    t = m // batch
    h = _ln_mod_call(x, mod, batch=batch, row0=row0)
    def proj(cols, outs, name, **kw):
        return _matmul_call(h, wts["w_in_t"], outs, name, row0=wts[cols][0], nrows=wts[cols][1], **kw)

    (q,) = proj("q_cols", [(BF16, SB_HEAD_DIM ** -0.5 * LOG2E)], "proj_q")
    kv_t = cache_k is None
    k32, kb = proj("k_cols", [(F32, 1.0), (BF16, 1.0)], "proj_k", feature_major=kv_t)
    v32, vb = proj("v_cols", [(F32, 1.0), (BF16, 1.0)], "proj_v", feature_major=kv_t)
    if kv_t:
        k32, v32 = k32.T, v32.T
    conv_ch = wts["conv_w"].shape[1]
    cprev = jnp.pad(conv_prev, ((0, 0), (CONV_PAD_ROWS - (CONV_WIDTH - 1), 0), (0, 0)))
    (xbc,) = proj("xbc_cols", [(F32, 1.0)], "proj_xbc")
    conv_new = xbc.reshape(batch, t, conv_ch)[:, t - (CONV_WIDTH - 1):]
    (zg,) = _matmul_call(h, wts["w_zg"], [(BF16, 1.0)], "proj_zg")
    (dtr,) = _matmul_call(h, wts["w_dt"], [(F32, 1.0)], "proj_dt")
    if cache_k is None:
        o_sb = _sb_prompt_call(q, kb, vb)
    else:
        o_sb = _sb_sample_call(q, kb, vb, cache_k, cache_v, tq=t, heads=q.shape[1] // SB_HEAD_DIM)
    npairs = ssm_prev.shape[1] // 2
    h0 = ssm_prev.reshape(batch, npairs, LANES, SSM_STATE)
    o_ssm, h_fin = _ssd_call(xbc, zg, dtr, cprev, h0, wts["conv_w"], wts["conv_b"], wts["dt_bias"], wts["a_log"],
                             wts["d_skip"], wts["ssm_norm_w"], batch=batch, lc=min(t, 128))
    x1 = _merge_call(o_sb, o_ssm, zg, x, mod, wts["ln_attn_g"], wts["ln_attn_b"],
                     wts["w_branch_sb"], wts["w_branch_ssm"], wts["w_out"], alpha=alpha, batch=batch, row0=row0)
    y = _ffn_call(x1, mod, wts["ln_ffn_g"], wts["ln_ffn_b"], wts["w_up"], wts["w_down"],
                  alpha=alpha, batch=batch, row0=row0)
    return y, k32, v32, conv_new, h_fin.reshape(ssm_prev.shape)


def _prep_weights(l, w_in, conv_w, conv_b, dt_bias, a_log, d_skip, ssm_norm_w, w_branch_sb, w_branch_ssm,
                  w_out, ln_attn_g, ln_attn_b, w_up, w_down, ln_ffn_g, ln_ffn_b, sbw):
    d = w_in.shape[1]
    inner = ssm_norm_w.shape[1]
    conv_ch = conv_w.shape[2]
    nh = a_log.shape[1]
    o_z = 3 * sbw
    o_xbc = o_z + inner
    o_dt = o_xbc + conv_ch
    o_g = o_dt + nh
    wit = jnp.transpose(w_in[l])
    pad_h = lambda v: jnp.pad(v, (0, LANES - nh))[None, :]
    return {
        "w_in_t": wit, "q_cols": (0, sbw), "k_cols": (sbw, sbw), "v_cols": (2 * sbw, sbw),
        "xbc_cols": (o_xbc, conv_ch),
        "w_zg": jnp.concatenate([wit[o_z:o_xbc], wit[o_g:]], axis=0),
        "w_dt": jnp.pad(wit[o_dt:o_g], ((0, LANES - nh), (0, 0))),
        "conv_w": conv_w[l], "conv_b": conv_b[l][None, :],
        "dt_bias": pad_h(dt_bias[l]), "a_log": pad_h(a_log[l]),
        "d_skip": jnp.repeat(d_skip[l], SSM_HEAD_DIM)[None, :],
        "ssm_norm_w": ssm_norm_w[l][None, :],
        "w_branch_sb": w_branch_sb[l].astype(BF16), "w_branch_ssm": w_branch_ssm[l].astype(BF16),
        "w_out": w_out[l].astype(BF16),
        "ln_attn_g": ln_attn_g[l][None, :], "ln_attn_b": ln_attn_b[l][None, :],
        "w_up": w_up[l].astype(BF16), "w_down": w_down[l].astype(BF16),
        "ln_ffn_g": ln_ffn_g[l][None, :], "ln_ffn_b": ln_ffn_b[l][None, :],
    }


def kernel(x_prompt, x_sample, cache_sb_k, cache_sb_v, state_conv, state_ssm, c_prompt, c_sample, w_ada, b_ada, w_in, conv_w, conv_b, dt_bias, a_log, d_skip, ssm_norm_w, w_branch_sb, w_branch_ssm, w_out, ln_attn_g, ln_attn_b, w_up, w_down, ln_ffn_g, ln_ffn_b):
    depth = w_ada.shape[0]
    alpha = (2 * depth) ** 0.25
    n_p, t_p, d = x_prompt.shape
    n_s, t_s, _ = x_sample.shape
    past = cache_sb_k.shape[2]
    sbw = cache_sb_k.shape[3] * cache_sb_k.shape[4]
    conv_ch = conv_w.shape[2]
    y_p = x_prompt.reshape(n_p * t_p, d)
    y_s = x_sample.reshape(n_s * t_s, d)
    assert n_p == 1
    row0_p = -(-n_s // SUBLANES) * SUBLANES
    c_all = jnp.concatenate([jnp.pad(c_sample, ((0, row0_p - n_s), (0, 0))),
                             jnp.pad(c_prompt, ((0, SUBLANES - n_p), (0, 0)))], axis=0)
    new_p, new_s = [], []
    for l in range(depth):
        wts = _prep_weights(l, w_in, conv_w, conv_b, dt_bias, a_log, d_skip, ssm_norm_w, w_branch_sb,
                            w_branch_ssm, w_out, ln_attn_g, ln_attn_b, w_up, w_down, ln_ffn_g, ln_ffn_b, sbw)
        mod = _mod_call(c_all, w_ada[l], b_ada[l][None, :])
        conv0 = jnp.zeros((n_p, CONV_WIDTH - 1, conv_ch), F32)
        ssm0 = jnp.zeros((n_p,) + state_ssm.shape[2:], F32)
        y_p, k_p, v_p, cv_p, s_p = _layer(y_p, mod, None, None, conv0, ssm0, wts,
                                          batch=n_p, row0=row0_p, alpha=alpha)
        y_s, k_s, v_s, cv_s, s_s = _layer(y_s, mod, jnp.transpose(cache_sb_k[l], (0, 2, 3, 1)),
                                          jnp.transpose(cache_sb_v[l], (0, 2, 3, 1)), state_conv[l], state_ssm[l],
                                          wts, batch=n_s, row0=0, alpha=alpha)
        hshape = cache_sb_k.shape[3:]
        new_p.append((k_p.reshape((n_p, t_p) + hshape), v_p.reshape((n_p, t_p) + hshape), cv_p, s_p))
        new_s.append((k_s.reshape((n_s, t_s) + hshape), v_s.reshape((n_s, t_s) + hshape), cv_s, s_s))
    stack = lambda lst, i: jnp.stack([e[i] for e in lst])
    return (y_p.reshape(n_p, t_p, d), y_s.reshape(n_s, t_s, d),
            stack(new_p, 0), stack(new_p, 1), stack(new_p, 2), stack(new_p, 3),
            stack(new_s, 0), stack(new_s, 1), stack(new_s, 2), stack(new_s, 3))
```

```python
import functools

import jax
import jax.numpy as jnp
from jax import lax
from jax.experimental import pallas as pl
from jax.experimental.pallas import tpu as pltpu

F32 = jnp.float32
BF16 = jnp.bfloat16

SB_HEAD_DIM = 64
SSM_HEAD_DIM = 64
SSM_STATE = 128
SSM_GROUPS = 8
CONV_WIDTH = 4
LN_EPS = 1e-5
LANES = 128
SUBLANES = 8
CONV_PAD_ROWS = 8
VMEM_LIMIT = 56 * 1024 * 1024
LOG2E = 1.4426950408889634
SB_DEAD_LOG2 = 150.0


def _cparams(sem):
    return pltpu.CompilerParams(dimension_semantics=sem, vmem_limit_bytes=VMEM_LIMIT)


def _dot(a, b):
    return jnp.dot(a, b, preferred_element_type=F32)


def _dot_nt(a, b):
    return lax.dot_general(a, b, (((1,), (1,)), ((), ())), preferred_element_type=F32)


def _split_bf16(x, n):
    parts, r = [], x
    for _ in range(n):
        p = r.astype(BF16)
        parts.append(p)
        r = r - p.astype(F32)
    return parts


def _dot_exact_rhs(a_bf16, b_f32, n=3):
    out = None
    for p in _split_bf16(b_f32, n):
        t = _dot(a_bf16, p)
        out = t if out is None else out + t
    return out


def _transpose_exact(eye_bf16, b_f32, n=3):
    out = None
    for p in _split_bf16(b_f32, n):
        t = _dot_nt(eye_bf16, p)
        out = t if out is None else out + t
    return out


def _softplus(x):
    return jnp.maximum(x, 0.0) + jnp.log(1.0 + jnp.exp(-jnp.abs(x)))


def _silu(x):
    h = 0.5 * x
    return h + h * jnp.tanh(h)


def _ln(x):
    mu = jnp.mean(x, axis=-1, keepdims=True)
    xc = x - mu
    var = jnp.mean(xc * xc, axis=-1, keepdims=True)
    return xc * lax.rsqrt(var + LN_EPS)


def _causal_conv_silu(xp, cw_ref, cb_ref):
    conv = cb_ref[...] + cw_ref[CONV_WIDTH - 1:CONV_WIDTH, :] * xp[CONV_PAD_ROWS:, :]
    for back in range(1, CONV_WIDTH):
        w = CONV_WIDTH - 1 - back
        conv = conv + cw_ref[w:w + 1, :] * pltpu.roll(xp, back, axis=0)[CONV_PAD_ROWS:, :]
    return _silu(conv)


def _eye(n, dtype):
    r = lax.broadcasted_iota(jnp.int32, (n, n), 0)
    c = lax.broadcasted_iota(jnp.int32, (n, n), 1)
    return jnp.where(r == c, 1.0, 0.0).astype(dtype)


MOD_PIECES = ("shift_a", "scale_a", "gate_a", "shift_f", "scale_f", "gate_f")


def _mod_spec(d, piece, batch, row0):
    col = MOD_PIECES.index(piece)
    if batch == 1:
        assert row0 % SUBLANES == 0
        return pl.BlockSpec((SUBLANES, d), lambda i: (row0 // SUBLANES, col))
    assert row0 == 0
    return pl.BlockSpec((batch, d), lambda i: (0, col))


def _mod_rows(ref, batch, m):
    if batch == 1:
        return ref[0:1, :]
    v = ref[...]
    return jnp.broadcast_to(v[:, None, :], (batch, m // batch, v.shape[1])).reshape(m, v.shape[1])


def _row_tile(m, batch, cap):
    tm = min(m, cap)
    assert batch == 1 or tm == m
    return tm


def _mod_kernel(c_ref, w_ref, b_ref, o_ref):
    s = _silu(c_ref[...])
    s_hi, s_lo = _split_bf16(s, 2)
    w_hi, w_lo = _split_bf16(w_ref[...], 2)
    o_ref[...] = _dot(s_hi, w_hi) + _dot(s_hi, w_lo) + _dot(s_lo, w_hi) + b_ref[...]


def _mod_call(c, w, b):
    r, d = c.shape
    n = w.shape[1]
    tn = 1024
    return pl.pallas_call(
        _mod_kernel,
        grid=(n // tn,),
        in_specs=[pl.BlockSpec((r, d), lambda j: (0, 0)),
                  pl.BlockSpec((d, tn), lambda j: (0, j)),
                  pl.BlockSpec((1, tn), lambda j: (0, j))],
        out_specs=pl.BlockSpec((r, tn), lambda j: (0, j)),
        out_shape=jax.ShapeDtypeStruct((r, n), F32),
        compiler_params=_cparams(("arbitrary",)),
        name="ada_mod",
    )(c, w, b)


def _ln_mod_kernel(x_ref, sc_ref, sh_ref, o_ref, *, batch):
    m = x_ref.shape[0]
    o_ref[...] = (_ln(x_ref[...]) * (1.0 + _mod_rows(sc_ref, batch, m))
                  + _mod_rows(sh_ref, batch, m)).astype(o_ref.dtype)


def _ln_mod_call(x, mod, *, batch, row0):
    m, d = x.shape
    tm = _row_tile(m, batch, 1024)
    return pl.pallas_call(
        functools.partial(_ln_mod_kernel, batch=batch),
        grid=(m // tm,),
        in_specs=[pl.BlockSpec((tm, d), lambda i: (i, 0)),
                  _mod_spec(d, "scale_a", batch, row0),
                  _mod_spec(d, "shift_a", batch, row0)],
        out_specs=pl.BlockSpec((tm, d), lambda i: (i, 0)),
        out_shape=jax.ShapeDtypeStruct((m, d), BF16),
        compiler_params=_cparams(("arbitrary",)),
        name="ln_mod",
    )(x, mod, mod)


def _matmul_kernel(x_ref, w_ref, *refs, scales, feature_major):
    o_refs = refs[:len(scales)]
    if w_ref.dtype == BF16:
        w = w_ref[...]
    else:
        wb_ref = refs[len(scales)]

        @pl.when(pl.program_id(1) == 0)
        def _():
            wb_ref[...] = w_ref[...].astype(BF16)

        w = wb_ref[...]
    acc = _dot_nt(w, x_ref[...]) if feature_major else _dot_nt(x_ref[...], w)
    for o_ref, s in zip(o_refs, scales):
        o_ref[...] = (acc if s == 1.0 else acc * s).astype(o_ref.dtype)


def _matmul_call(x, wt, outs, name, *, row0=0, nrows=None, feature_major=False):
    m, k = x.shape
    n = wt.shape[0] - row0 if nrows is None else nrows
    tm = min(m, 2048)
    tn = min(n, 1024)
    assert row0 % tn == 0 and n % tn == 0
    rb0 = row0 // tn
    if feature_major:
        out_spec, out_dims = pl.BlockSpec((tn, tm), lambda j, i: (j, i)), (n, m)
    else:
        out_spec, out_dims = pl.BlockSpec((tm, tn), lambda j, i: (i, j)), (m, n)
    res = pl.pallas_call(
        functools.partial(_matmul_kernel, scales=tuple(s for _, s in outs), feature_major=feature_major),
        grid=(n // tn, m // tm),
        in_specs=[pl.BlockSpec((tm, k), lambda j, i: (i, 0)),
                  pl.BlockSpec((tn, k), lambda j, i: (rb0 + j, 0))],
        out_specs=[out_spec for _ in outs],
        out_shape=[jax.ShapeDtypeStruct(out_dims, dt) for dt, _ in outs],
        scratch_shapes=[] if wt.dtype == BF16 else [pltpu.VMEM((tn, k), BF16)],
        compiler_params=_cparams(("arbitrary", "arbitrary")),
        name=name,
    )(x, wt)
    return res


def _suffix_matrix(tk):
    r = lax.broadcasted_iota(jnp.int32, (tk, tk), 0)
    c = lax.broadcasted_iota(jnp.int32, (tk, tk), 1)
    return jnp.where(r >= c, 1.0, 0.0).astype(BF16)


def _stacked_row_ids(n, tq, tk):
    one = lax.broadcasted_iota(jnp.int32, (tq, tk), 0)
    return jnp.concatenate([one] * n, axis=0)


def _sb_weights(z, c_run, suffix, visible):
    neg_abs = lax.bitcast_convert_type(lax.bitcast_convert_type(z, jnp.int32) | jnp.int32(-2 ** 31), F32)
    sp = jnp.maximum(z, 0.0) + jnp.log(1.0 + jnp.exp2(neg_abs)) * LOG2E
    if visible is not None:
        sp = jnp.where(visible, sp, 0.0)
    s = _dot(sp.astype(BF16), suffix) + c_run
    w = jnp.exp2(z - s)
    if visible is not None:
        w = jnp.where(visible, w, 0.0)
    return w.astype(BF16), s[:, 0:1]


def _sb_live(c_run):
    return (jnp.min(c_run) <= SB_DEAD_LOG2).astype(jnp.int32)


def _sb_prompt_kernel(q_ref, k_ref, v_ref, o_ref, *, tq, npair, nq):
    i = pl.program_id(1)
    low = lax.broadcasted_iota(jnp.int32, (tq, LANES), 1) < SB_HEAD_DIM
    suffix = _suffix_matrix(tq)
    rows = _stacked_row_ids(2, tq, tq)
    cols = lax.broadcasted_iota(jnp.int32, (2 * tq, tq), 1)
    units = [(s, slice(s * tq, (s + 1) * tq), slice(p * LANES, (p + 1) * LANES))
             for s in range(nq) for p in range(npair)]
    q_ab = []
    for _, rsl, lsl in units:
        qf = q_ref[rsl, lsl].astype(F32)
        q_ab.append(jnp.concatenate([jnp.where(low, qf, 0.0), jnp.where(low, 0.0, qf)], axis=0).astype(BF16))

    def block(key_block, carry, visible):
        out = []
        for u, ((s, _, lsl), (c_run, acc)) in enumerate(zip(units, carry)):
            at = pl.multiple_of(key_block(s) * tq, tq)
            w, c_run = _sb_weights(_dot(q_ab[u], k_ref[lsl, pl.ds(at, tq)]), c_run, suffix, visible(s))
            pv = _dot_nt(w, v_ref[lsl, pl.ds(at, tq)])
            out.append((c_run, acc + jnp.where(low, pv[:tq], pv[tq:])))
        return out

    def live(carry):
        return _sb_live(functools.reduce(jnp.minimum, [c for c, _ in carry]))

    def all_or_none(flag):
        return rows >= jnp.where(flag, 0, tq)

    qb = lambda s: i * nq + s
    carry = [(jnp.zeros((2 * tq, 1), F32), jnp.zeros((tq, LANES), F32)) for _ in units]
    carry = block(qb, carry, lambda s: cols < rows)
    carry = block(lambda s: jnp.maximum(qb(s) - 1, 0), carry, lambda s: all_or_none(qb(s) > 0))

    def cond(st):
        return jnp.logical_and(qb(nq - 1) - 2 - st[0] >= 0, st[1] > 0)

    def body(st):
        n, _, carry = st
        kb = lambda s: qb(s) - 2 - n
        carry = block(lambda s: jnp.maximum(kb(s), 0), carry,
                      (lambda s: None) if nq == 1 else (lambda s: all_or_none(kb(s) >= 0)))
        return n + 1, live(carry), carry

    _, _, carry = lax.while_loop(cond, body, (0, live(carry), carry))
    for (_, rsl, lsl), (_, acc) in zip(units, carry):
        o_ref[rsl, lsl] = acc.astype(o_ref.dtype)


def _sb_prompt_call(q, kt, vt):
    t, w = q.shape
    tq = min(t, 256)
    npair = 2 if w % (2 * LANES) == 0 else 1
    nq = 2 if t % (2 * tq) == 0 else 1
    wb = npair * LANES
    return pl.pallas_call(
        functools.partial(_sb_prompt_kernel, tq=tq, npair=npair, nq=nq),
        grid=(w // wb, t // (nq * tq)),
        in_specs=[pl.BlockSpec((nq * tq, wb), lambda p, i: (i, p)),
                  pl.BlockSpec((wb, t), lambda p, i: (p, 0)),
                  pl.BlockSpec((wb, t), lambda p, i: (p, 0))],
        out_specs=pl.BlockSpec((nq * tq, wb), lambda p, i: (i, p)),
        out_shape=jax.ShapeDtypeStruct((t, w), BF16),
        compiler_params=_cparams(("arbitrary", "arbitrary")),
        name="sb_prompt",
    )(q, kt, vt)


def _sb_sample_kernel(q_ref, kn_ref, vn_ref, kc_hbm, vc_hbm, o_ref, kbuf, vbuf, sem, *, heads, tq, tk, nblk):
    b = pl.program_id(0)
    hd = SB_HEAD_DIM

    def copies(j, slot):
        at = pl.multiple_of(j * tk, tk)
        return (pltpu.make_async_copy(kc_hbm.at[b, :, :, pl.ds(at, tk)], kbuf.at[slot], sem.at[0, slot]),
                pltpu.make_async_copy(vc_hbm.at[b, :, :, pl.ds(at, tk)], vbuf.at[slot], sem.at[1, slot]))

    def start(j, slot):
        for cp in copies(j, slot):
            cp.start()

    def wait(j, slot):
        for cp in copies(j, slot):
            cp.wait()

    slot_of = lambda j: lax.rem(nblk - 1 - j, 2)

    def stacked_block(qk_of, pv_of, suffix, visible, c_run, acc):
        q = q_ref[...]
        z = jnp.concatenate([qk_of(q[:, h * hd:(h + 1) * hd], h) for h in range(heads)], axis=0)
        w, c_run = _sb_weights(z, c_run, suffix, visible)
        acc = acc + jnp.concatenate([pv_of(w[h * tq:(h + 1) * tq], h) for h in range(heads)], axis=0)
        return c_run, acc

    start(nblk - 1, 0)
    rows = _stacked_row_ids(heads, tq, tq)
    cols = lax.broadcasted_iota(jnp.int32, (heads * tq, tq), 1)
    c_run, acc = stacked_block(lambda qh, h: _dot_nt(qh, kn_ref[:, h * hd:(h + 1) * hd]),
                               lambda wh, h: _dot(wh, vn_ref[:, h * hd:(h + 1) * hd]),
                               _suffix_matrix(tq), cols < rows,
                               jnp.zeros((heads * tq, 1), F32), jnp.zeros((heads * tq, hd), F32))
    suffix = _suffix_matrix(tk)

    def cond(st):
        return jnp.logical_and(st[0] >= 0, st[1] > 0)

    def body(st):
        j, _, c_run, acc = st
        slot = slot_of(j)
        wait(j, slot)

        @pl.when(j > 0)
        def _():
            start(j - 1, 1 - slot)

        c_run, acc = stacked_block(lambda qh, h: _dot(qh, kbuf[slot, h].astype(BF16)),
                                   lambda wh, h: _dot_nt(wh, vbuf[slot, h].astype(BF16)),
                                   suffix, None, c_run, acc)
        return j - 1, _sb_live(c_run), c_run, acc

    j_end, _, _, acc = lax.while_loop(cond, body, (nblk - 1, _sb_live(c_run), c_run, acc))

    @pl.when(j_end >= 0)
    def _():
        wait(j_end, slot_of(j_end))

    o_ref[...] = jnp.concatenate([acc[h * tq:(h + 1) * tq] for h in range(heads)], axis=1).astype(o_ref.dtype)


def _sb_sample_call(q, kn, vn, kc, vc, *, tq, heads):
    m, w = q.shape
    b, _, _, past = kc.shape
    tk = min(past, 256)
    new_spec = pl.BlockSpec((tq, w), lambda bi: (bi, 0))
    hbm_spec = pl.BlockSpec(memory_space=pl.ANY)
    return pl.pallas_call(
        functools.partial(_sb_sample_kernel, heads=heads, tq=tq, tk=tk, nblk=past // tk),
        grid=(b,),
        in_specs=[new_spec, new_spec, new_spec, hbm_spec, hbm_spec],
        out_specs=new_spec,
        out_shape=jax.ShapeDtypeStruct((m, w), BF16),
        scratch_shapes=[pltpu.VMEM((2, heads, SB_HEAD_DIM, tk), F32),
                        pltpu.VMEM((2, heads, SB_HEAD_DIM, tk), F32),
                        pltpu.SemaphoreType.DMA((2, 2))],
        compiler_params=_cparams(("arbitrary",)),
        name="sb_sample",
    )(q, kn, vn, kc, vc)


def _ssd_kernel(xbc_ref, z_ref, dt_ref, cprev_ref, h0_ref, cw_ref, cb_ref, dtb_ref, alog_ref, dsk_ref,
                nw_ref, o_ref, hout_ref, xpad_ref, st_ref, act_ref, y_ref, *, lc, cps, inner, npairs):
    ci = pl.program_id(1)
    eye = _eye(LANES, BF16)
    rows_step = cps * lc

    @pl.when(ci == 0)
    def _():
        xpad_ref[0:CONV_PAD_ROWS, :] = cprev_ref[0]
        for p in range(npairs):
            st_ref[p] = h0_ref[0, p].T

    @pl.when(ci > 0)
    def _():
        xpad_ref[0:CONV_PAD_ROWS, :] = xpad_ref[rows_step:rows_step + CONV_PAD_ROWS, :]

    xpad_ref[CONV_PAD_ROWS:CONV_PAD_ROWS + rows_step, :] = xbc_ref[...]
    act_ref[...] = _causal_conv_silu(xpad_ref[...], cw_ref, cb_ref)

    dt_all = _softplus(dt_ref[...] + dtb_ref[...])
    a_neg = -jnp.exp(alog_ref[...])
    r = lax.broadcasted_iota(jnp.int32, (lc, lc), 0)
    c = lax.broadcasted_iota(jnp.int32, (lc, lc), 1)
    causal = r >= c
    prefix = jnp.where(causal, 1.0, 0.0).astype(BF16)
    acausal_off = jnp.where(causal, 0.0, -1e30)
    low = lax.broadcasted_iota(jnp.int32, (lc, LANES), 1) < SSM_HEAD_DIM
    low1 = lax.broadcasted_iota(jnp.int32, (1, LANES), 1) < SSM_HEAD_DIM

    def pair_cols(arr, ha):
        return jnp.where(low, arr[:, ha:ha + 1], arr[:, ha + 1:ha + 2])

    pairs_per_group = npairs // SSM_GROUPS
    for cc in range(cps):
        rs = slice(cc * lc, (cc + 1) * lc)
        dtv = dt_all[rs]
        a_cs = _dot_exact_rhs(prefix, dtv * a_neg)
        a_cs_t = _transpose_exact(eye, a_cs)
        total = a_cs[lc - 1:lc, :]
        ea = jnp.exp(a_cs)
        to_end = jnp.exp(total - a_cs)
        chunk_decay = jnp.exp(total)
        for g in range(SSM_GROUPS):
            b_bf = act_ref[rs, inner + g * SSM_STATE:inner + (g + 1) * SSM_STATE].astype(BF16)
            c_off = inner + SSM_GROUPS * SSM_STATE
            c_bf = act_ref[rs, c_off + g * SSM_STATE:c_off + (g + 1) * SSM_STATE].astype(BF16)
            cb = _dot_nt(c_bf, b_bf)
            b_t = _dot_nt(eye, b_bf).astype(BF16)
            for kk in range(pairs_per_group):
                p = g * pairs_per_group + kk
                ha = 2 * p
                sl = slice(p * LANES, (p + 1) * LANES)
                x = act_ref[rs, sl]
                xdt = x * pair_cols(dtv, ha)
                xdt_bf = xdt.astype(BF16)
                ys = []
                for h in (ha, ha + 1):
                    seg = a_cs[:, h:h + 1] - a_cs_t[h:h + 1, :]
                    decay = jnp.exp(seg + acausal_off)
                    ys.append(_dot((cb * decay).astype(BF16), xdt_bf))
                st = st_ref[p]
                y_off = _dot(c_bf, st.astype(BF16)) * pair_cols(ea, ha)
                y_ref[rs, sl] = jnp.where(low, ys[0], ys[1]) + y_off + dsk_ref[:, sl] * x
                xw = (xdt * pair_cols(to_end, ha)).astype(BF16)
                dec = jnp.where(low1, chunk_decay[:, ha:ha + 1], chunk_decay[:, ha + 1:ha + 2])
                st_ref[p] = st * dec + _dot(b_t, xw)

    gy = y_ref[...] * _silu(z_ref[...].astype(F32))
    gw = inner // SSM_GROUPS
    for g in range(SSM_GROUPS):
        seg = gy[:, g * gw:(g + 1) * gw]
        ms = jnp.mean(seg * seg, axis=-1, keepdims=True)
        o_ref[:, g * gw:(g + 1) * gw] = (seg * lax.rsqrt(ms + LN_EPS) * nw_ref[:, g * gw:(g + 1) * gw]).astype(o_ref.dtype)

    @pl.when(ci == pl.num_programs(1) - 1)
    def _():
        for p in range(npairs):
            hout_ref[0, p] = st_ref[p].T


def _ssd_call(xbc, zg, dtr, conv_prev, h0, cw, cb, dtb, alog, dsk, nw, *, batch, lc):
    m = xbc.shape[0]
    conv_ch = cw.shape[1]
    inner = nw.shape[1]
    npairs = h0.shape[1]
    chunks = m // batch // lc
    cps = 4 if chunks % 4 == 0 else 1
    rs = cps * lc
    nc = chunks // cps
    row = lambda b, c: b * nc + c
    const = lambda b, c: (0, 0)
    return pl.pallas_call(
        functools.partial(_ssd_kernel, lc=lc, cps=cps, inner=inner, npairs=npairs),
        grid=(batch, nc),
        in_specs=[pl.BlockSpec((rs, conv_ch), lambda b, c: (row(b, c), 0)),
                  pl.BlockSpec((rs, inner), lambda b, c: (row(b, c), 0)),
                  pl.BlockSpec((rs, LANES), lambda b, c: (row(b, c), 0)),
                  pl.BlockSpec((1, CONV_PAD_ROWS, conv_ch), lambda b, c: (b, 0, 0)),
                  pl.BlockSpec((1, npairs, LANES, LANES), lambda b, c: (b, 0, 0, 0)),
                  pl.BlockSpec((CONV_WIDTH, conv_ch), const),
                  pl.BlockSpec((1, conv_ch), const),
                  pl.BlockSpec((1, LANES), const),
                  pl.BlockSpec((1, LANES), const),
                  pl.BlockSpec((1, inner), const),
                  pl.BlockSpec((1, inner), const)],
        out_specs=[pl.BlockSpec((rs, inner), lambda b, c: (row(b, c), 0)),
                   pl.BlockSpec((1, npairs, LANES, LANES), lambda b, c: (b, 0, 0, 0))],
        out_shape=[jax.ShapeDtypeStruct((m, inner), BF16),
                   jax.ShapeDtypeStruct(h0.shape, F32)],
        scratch_shapes=[pltpu.VMEM((CONV_PAD_ROWS + rs, conv_ch), F32),
                        pltpu.VMEM((npairs, LANES, LANES), F32),
                        pltpu.VMEM((rs, conv_ch), F32),
                        pltpu.VMEM((rs, inner), F32)],
        compiler_params=_cparams(("arbitrary", "arbitrary")),
        name="ssd",
    )(xbc, zg, dtr, conv_prev, h0, cw, cb, dtb, alog, dsk, nw)


def _merge_kernel(osb_ref, ossm_ref, gsb_ref, gssm_ref, x_ref, ga_ref, lg_ref, lb_ref,
                  wsb_ref, wssm_ref, wout_ref, o_ref, *, alpha, batch):
    merged = (jax.nn.sigmoid(gsb_ref[...].astype(F32)) * _dot(osb_ref[...], wsb_ref[...])
              + jax.nn.sigmoid(gssm_ref[...].astype(F32)) * _dot(ossm_ref[...], wssm_ref[...]))
    gate = _mod_rows(ga_ref, batch, x_ref.shape[0])
    res = alpha * x_ref[...] + gate * _dot(merged.astype(BF16), wout_ref[...])
    o_ref[...] = _ln(res) * lg_ref[...] + lb_ref[...]


def _resident(shape):
    return pl.BlockSpec(shape, lambda i: (0, 0), pipeline_mode=pl.Buffered(1))


def _merge_call(osb, ossm, zg, x, mod, ln_g, ln_b, wsb, wssm, wout, *, alpha, batch, row0):
    m, d = x.shape
    sbw = osb.shape[1]
    inner = ossm.shape[1]
    tm = _row_tile(m, batch, 512)
    gcol = inner // d
    return pl.pallas_call(
        functools.partial(_merge_kernel, alpha=alpha, batch=batch),
        grid=(m // tm,),
        in_specs=[pl.BlockSpec((tm, sbw), lambda i: (i, 0)),
                  pl.BlockSpec((tm, inner), lambda i: (i, 0)),
                  pl.BlockSpec((tm, d), lambda i: (i, gcol)),
                  pl.BlockSpec((tm, d), lambda i: (i, gcol + 1)),
                  pl.BlockSpec((tm, d), lambda i: (i, 0)),
                  _mod_spec(d, "gate_a", batch, row0),
                  pl.BlockSpec((1, d), lambda i: (0, 0)),
                  pl.BlockSpec((1, d), lambda i: (0, 0)),
                  _resident(wsb.shape), _resident(wssm.shape), _resident(wout.shape)],
        out_specs=pl.BlockSpec((tm, d), lambda i: (i, 0)),
        out_shape=jax.ShapeDtypeStruct((m, d), F32),
        compiler_params=_cparams(("arbitrary",)),
        name="merge",
    )(osb, ossm, zg, zg, x, mod, ln_g, ln_b, wsb, wssm, wout)


def _ffn_kernel(x1_ref, sc_ref, sh_ref, gf_ref, lg_ref, lb_ref, wup_ref, wdn_ref, o_ref, *, alpha, nchunk, batch):
    x1 = x1_ref[...]
    m = x1.shape[0]
    h2 = (_ln(x1) * (1.0 + _mod_rows(sc_ref, batch, m)) + _mod_rows(sh_ref, batch, m)).astype(BF16)
    dff = wup_ref.shape[1]
    cw = dff // nchunk
    f = None
    for ck in range(nchunk):
        a = jnp.maximum(_dot(h2, wup_ref[:, ck * cw:(ck + 1) * cw]), 0.0)
        t = _dot((a * a).astype(BF16), wdn_ref[ck * cw:(ck + 1) * cw, :])
        f = t if f is None else f + t
    o_ref[...] = _ln(alpha * x1 + _mod_rows(gf_ref, batch, m) * f) * lg_ref[...] + lb_ref[...]


def _ffn_call(x1, mod, ln_g, ln_b, wup, wdn, *, alpha, batch, row0):
    m, d = x1.shape
    tm = _row_tile(m, batch, 1024)
    return pl.pallas_call(
        functools.partial(_ffn_kernel, alpha=alpha, nchunk=4, batch=batch),
        grid=(m // tm,),
        in_specs=[pl.BlockSpec((tm, d), lambda i: (i, 0)),
                  _mod_spec(d, "scale_f", batch, row0),
                  _mod_spec(d, "shift_f", batch, row0),
                  _mod_spec(d, "gate_f", batch, row0),
                  pl.BlockSpec((1, d), lambda i: (0, 0)),
                  pl.BlockSpec((1, d), lambda i: (0, 0)),
                  _resident(wup.shape), _resident(wdn.shape)],
        out_specs=pl.BlockSpec((tm, d), lambda i: (i, 0)),
        out_shape=jax.ShapeDtypeStruct((m, d), F32),
        compiler_params=_cparams(("arbitrary",)),
        name="ffn",
    )(x1, mod, mod, mod, ln_g, ln_b, wup, wdn)


def _layer(x, mod, cache_k, cache_v, conv_prev, ssm_prev, wts, *, batch, row0, alpha):
    m, d = x.shape
    t = m // batch
    h = _ln_mod_call(x, mod, batch=batch, row0=row0)
    def proj(cols, outs, name, **kw):
        return _matmul_call(h, wts["w_in_t"], outs, name, row0=wts[cols][0], nrows=wts[cols][1], **kw)

    (q,) = proj("q_cols", [(BF16, SB_HEAD_DIM ** -0.5 * LOG2E)], "proj_q")
    kv_t = cache_k is None
    k32, kb = proj("k_cols", [(F32, 1.0), (BF16, 1.0)], "proj_k", feature_major=kv_t)
    v32, vb = proj("v_cols", [(F32, 1.0), (BF16, 1.0)], "proj_v", feature_major=kv_t)
    if kv_t:
        k32, v32 = k32.T, v32.T
    conv_ch = wts["conv_w"].shape[1]
    cprev = jnp.pad(conv_prev, ((0, 0), (CONV_PAD_ROWS - (CONV_WIDTH - 1), 0), (0, 0)))
    (xbc,) = proj("xbc_cols", [(F32, 1.0)], "proj_xbc")
    conv_new = xbc.reshape(batch, t, conv_ch)[:, t - (CONV_WIDTH - 1):]
    (zg,) = _matmul_call(h, wts["w_zg"], [(BF16, 1.0)], "proj_zg")
    (dtr,) = _matmul_call(h, wts["w_dt"], [(F32, 1.0)], "proj_dt")
    if cache_k is None:
        o_sb = _sb_prompt_call(q, kb, vb)
    else:
        o_sb = _sb_sample_call(q, kb, vb, cache_k, cache_v, tq=t, heads=q.shape[1] // SB_HEAD_DIM)
    npairs = ssm_prev.shape[1] // 2
    h0 = ssm_prev.reshape(batch, npairs, LANES, SSM_STATE)
    o_ssm, h_fin = _ssd_call(xbc, zg, dtr, cprev, h0, wts["conv_w"], wts["conv_b"], wts["dt_bias"], wts["a_log"],
                             wts["d_skip"], wts["ssm_norm_w"], batch=batch, lc=min(t, 128))
    x1 = _merge_call(o_sb, o_ssm, zg, x, mod, wts["ln_attn_g"], wts["ln_attn_b"],
                     wts["w_branch_sb"], wts["w_branch_ssm"], wts["w_out"], alpha=alpha, batch=batch, row0=row0)
    y = _ffn_call(x1, mod, wts["ln_ffn_g"], wts["ln_ffn_b"], wts["w_up"], wts["w_down"],
                  alpha=alpha, batch=batch, row0=row0)
    return y, k32, v32, conv_new, h_fin.reshape(ssm_prev.shape)


def _prep_weights(l, w_in, conv_w, conv_b, dt_bias, a_log, d_skip, ssm_norm_w, w_branch_sb, w_branch_ssm,
                  w_out, ln_attn_g, ln_attn_b, w_up, w_down, ln_ffn_g, ln_ffn_b, sbw):
    d = w_in.shape[1]
    inner = ssm_norm_w.shape[1]
    conv_ch = conv_w.shape[2]
    nh = a_log.shape[1]
    o_z = 3 * sbw
    o_xbc = o_z + inner
    o_dt = o_xbc + conv_ch
    o_g = o_dt + nh
    wit = jnp.transpose(w_in[l])
    pad_h = lambda v: jnp.pad(v, (0, LANES - nh))[None, :]
    return {
        "w_in_t": wit, "q_cols": (0, sbw), "k_cols": (sbw, sbw), "v_cols": (2 * sbw, sbw),
        "xbc_cols": (o_xbc, conv_ch),
        "w_zg": jnp.concatenate([wit[o_z:o_xbc], wit[o_g:]], axis=0),
        "w_dt": jnp.pad(wit[o_dt:o_g], ((0, LANES - nh), (0, 0))),
        "conv_w": conv_w[l], "conv_b": conv_b[l][None, :],
        "dt_bias": pad_h(dt_bias[l]), "a_log": pad_h(a_log[l]),
        "d_skip": jnp.repeat(d_skip[l], SSM_HEAD_DIM)[None, :],
        "ssm_norm_w": ssm_norm_w[l][None, :],
        "w_branch_sb": w_branch_sb[l].astype(BF16), "w_branch_ssm": w_branch_ssm[l].astype(BF16),
        "w_out": w_out[l].astype(BF16),
        "ln_attn_g": ln_attn_g[l][None, :], "ln_attn_b": ln_attn_b[l][None, :],
        "w_up": w_up[l].astype(BF16), "w_down": w_down[l].astype(BF16),
        "ln_ffn_g": ln_ffn_g[l][None, :], "ln_ffn_b": ln_ffn_b[l][None, :],
    }


def kernel(x_prompt, x_sample, cache_sb_k, cache_sb_v, state_conv, state_ssm, c_prompt, c_sample, w_ada, b_ada, w_in, conv_w, conv_b, dt_bias, a_log, d_skip, ssm_norm_w, w_branch_sb, w_branch_ssm, w_out, ln_attn_g, ln_attn_b, w_up, w_down, ln_ffn_g, ln_ffn_b):
    depth = w_ada.shape[0]
    alpha = (2 * depth) ** 0.25
    n_p, t_p, d = x_prompt.shape
    n_s, t_s, _ = x_sample.shape
    past = cache_sb_k.shape[2]
    sbw = cache_sb_k.shape[3] * cache_sb_k.shape[4]
    conv_ch = conv_w.shape[2]
    y_p = x_prompt.reshape(n_p * t_p, d)
    y_s = x_sample.reshape(n_s * t_s, d)
    assert n_p == 1
    row0_p = -(-n_s // SUBLANES) * SUBLANES
    c_all = jnp.concatenate([jnp.pad(c_sample, ((0, row0_p - n_s), (0, 0))),
                             jnp.pad(c_prompt, ((0, SUBLANES - n_p), (0, 0)))], axis=0)
    new_p, new_s = [], []
    for l in range(depth):
        wts = _prep_weights(l, w_in, conv_w, conv_b, dt_bias, a_log, d_skip, ssm_norm_w, w_branch_sb,
                            w_branch_ssm, w_out, ln_attn_g, ln_attn_b, w_up, w_down, ln_ffn_g, ln_ffn_b, sbw)
        mod = _mod_call(c_all, w_ada[l], b_ada[l][None, :])
        conv0 = jnp.zeros((n_p, CONV_WIDTH - 1, conv_ch), F32)
        ssm0 = jnp.zeros((n_p,) + state_ssm.shape[2:], F32)
        y_p, k_p, v_p, cv_p, s_p = _layer(y_p, mod, None, None, conv0, ssm0, wts,
                                          batch=n_p, row0=row0_p, alpha=alpha)
        y_s, k_s, v_s, cv_s, s_s = _layer(y_s, mod, jnp.transpose(cache_sb_k[l], (0, 2, 3, 1)),
                                          jnp.transpose(cache_sb_v[l], (0, 2, 3, 1)), state_conv[l], state_ssm[l],
                                          wts, batch=n_s, row0=0, alpha=alpha)
        hshape = cache_sb_k.shape[3:]
        new_p.append((k_p.reshape((n_p, t_p) + hshape), v_p.reshape((n_p, t_p) + hshape), cv_p, s_p))
        new_s.append((k_s.reshape((n_s, t_s) + hshape), v_s.reshape((n_s, t_s) + hshape), cv_s, s_s))
    stack = lambda lst, i: jnp.stack([e[i] for e in lst])
    return (y_p.reshape(n_p, t_p, d), y_s.reshape(n_s, t_s, d),
            stack(new_p, 0), stack(new_p, 1), stack(new_p, 2), stack(new_p, 3),
            stack(new_s, 0), stack(new_s, 1), stack(new_s, 2), stack(new_s, 3))
```

```python
import functools

import jax
import jax.numpy as jnp
from jax import lax
from jax.experimental import pallas as pl
from jax.experimental.pallas import tpu as pltpu

F32 = jnp.float32
BF16 = jnp.bfloat16

SB_HEAD_DIM = 64
SSM_HEAD_DIM = 64
SSM_STATE = 128
SSM_GROUPS = 8
CONV_WIDTH = 4
LN_EPS = 1e-5
LANES = 128
SUBLANES = 8
CONV_PAD_ROWS = 8
VMEM_LIMIT = 56 * 1024 * 1024
LOG2E = 1.4426950408889634
SB_DEAD_LOG2 = 150.0


def _cparams(sem):
    return pltpu.CompilerParams(dimension_semantics=sem, vmem_limit_bytes=VMEM_LIMIT)


def _dot(a, b):
    return jnp.dot(a, b, preferred_element_type=F32)


def _dot_nt(a, b):
    return lax.dot_general(a, b, (((1,), (1,)), ((), ())), preferred_element_type=F32)


def _split_bf16(x, n):
    parts, r = [], x
    for _ in range(n):
        p = r.astype(BF16)
        parts.append(p)
        r = r - p.astype(F32)
    return parts


def _dot_exact_rhs(a_bf16, b_f32, n=3):
    out = None
    for p in _split_bf16(b_f32, n):
        t = _dot(a_bf16, p)
        out = t if out is None else out + t
    return out


def _transpose_exact(eye_bf16, b_f32, n=3):
    out = None
    for p in _split_bf16(b_f32, n):
        t = _dot_nt(eye_bf16, p)
        out = t if out is None else out + t
    return out


def _softplus(x):
    return jnp.maximum(x, 0.0) + jnp.log(1.0 + jnp.exp(-jnp.abs(x)))


def _silu(x):
    h = 0.5 * x
    return h + h * jnp.tanh(h)


def _ln(x):
    mu = jnp.mean(x, axis=-1, keepdims=True)
    xc = x - mu
    var = jnp.mean(xc * xc, axis=-1, keepdims=True)
    return xc * lax.rsqrt(var + LN_EPS)


def _causal_conv_silu(xp, cw_ref, cb_ref):
    conv = cb_ref[...] + cw_ref[CONV_WIDTH - 1:CONV_WIDTH, :] * xp[CONV_PAD_ROWS:, :]
    for back in range(1, CONV_WIDTH):
        w = CONV_WIDTH - 1 - back
        conv = conv + cw_ref[w:w + 1, :] * pltpu.roll(xp, back, axis=0)[CONV_PAD_ROWS:, :]
    return _silu(conv)


def _eye(n, dtype):
    r = lax.broadcasted_iota(jnp.int32, (n, n), 0)
    c = lax.broadcasted_iota(jnp.int32, (n, n), 1)
    return jnp.where(r == c, 1.0, 0.0).astype(dtype)


MOD_PIECES = ("shift_a", "scale_a", "gate_a", "shift_f", "scale_f", "gate_f")


def _mod_spec(d, piece, batch, row0):
    col = MOD_PIECES.index(piece)
    if batch == 1:
        assert row0 % SUBLANES == 0
        return pl.BlockSpec((SUBLANES, d), lambda i: (row0 // SUBLANES, col))
    assert row0 == 0
    return pl.BlockSpec((batch, d), lambda i: (0, col))


def _mod_rows(ref, batch, m):
    if batch == 1:
        return ref[0:1, :]
    v = ref[...]
    return jnp.broadcast_to(v[:, None, :], (batch, m // batch, v.shape[1])).reshape(m, v.shape[1])


def _row_tile(m, batch, cap):
    tm = min(m, cap)
    assert batch == 1 or tm == m
    return tm


def _mod_kernel(c_ref, w_ref, b_ref, o_ref):
    s = _silu(c_ref[...])
    s_hi, s_lo = _split_bf16(s, 2)
    w_hi, w_lo = _split_bf16(w_ref[...], 2)
    o_ref[...] = _dot(s_hi, w_hi) + _dot(s_hi, w_lo) + _dot(s_lo, w_hi) + b_ref[...]


def _mod_call(c, w, b):
    r, d = c.shape
    n = w.shape[1]
    tn = 1024
    return pl.pallas_call(
        _mod_kernel,
        grid=(n // tn,),
        in_specs=[pl.BlockSpec((r, d), lambda j: (0, 0)),
                  pl.BlockSpec((d, tn), lambda j: (0, j)),
                  pl.BlockSpec((1, tn), lambda j: (0, j))],
        out_specs=pl.BlockSpec((r, tn), lambda j: (0, j)),
        out_shape=jax.ShapeDtypeStruct((r, n), F32),
        compiler_params=_cparams(("arbitrary",)),
        name="ada_mod",
    )(c, w, b)


def _ln_mod_kernel(x_ref, sc_ref, sh_ref, o_ref, *, batch):
    m = x_ref.shape[0]
    o_ref[...] = (_ln(x_ref[...]) * (1.0 + _mod_rows(sc_ref, batch, m))
                  + _mod_rows(sh_ref, batch, m)).astype(o_ref.dtype)


def _ln_mod_call(x, mod, *, batch, row0):
    m, d = x.shape
    tm = _row_tile(m, batch, 1024)
    return pl.pallas_call(
        functools.partial(_ln_mod_kernel, batch=batch),
        grid=(m // tm,),
        in_specs=[pl.BlockSpec((tm, d), lambda i: (i, 0)),
                  _mod_spec(d, "scale_a", batch, row0),
                  _mod_spec(d, "shift_a", batch, row0)],
        out_specs=pl.BlockSpec((tm, d), lambda i: (i, 0)),
        out_shape=jax.ShapeDtypeStruct((m, d), BF16),
        compiler_params=_cparams(("arbitrary",)),
        name="ln_mod",
    )(x, mod, mod)


def _matmul_kernel(x_ref, w_ref, *refs, scales, feature_major):
    o_refs = refs[:len(scales)]
    if w_ref.dtype == BF16:
        w = w_ref[...]
    else:
        wb_ref = refs[len(scales)]

        @pl.when(pl.program_id(1) == 0)
        def _():
            wb_ref[...] = w_ref[...].astype(BF16)

        w = wb_ref[...]
    acc = _dot_nt(w, x_ref[...]) if feature_major else _dot_nt(x_ref[...], w)
    for o_ref, s in zip(o_refs, scales):
        o_ref[...] = (acc if s == 1.0 else acc * s).astype(o_ref.dtype)


def _matmul_call(x, wt, outs, name, *, row0=0, nrows=None, feature_major=False):
    m, k = x.shape
    n = wt.shape[0] - row0 if nrows is None else nrows
    tm = min(m, 2048)
    tn = min(n, 1024)
    assert row0 % tn == 0 and n % tn == 0
    rb0 = row0 // tn
    if feature_major:
        out_spec, out_dims = pl.BlockSpec((tn, tm), lambda j, i: (j, i)), (n, m)
    else:
        out_spec, out_dims = pl.BlockSpec((tm, tn), lambda j, i: (i, j)), (m, n)
    res = pl.pallas_call(
        functools.partial(_matmul_kernel, scales=tuple(s for _, s in outs), feature_major=feature_major),
        grid=(n // tn, m // tm),
        in_specs=[pl.BlockSpec((tm, k), lambda j, i: (i, 0)),
                  pl.BlockSpec((tn, k), lambda j, i: (rb0 + j, 0))],
        out_specs=[out_spec for _ in outs],
        out_shape=[jax.ShapeDtypeStruct(out_dims, dt) for dt, _ in outs],
        scratch_shapes=[] if wt.dtype == BF16 else [pltpu.VMEM((tn, k), BF16)],
        compiler_params=_cparams(("arbitrary", "arbitrary")),
        name=name,
    )(x, wt)
    return res


def _suffix_matrix(tk):
    r = lax.broadcasted_iota(jnp.int32, (tk, tk), 0)
    c = lax.broadcasted_iota(jnp.int32, (tk, tk), 1)
    return jnp.where(r >= c, 1.0, 0.0).astype(BF16)


def _stacked_row_ids(n, tq, tk):
    one = lax.broadcasted_iota(jnp.int32, (tq, tk), 0)
    return jnp.concatenate([one] * n, axis=0)


def _sb_weights(z, c_run, suffix, visible):
    neg_abs = lax.bitcast_convert_type(lax.bitcast_convert_type(z, jnp.int32) | jnp.int32(-2 ** 31), F32)
    sp = jnp.maximum(z, 0.0) + jnp.log(1.0 + jnp.exp2(neg_abs)) * LOG2E
    if visible is not None:
        sp = jnp.where(visible, sp, 0.0)
    s = _dot(sp.astype(BF16), suffix) + c_run
    w = jnp.exp2(z - s)
    if visible is not None:
        w = jnp.where(visible, w, 0.0)
    return w.astype(BF16), s[:, 0:1]


def _sb_live(c_run):
    return (jnp.min(c_run) <= SB_DEAD_LOG2).astype(jnp.int32)


def _sb_prompt_kernel(q_ref, k_ref, v_ref, o_ref, *, tq, npair, nq):
    i = pl.program_id(1)
    low = lax.broadcasted_iota(jnp.int32, (tq, LANES), 1) < SB_HEAD_DIM
    suffix = _suffix_matrix(tq)
    rows = _stacked_row_ids(2, tq, tq)
    cols = lax.broadcasted_iota(jnp.int32, (2 * tq, tq), 1)
    units = [(s, slice(s * tq, (s + 1) * tq), slice(p * LANES, (p + 1) * LANES))
             for s in range(nq) for p in range(npair)]
    q_ab = []
    for _, rsl, lsl in units:
        qf = q_ref[rsl, lsl].astype(F32)
        q_ab.append(jnp.concatenate([jnp.where(low, qf, 0.0), jnp.where(low, 0.0, qf)], axis=0).astype(BF16))

    def block(key_block, carry, visible):
        out = []
        for u, ((s, _, lsl), (c_run, acc)) in enumerate(zip(units, carry)):
            at = pl.multiple_of(key_block(s) * tq, tq)
            w, c_run = _sb_weights(_dot(q_ab[u], k_ref[lsl, pl.ds(at, tq)]), c_run, suffix, visible(s))
            pv = _dot_nt(w, v_ref[lsl, pl.ds(at, tq)])
            out.append((c_run, acc + jnp.where(low, pv[:tq], pv[tq:])))
        return out

    def live(carry):
        return _sb_live(functools.reduce(jnp.minimum, [c for c, _ in carry]))

    def all_or_none(flag):
        return rows >= jnp.where(flag, 0, tq)

    qb = lambda s: i * nq + s
    carry = [(jnp.zeros((2 * tq, 1), F32), jnp.zeros((tq, LANES), F32)) for _ in units]
    carry = block(qb, carry, lambda s: cols < rows)
    carry = block(lambda s: jnp.maximum(qb(s) - 1, 0), carry, lambda s: all_or_none(qb(s) > 0))

    def cond(st):
        return jnp.logical_and(qb(nq - 1) - 2 - st[0] >= 0, st[1] > 0)

    def body(st):
        n, _, carry = st
        kb = lambda s: qb(s) - 2 - n
        carry = block(lambda s: jnp.maximum(kb(s), 0), carry,
                      (lambda s: None) if nq == 1 else (lambda s: all_or_none(kb(s) >= 0)))
        return n + 1, live(carry), carry

    _, _, carry = lax.while_loop(cond, body, (0, live(carry), carry))
    for (_, rsl, lsl), (_, acc) in zip(units, carry):
        o_ref[rsl, lsl] = acc.astype(o_ref.dtype)


def _sb_prompt_call(q, kt, vt):
    t, w = q.shape
    tq = min(t, 256)
    npair = 2 if w % (2 * LANES) == 0 else 1
    nq = 2 if t % (2 * tq) == 0 else 1
    wb = npair * LANES
    return pl.pallas_call(
        functools.partial(_sb_prompt_kernel, tq=tq, npair=npair, nq=nq),
        grid=(w // wb, t // (nq * tq)),
        in_specs=[pl.BlockSpec((nq * tq, wb), lambda p, i: (i, p)),
                  pl.BlockSpec((wb, t), lambda p, i: (p, 0)),
                  pl.BlockSpec((wb, t), lambda p, i: (p, 0))],
        out_specs=pl.BlockSpec((nq * tq, wb), lambda p, i: (i, p)),
        out_shape=jax.ShapeDtypeStruct((t, w), BF16),
        compiler_params=_cparams(("arbitrary", "arbitrary")),
        name="sb_prompt",
    )(q, kt, vt)


def _sb_sample_kernel(q_ref, kn_ref, vn_ref, kc_hbm, vc_hbm, o_ref, kbuf, vbuf, sem, *, heads, tq, tk, nblk):
    b = pl.program_id(0)
    hd = SB_HEAD_DIM

    def copies(j, slot):
        at = pl.multiple_of(j * tk, tk)
        return (pltpu.make_async_copy(kc_hbm.at[b, :, :, pl.ds(at, tk)], kbuf.at[slot], sem.at[0, slot]),
                pltpu.make_async_copy(vc_hbm.at[b, :, :, pl.ds(at, tk)], vbuf.at[slot], sem.at[1, slot]))

    def start(j, slot):
        for cp in copies(j, slot):
            cp.start()

    def wait(j, slot):
        for cp in copies(j, slot):
            cp.wait()

    slot_of = lambda j: lax.rem(nblk - 1 - j, 2)

    def stacked_block(qk_of, pv_of, suffix, visible, c_run, acc):
        q = q_ref[...]
        z = jnp.concatenate([qk_of(q[:, h * hd:(h + 1) * hd], h) for h in range(heads)], axis=0)
        w, c_run = _sb_weights(z, c_run, suffix, visible)
        acc = acc + jnp.concatenate([pv_of(w[h * tq:(h + 1) * tq], h) for h in range(heads)], axis=0)
        return c_run, acc

    start(nblk - 1, 0)
    rows = _stacked_row_ids(heads, tq, tq)
    cols = lax.broadcasted_iota(jnp.int32, (heads * tq, tq), 1)
    c_run, acc = stacked_block(lambda qh, h: _dot_nt(qh, kn_ref[:, h * hd:(h + 1) * hd]),
                               lambda wh, h: _dot(wh, vn_ref[:, h * hd:(h + 1) * hd]),
                               _suffix_matrix(tq), cols < rows,
                               jnp.zeros((heads * tq, 1), F32), jnp.zeros((heads * tq, hd), F32))
    suffix = _suffix_matrix(tk)

    def cond(st):
        return jnp.logical_and(st[0] >= 0, st[1] > 0)

    def body(st):
        j, _, c_run, acc = st
        slot = slot_of(j)
        wait(j, slot)

        @pl.when(j > 0)
        def _():
            start(j - 1, 1 - slot)

        c_run, acc = stacked_block(lambda qh, h: _dot(qh, kbuf[slot, h].astype(BF16)),
                                   lambda wh, h: _dot_nt(wh, vbuf[slot, h].astype(BF16)),
                                   suffix, None, c_run, acc)
        return j - 1, _sb_live(c_run), c_run, acc

    j_end, _, _, acc = lax.while_loop(cond, body, (nblk - 1, _sb_live(c_run), c_run, acc))

    @pl.when(j_end >= 0)
    def _():
        wait(j_end, slot_of(j_end))

    o_ref[...] = jnp.concatenate([acc[h * tq:(h + 1) * tq] for h in range(heads)], axis=1).astype(o_ref.dtype)


def _sb_sample_call(q, kn, vn, kc, vc, *, tq, heads):
    m, w = q.shape
    b, _, _, past = kc.shape
    tk = min(past, 256)
    new_spec = pl.BlockSpec((tq, w), lambda bi: (bi, 0))
    hbm_spec = pl.BlockSpec(memory_space=pl.ANY)
    return pl.pallas_call(
        functools.partial(_sb_sample_kernel, heads=heads, tq=tq, tk=tk, nblk=past // tk),
        grid=(b,),
        in_specs=[new_spec, new_spec, new_spec, hbm_spec, hbm_spec],
        out_specs=new_spec,
        out_shape=jax.ShapeDtypeStruct((m, w), BF16),
        scratch_shapes=[pltpu.VMEM((2, heads, SB_HEAD_DIM, tk), F32),
                        pltpu.VMEM((2, heads, SB_HEAD_DIM, tk), F32),
                        pltpu.SemaphoreType.DMA((2, 2))],
        compiler_params=_cparams(("arbitrary",)),
        name="sb_sample",
    )(q, kn, vn, kc, vc)


def _ssd_kernel(xbc_ref, z_ref, dt_ref, cprev_ref, h0_ref, cw_ref, cb_ref, dtb_ref, alog_ref, dsk_ref,
                nw_ref, o_ref, hout_ref, xpad_ref, st_ref, act_ref, y_ref, *, lc, cps, inner, npairs):
    ci = pl.program_id(1)
    eye = _eye(LANES, BF16)
    rows_step = cps * lc

    @pl.when(ci == 0)
    def _():
        xpad_ref[0:CONV_PAD_ROWS, :] = cprev_ref[0]
        for p in range(npairs):
            st_ref[p] = h0_ref[0, p].T

    @pl.when(ci > 0)
    def _():
        xpad_ref[0:CONV_PAD_ROWS, :] = xpad_ref[rows_step:rows_step + CONV_PAD_ROWS, :]

    xpad_ref[CONV_PAD_ROWS:CONV_PAD_ROWS + rows_step, :] = xbc_ref[...]
    act_ref[...] = _causal_conv_silu(xpad_ref[...], cw_ref, cb_ref)

    dt_all = _softplus(dt_ref[...] + dtb_ref[...])
    a_neg = -jnp.exp(alog_ref[...])
    r = lax.broadcasted_iota(jnp.int32, (lc, lc), 0)
    c = lax.broadcasted_iota(jnp.int32, (lc, lc), 1)
    causal = r >= c
    prefix = jnp.where(causal, 1.0, 0.0).astype(BF16)
    acausal_off = jnp.where(causal, 0.0, -1e30)
    low = lax.broadcasted_iota(jnp.int32, (lc, LANES), 1) < SSM_HEAD_DIM
    low1 = lax.broadcasted_iota(jnp.int32, (1, LANES), 1) < SSM_HEAD_DIM

    def pair_cols(arr, ha):
        return jnp.where(low, arr[:, ha:ha + 1], arr[:, ha + 1:ha + 2])

    pairs_per_group = npairs // SSM_GROUPS
    for cc in range(cps):
        rs = slice(cc * lc, (cc + 1) * lc)
        dtv = dt_all[rs]
        a_cs = _dot_exact_rhs(prefix, dtv * a_neg)
        a_cs_t = _transpose_exact(eye, a_cs)
        total = a_cs[lc - 1:lc, :]
        ea = jnp.exp(a_cs)
        to_end = jnp.exp(total - a_cs)
        chunk_decay = jnp.exp(total)
        for g in range(SSM_GROUPS):
            b_bf = act_ref[rs, inner + g * SSM_STATE:inner + (g + 1) * SSM_STATE].astype(BF16)
            c_off = inner + SSM_GROUPS * SSM_STATE
            c_bf = act_ref[rs, c_off + g * SSM_STATE:c_off + (g + 1) * SSM_STATE].astype(BF16)
            cb = _dot_nt(c_bf, b_bf)
            b_t = _dot_nt(eye, b_bf).astype(BF16)
            for kk in range(pairs_per_group):
                p = g * pairs_per_group + kk
                ha = 2 * p
                sl = slice(p * LANES, (p + 1) * LANES)
                x = act_ref[rs, sl]
                xdt = x * pair_cols(dtv, ha)
                xdt_bf = xdt.astype(BF16)
                ys = []
                for h in (ha, ha + 1):
                    seg = a_cs[:, h:h + 1] - a_cs_t[h:h + 1, :]
                    decay = jnp.exp(seg + acausal_off)
                    ys.append(_dot((cb * decay).astype(BF16), xdt_bf))
                st = st_ref[p]
                y_off = _dot(c_bf, st.astype(BF16)) * pair_cols(ea, ha)
                y_ref[rs, sl] = jnp.where(low, ys[0], ys[1]) + y_off + dsk_ref[:, sl] * x
                xw = (xdt * pair_cols(to_end, ha)).astype(BF16)
                dec = jnp.where(low1, chunk_decay[:, ha:ha + 1], chunk_decay[:, ha + 1:ha + 2])
                st_ref[p] = st * dec + _dot(b_t, xw)

    gy = y_ref[...] * _silu(z_ref[...].astype(F32))
    gw = inner // SSM_GROUPS
    for g in range(SSM_GROUPS):
        seg = gy[:, g * gw:(g + 1) * gw]
        ms = jnp.mean(seg * seg, axis=-1, keepdims=True)
        o_ref[:, g * gw:(g + 1) * gw] = (seg * lax.rsqrt(ms + LN_EPS) * nw_ref[:, g * gw:(g + 1) * gw]).astype(o_ref.dtype)

    @pl.when(ci == pl.num_programs(1) - 1)
    def _():
        for p in range(npairs):
            hout_ref[0, p] = st_ref[p].T


def _ssd_call(xbc, zg, dtr, conv_prev, h0, cw, cb, dtb, alog, dsk, nw, *, batch, lc):
    m = xbc.shape[0]
    conv_ch = cw.shape[1]
    inner = nw.shape[1]
    npairs = h0.shape[1]
    chunks = m // batch // lc
    cps = 4 if chunks % 4 == 0 else 1
    rs = cps * lc
    nc = chunks // cps
    row = lambda b, c: b * nc + c
    const = lambda b, c: (0, 0)
    return pl.pallas_call(
        functools.partial(_ssd_kernel, lc=lc, cps=cps, inner=inner, npairs=npairs),
        grid=(batch, nc),
        in_specs=[pl.BlockSpec((rs, conv_ch), lambda b, c: (row(b, c), 0)),
                  pl.BlockSpec((rs, inner), lambda b, c: (row(b, c), 0)),
                  pl.BlockSpec((rs, LANES), lambda b, c: (row(b, c), 0)),
                  pl.BlockSpec((1, CONV_PAD_ROWS, conv_ch), lambda b, c: (b, 0, 0)),
                  pl.BlockSpec((1, npairs, LANES, LANES), lambda b, c: (b, 0, 0, 0)),
                  pl.BlockSpec((CONV_WIDTH, conv_ch), const),
                  pl.BlockSpec((1, conv_ch), const),
                  pl.BlockSpec((1, LANES), const),
                  pl.BlockSpec((1, LANES), const),
                  pl.BlockSpec((1, inner), const),
                  pl.BlockSpec((1, inner), const)],
        out_specs=[pl.BlockSpec((rs, inner), lambda b, c: (row(b, c), 0)),
                   pl.BlockSpec((1, npairs, LANES, LANES), lambda b, c: (b, 0, 0, 0))],
        out_shape=[jax.ShapeDtypeStruct((m, inner), BF16),
                   jax.ShapeDtypeStruct(h0.shape, F32)],
        scratch_shapes=[pltpu.VMEM((CONV_PAD_ROWS + rs, conv_ch), F32),
                        pltpu.VMEM((npairs, LANES, LANES), F32),
                        pltpu.VMEM((rs, conv_ch), F32),
                        pltpu.VMEM((rs, inner), F32)],
        compiler_params=_cparams(("arbitrary", "arbitrary")),
        name="ssd",
    )(xbc, zg, dtr, conv_prev, h0, cw, cb, dtb, alog, dsk, nw)


def _merge_kernel(osb_ref, ossm_ref, gsb_ref, gssm_ref, x_ref, ga_ref, lg_ref, lb_ref,
                  wsb_ref, wssm_ref, wout_ref, o_ref, *, alpha, batch):
    merged = (jax.nn.sigmoid(gsb_ref[...].astype(F32)) * _dot(osb_ref[...], wsb_ref[...])
              + jax.nn.sigmoid(gssm_ref[...].astype(F32)) * _dot(ossm_ref[...], wssm_ref[...]))
    gate = _mod_rows(ga_ref, batch, x_ref.shape[0])
    res = alpha * x_ref[...] + gate * _dot(merged.astype(BF16), wout_ref[...])
    o_ref[...] = _ln(res) * lg_ref[...] + lb_ref[...]


def _resident(shape):
    return pl.BlockSpec(shape, lambda i: (0, 0), pipeline_mode=pl.Buffered(1))


def _merge_call(osb, ossm, zg, x, mod, ln_g, ln_b, wsb, wssm, wout, *, alpha, batch, row0):
    m, d = x.shape
    sbw = osb.shape[1]
    inner = ossm.shape[1]
    tm = _row_tile(m, batch, 1024)
    gcol = inner // d
    return pl.pallas_call(
        functools.partial(_merge_kernel, alpha=alpha, batch=batch),
        grid=(m // tm,),
        in_specs=[pl.BlockSpec((tm, sbw), lambda i: (i, 0)),
                  pl.BlockSpec((tm, inner), lambda i: (i, 0)),
                  pl.BlockSpec((tm, d), lambda i: (i, gcol)),
                  pl.BlockSpec((tm, d), lambda i: (i, gcol + 1)),
                  pl.BlockSpec((tm, d), lambda i: (i, 0)),
                  _mod_spec(d, "gate_a", batch, row0),
                  pl.BlockSpec((1, d), lambda i: (0, 0)),
                  pl.BlockSpec((1, d), lambda i: (0, 0)),
                  _resident(wsb.shape), _resident(wssm.shape), _resident(wout.shape)],
        out_specs=pl.BlockSpec((tm, d), lambda i: (i, 0)),
        out_shape=jax.ShapeDtypeStruct((m, d), F32),
        compiler_params=_cparams(("arbitrary",)),
        name="merge",
    )(osb, ossm, zg, zg, x, mod, ln_g, ln_b, wsb, wssm, wout)


def _ffn_kernel(x1_ref, sc_ref, sh_ref, gf_ref, lg_ref, lb_ref, wup_ref, wdn_ref, o_ref, *, alpha, nchunk, batch):
    x1 = x1_ref[...]
    m = x1.shape[0]
    h2 = (_ln(x1) * (1.0 + _mod_rows(sc_ref, batch, m)) + _mod_rows(sh_ref, batch, m)).astype(BF16)
    dff = wup_ref.shape[1]
    cw = dff // nchunk
    f = None
    for ck in range(nchunk):
        a = jnp.maximum(_dot(h2, wup_ref[:, ck * cw:(ck + 1) * cw]), 0.0)
        t = _dot((a * a).astype(BF16), wdn_ref[ck * cw:(ck + 1) * cw, :])
        f = t if f is None else f + t
    o_ref[...] = _ln(alpha * x1 + _mod_rows(gf_ref, batch, m) * f) * lg_ref[...] + lb_ref[...]


def _ffn_call(x1, mod, ln_g, ln_b, wup, wdn, *, alpha, batch, row0):
    m, d = x1.shape
    tm = _row_tile(m, batch, 1024)
    return pl.pallas_call(
        functools.partial(_ffn_kernel, alpha=alpha, nchunk=4, batch=batch),
        grid=(m // tm,),
        in_specs=[pl.BlockSpec((tm, d), lambda i: (i, 0)),
                  _mod_spec(d, "scale_f", batch, row0),
                  _mod_spec(d, "shift_f", batch, row0),
                  _mod_spec(d, "gate_f", batch, row0),
                  pl.BlockSpec((1, d), lambda i: (0, 0)),
                  pl.BlockSpec((1, d), lambda i: (0, 0)),
                  _resident(wup.shape), _resident(wdn.shape)],
        out_specs=pl.BlockSpec((tm, d), lambda i: (i, 0)),
        out_shape=jax.ShapeDtypeStruct((m, d), F32),
        compiler_params=_cparams(("arbitrary",)),
        name="ffn",
    )(x1, mod, mod, mod, ln_g, ln_b, wup, wdn)


def _layer(x, mod, cache_k, cache_v, conv_prev, ssm_prev, wts, *, batch, row0, alpha):
    m, d = x.shape
    t = m // batch
    h = _ln_mod_call(x, mod, batch=batch, row0=row0)
    def proj(cols, outs, name, **kw):
        return _matmul_call(h, wts["w_in_t"], outs, name, row0=wts[cols][0], nrows=wts[cols][1], **kw)

    (q,) = proj("q_cols", [(BF16, SB_HEAD_DIM ** -0.5 * LOG2E)], "proj_q")
    kv_t = cache_k is None
    k32, kb = proj("k_cols", [(F32, 1.0), (BF16, 1.0)], "proj_k", feature_major=kv_t)
    v32, vb = proj("v_cols", [(F32, 1.0), (BF16, 1.0)], "proj_v", feature_major=kv_t)
    if kv_t:
        k32, v32 = k32.T, v32.T
    conv_ch = wts["conv_w"].shape[1]
    cprev = jnp.pad(conv_prev, ((0, 0), (CONV_PAD_ROWS - (CONV_WIDTH - 1), 0), (0, 0)))
    (xbc,) = proj("xbc_cols", [(F32, 1.0)], "proj_xbc")
    conv_new = xbc.reshape(batch, t, conv_ch)[:, t - (CONV_WIDTH - 1):]
    (zg,) = _matmul_call(h, wts["w_zg"], [(BF16, 1.0)], "proj_zg")
    (dtr,) = _matmul_call(h, wts["w_dt"], [(F32, 1.0)], "proj_dt")
    if cache_k is None:
        o_sb = _sb_prompt_call(q, kb, vb)
    else:
        o_sb = _sb_sample_call(q, kb, vb, cache_k, cache_v, tq=t, heads=q.shape[1] // SB_HEAD_DIM)
    npairs = ssm_prev.shape[1] // 2
    h0 = ssm_prev.reshape(batch, npairs, LANES, SSM_STATE)
    o_ssm, h_fin = _ssd_call(xbc, zg, dtr, cprev, h0, wts["conv_w"], wts["conv_b"], wts["dt_bias"], wts["a_log"],
                             wts["d_skip"], wts["ssm_norm_w"], batch=batch, lc=min(t, 128))
    x1 = _merge_call(o_sb, o_ssm, zg, x, mod, wts["ln_attn_g"], wts["ln_attn_b"],
                     wts["w_branch_sb"], wts["w_branch_ssm"], wts["w_out"], alpha=alpha, batch=batch, row0=row0)
    y = _ffn_call(x1, mod, wts["ln_ffn_g"], wts["ln_ffn_b"], wts["w_up"], wts["w_down"],
                  alpha=alpha, batch=batch, row0=row0)
    return y, k32, v32, conv_new, h_fin.reshape(ssm_prev.shape)


def _prep_weights(l, w_in, conv_w, conv_b, dt_bias, a_log, d_skip, ssm_norm_w, w_branch_sb, w_branch_ssm,
                  w_out, ln_attn_g, ln_attn_b, w_up, w_down, ln_ffn_g, ln_ffn_b, sbw):
    d = w_in.shape[1]
    inner = ssm_norm_w.shape[1]
    conv_ch = conv_w.shape[2]
    nh = a_log.shape[1]
    o_z = 3 * sbw
    o_xbc = o_z + inner
    o_dt = o_xbc + conv_ch
    o_g = o_dt + nh
    wit = jnp.transpose(w_in[l])
    pad_h = lambda v: jnp.pad(v, (0, LANES - nh))[None, :]
    return {
        "w_in_t": wit, "q_cols": (0, sbw), "k_cols": (sbw, sbw), "v_cols": (2 * sbw, sbw),
        "xbc_cols": (o_xbc, conv_ch),
        "w_zg": jnp.concatenate([wit[o_z:o_xbc], wit[o_g:]], axis=0),
        "w_dt": jnp.pad(wit[o_dt:o_g], ((0, LANES - nh), (0, 0))),
        "conv_w": conv_w[l], "conv_b": conv_b[l][None, :],
        "dt_bias": pad_h(dt_bias[l]), "a_log": pad_h(a_log[l]),
        "d_skip": jnp.repeat(d_skip[l], SSM_HEAD_DIM)[None, :],
        "ssm_norm_w": ssm_norm_w[l][None, :],
        "w_branch_sb": w_branch_sb[l].astype(BF16), "w_branch_ssm": w_branch_ssm[l].astype(BF16),
        "w_out": w_out[l].astype(BF16),
        "ln_attn_g": ln_attn_g[l][None, :], "ln_attn_b": ln_attn_b[l][None, :],
        "w_up": w_up[l].astype(BF16), "w_down": w_down[l].astype(BF16),
        "ln_ffn_g": ln_ffn_g[l][None, :], "ln_ffn_b": ln_ffn_b[l][None, :],
    }


def kernel(x_prompt, x_sample, cache_sb_k, cache_sb_v, state_conv, state_ssm, c_prompt, c_sample, w_ada, b_ada, w_in, conv_w, conv_b, dt_bias, a_log, d_skip, ssm_norm_w, w_branch_sb, w_branch_ssm, w_out, ln_attn_g, ln_attn_b, w_up, w_down, ln_ffn_g, ln_ffn_b):
    depth = w_ada.shape[0]
    alpha = (2 * depth) ** 0.25
    n_p, t_p, d = x_prompt.shape
    n_s, t_s, _ = x_sample.shape
    past = cache_sb_k.shape[2]
    sbw = cache_sb_k.shape[3] * cache_sb_k.shape[4]
    conv_ch = conv_w.shape[2]
    y_p = x_prompt.reshape(n_p * t_p, d)
    y_s = x_sample.reshape(n_s * t_s, d)
    assert n_p == 1
    row0_p = -(-n_s // SUBLANES) * SUBLANES
    c_all = jnp.concatenate([jnp.pad(c_sample, ((0, row0_p - n_s), (0, 0))),
                             jnp.pad(c_prompt, ((0, SUBLANES - n_p), (0, 0)))], axis=0)
    new_p, new_s = [], []
    for l in range(depth):
        wts = _prep_weights(l, w_in, conv_w, conv_b, dt_bias, a_log, d_skip, ssm_norm_w, w_branch_sb,
                            w_branch_ssm, w_out, ln_attn_g, ln_attn_b, w_up, w_down, ln_ffn_g, ln_ffn_b, sbw)
        mod = _mod_call(c_all, w_ada[l], b_ada[l][None, :])
        conv0 = jnp.zeros((n_p, CONV_WIDTH - 1, conv_ch), F32)
        ssm0 = jnp.zeros((n_p,) + state_ssm.shape[2:], F32)
        y_p, k_p, v_p, cv_p, s_p = _layer(y_p, mod, None, None, conv0, ssm0, wts,
                                          batch=n_p, row0=row0_p, alpha=alpha)
        y_s, k_s, v_s, cv_s, s_s = _layer(y_s, mod, jnp.transpose(cache_sb_k[l], (0, 2, 3, 1)),
                                          jnp.transpose(cache_sb_v[l], (0, 2, 3, 1)), state_conv[l], state_ssm[l],
                                          wts, batch=n_s, row0=0, alpha=alpha)
        hshape = cache_sb_k.shape[3:]
        new_p.append((k_p.reshape((n_p, t_p) + hshape), v_p.reshape((n_p, t_p) + hshape), cv_p, s_p))
        new_s.append((k_s.reshape((n_s, t_s) + hshape), v_s.reshape((n_s, t_s) + hshape), cv_s, s_s))
    stack = lambda lst, i: jnp.stack([e[i] for e in lst])
    return (y_p.reshape(n_p, t_p, d), y_s.reshape(n_s, t_s, d),
            stack(new_p, 0), stack(new_p, 1), stack(new_p, 2), stack(new_p, 3),
            stack(new_s, 0), stack(new_s, 1), stack(new_s, 2), stack(new_s, 3))
```

```python
import functools

import jax
import jax.numpy as jnp
from jax import lax
from jax.experimental import pallas as pl
from jax.experimental.pallas import tpu as pltpu

F32 = jnp.float32
BF16 = jnp.bfloat16

SB_HEAD_DIM = 64
SSM_HEAD_DIM = 64
SSM_STATE = 128
SSM_GROUPS = 8
CONV_WIDTH = 4
LN_EPS = 1e-5
LANES = 128
SUBLANES = 8
CONV_PAD_ROWS = 8
VMEM_LIMIT = 56 * 1024 * 1024
LOG2E = 1.4426950408889634
SB_DEAD_LOG2 = 150.0
SB_PEEL = 3


def _cparams(sem):
    return pltpu.CompilerParams(dimension_semantics=sem, vmem_limit_bytes=VMEM_LIMIT)


def _dot(a, b):
    return jnp.dot(a, b, preferred_element_type=F32)


def _dot_nt(a, b):
    return lax.dot_general(a, b, (((1,), (1,)), ((), ())), preferred_element_type=F32)


def _split_bf16(x, n):
    parts, r = [], x
    for _ in range(n):
        p = r.astype(BF16)
        parts.append(p)
        r = r - p.astype(F32)
    return parts


def _dot_exact_rhs(a_bf16, b_f32, n=3):
    out = None
    for p in _split_bf16(b_f32, n):
        t = _dot(a_bf16, p)
        out = t if out is None else out + t
    return out


def _transpose_exact(eye_bf16, b_f32, n=3):
    out = None
    for p in _split_bf16(b_f32, n):
        t = _dot_nt(eye_bf16, p)
        out = t if out is None else out + t
    return out


def _softplus(x):
    return jnp.maximum(x, 0.0) + jnp.log(1.0 + jnp.exp(-jnp.abs(x)))


def _silu(x):
    h = 0.5 * x
    return h + h * jnp.tanh(h)


def _ln(x):
    mu = jnp.mean(x, axis=-1, keepdims=True)
    xc = x - mu
    var = jnp.mean(xc * xc, axis=-1, keepdims=True)
    return xc * lax.rsqrt(var + LN_EPS)


def _causal_conv_silu(xp, cw_ref, cb_ref):
    conv = cb_ref[...] + cw_ref[CONV_WIDTH - 1:CONV_WIDTH, :] * xp[CONV_PAD_ROWS:, :]
    for back in range(1, CONV_WIDTH):
        w = CONV_WIDTH - 1 - back
        conv = conv + cw_ref[w:w + 1, :] * pltpu.roll(xp, back, axis=0)[CONV_PAD_ROWS:, :]
    return _silu(conv)


def _eye(n, dtype):
    r = lax.broadcasted_iota(jnp.int32, (n, n), 0)
    c = lax.broadcasted_iota(jnp.int32, (n, n), 1)
    return jnp.where(r == c, 1.0, 0.0).astype(dtype)


MOD_PIECES = ("shift_a", "scale_a", "gate_a", "shift_f", "scale_f", "gate_f")


def _mod_spec(d, piece, batch, row0):
    col = MOD_PIECES.index(piece)
    if batch == 1:
        assert row0 % SUBLANES == 0
        return pl.BlockSpec((SUBLANES, d), lambda i: (row0 // SUBLANES, col))
    assert row0 == 0
    return pl.BlockSpec((batch, d), lambda i: (0, col))


def _mod_rows(ref, batch, m):
    if batch == 1:
        return ref[0:1, :]
    v = ref[...]
    return jnp.broadcast_to(v[:, None, :], (batch, m // batch, v.shape[1])).reshape(m, v.shape[1])


def _row_tile(m, batch, cap):
    tm = min(m, cap)
    assert batch == 1 or tm == m
    return tm


def _mod_kernel(c_ref, w_ref, b_ref, o_ref):
    s = _silu(c_ref[...])
    s_hi, s_lo = _split_bf16(s, 2)
    w_hi, w_lo = _split_bf16(w_ref[...], 2)
    o_ref[...] = _dot(s_hi, w_hi) + _dot(s_hi, w_lo) + _dot(s_lo, w_hi) + b_ref[...]


def _mod_call(c, w, b):
    r, d = c.shape
    n = w.shape[1]
    tn = 1024
    return pl.pallas_call(
        _mod_kernel,
        grid=(n // tn,),
        in_specs=[pl.BlockSpec((r, d), lambda j: (0, 0)),
                  pl.BlockSpec((d, tn), lambda j: (0, j)),
                  pl.BlockSpec((1, tn), lambda j: (0, j))],
        out_specs=pl.BlockSpec((r, tn), lambda j: (0, j)),
        out_shape=jax.ShapeDtypeStruct((r, n), F32),
        compiler_params=_cparams(("arbitrary",)),
        name="ada_mod",
    )(c, w, b)


def _ln_mod_kernel(x_ref, sc_ref, sh_ref, o_ref, *, batch):
    m = x_ref.shape[0]
    o_ref[...] = (_ln(x_ref[...]) * (1.0 + _mod_rows(sc_ref, batch, m))
                  + _mod_rows(sh_ref, batch, m)).astype(o_ref.dtype)


def _ln_mod_call(x, mod, *, batch, row0):
    m, d = x.shape
    tm = _row_tile(m, batch, 1024)
    return pl.pallas_call(
        functools.partial(_ln_mod_kernel, batch=batch),
        grid=(m // tm,),
        in_specs=[pl.BlockSpec((tm, d), lambda i: (i, 0)),
                  _mod_spec(d, "scale_a", batch, row0),
                  _mod_spec(d, "shift_a", batch, row0)],
        out_specs=pl.BlockSpec((tm, d), lambda i: (i, 0)),
        out_shape=jax.ShapeDtypeStruct((m, d), BF16),
        compiler_params=_cparams(("arbitrary",)),
        name="ln_mod",
    )(x, mod, mod)


def _matmul_kernel(x_ref, w_ref, *refs, scales, feature_major):
    o_refs = refs[:len(scales)]
    if w_ref.dtype == BF16:
        w = w_ref[...]
    else:
        wb_ref = refs[len(scales)]

        @pl.when(pl.program_id(1) == 0)
        def _():
            wb_ref[...] = w_ref[...].astype(BF16)

        w = wb_ref[...]
    acc = _dot_nt(w, x_ref[...]) if feature_major else _dot_nt(x_ref[...], w)
    for o_ref, s in zip(o_refs, scales):
        o_ref[...] = (acc if s == 1.0 else acc * s).astype(o_ref.dtype)


def _matmul_call(x, wt, outs, name, *, row0=0, nrows=None, feature_major=False):
    m, k = x.shape
    n = wt.shape[0] - row0 if nrows is None else nrows
    tm = min(m, 2048)
    tn = min(n, 1024)
    assert row0 % tn == 0 and n % tn == 0
    rb0 = row0 // tn
    if feature_major:
        out_spec, out_dims = pl.BlockSpec((tn, tm), lambda j, i: (j, i)), (n, m)
    else:
        out_spec, out_dims = pl.BlockSpec((tm, tn), lambda j, i: (i, j)), (m, n)
    res = pl.pallas_call(
        functools.partial(_matmul_kernel, scales=tuple(s for _, s in outs), feature_major=feature_major),
        grid=(n // tn, m // tm),
        in_specs=[pl.BlockSpec((tm, k), lambda j, i: (i, 0)),
                  pl.BlockSpec((tn, k), lambda j, i: (rb0 + j, 0))],
        out_specs=[out_spec for _ in outs],
        out_shape=[jax.ShapeDtypeStruct(out_dims, dt) for dt, _ in outs],
        scratch_shapes=[] if wt.dtype == BF16 else [pltpu.VMEM((tn, k), BF16)],
        compiler_params=_cparams(("arbitrary", "arbitrary")),
        name=name,
    )(x, wt)
    return res


def _suffix_matrix(tk):
    r = lax.broadcasted_iota(jnp.int32, (tk, tk), 0)
    c = lax.broadcasted_iota(jnp.int32, (tk, tk), 1)
    return jnp.where(r >= c, 1.0, 0.0).astype(BF16)


def _stacked_row_ids(n, tq, tk):
    one = lax.broadcasted_iota(jnp.int32, (tq, tk), 0)
    return jnp.concatenate([one] * n, axis=0)


def _sb_weights(z, c_run, suffix, visible):
    neg_abs = lax.bitcast_convert_type(lax.bitcast_convert_type(z, jnp.int32) | jnp.int32(-2 ** 31), F32)
    sp = jnp.maximum(z, 0.0) + jnp.log(1.0 + jnp.exp2(neg_abs)) * LOG2E
    if visible is not None:
        sp = jnp.where(visible, sp, 0.0)
    s = _dot(sp.astype(BF16), suffix) + c_run
    w = jnp.exp2(z - s)
    if visible is not None:
        w = jnp.where(visible, w, 0.0)
    return w.astype(BF16), s[:, 0:1]


def _sb_live(c_run):
    return (jnp.min(c_run) <= SB_DEAD_LOG2).astype(jnp.int32)


def _sb_prompt_kernel(q_ref, k_ref, v_ref, o_ref, *, tq, npair, nq):
    i = pl.program_id(1)
    low = lax.broadcasted_iota(jnp.int32, (tq, LANES), 1) < SB_HEAD_DIM
    suffix = _suffix_matrix(tq)
    rows = _stacked_row_ids(2, tq, tq)
    cols = lax.broadcasted_iota(jnp.int32, (2 * tq, tq), 1)
    units = [(s, slice(s * tq, (s + 1) * tq), slice(p * LANES, (p + 1) * LANES))
             for s in range(nq) for p in range(npair)]
    q_ab = []
    for _, rsl, lsl in units:
        qf = q_ref[rsl, lsl].astype(F32)
        q_ab.append(jnp.concatenate([jnp.where(low, qf, 0.0), jnp.where(low, 0.0, qf)], axis=0).astype(BF16))

    def block(key_block, carry, visible):
        out = []
        for u, ((s, _, lsl), (c_run, acc)) in enumerate(zip(units, carry)):
            at = pl.multiple_of(key_block(s) * tq, tq)
            w, c_run = _sb_weights(_dot(q_ab[u], k_ref[lsl, pl.ds(at, tq)]), c_run, suffix, visible(s))
            pv = _dot_nt(w, v_ref[lsl, pl.ds(at, tq)])
            out.append((c_run, acc + jnp.where(low, pv[:tq], pv[tq:])))
        return out

    def live(carry):
        return _sb_live(functools.reduce(jnp.minimum, [c for c, _ in carry]))

    def all_or_none(flag):
        return rows >= jnp.where(flag, 0, tq)

    qb = lambda s: i * nq + s
    carry = [(jnp.zeros((2 * tq, 1), F32), jnp.zeros((tq, LANES), F32)) for _ in units]
    carry = block(qb, carry, lambda s: cols < rows)
    for back in range(1, SB_PEEL):
        carry = block(lambda s, back=back: jnp.maximum(qb(s) - back, 0), carry,
                      lambda s, back=back: all_or_none(qb(s) >= back))

    def cond(st):
        return jnp.logical_and(qb(nq - 1) - SB_PEEL - st[0] >= 0, st[1] > 0)

    def body(st):
        n, _, carry = st
        kb = lambda s: qb(s) - SB_PEEL - n
        carry = block(lambda s: jnp.maximum(kb(s), 0), carry,
                      (lambda s: None) if nq == 1 else (lambda s: all_or_none(kb(s) >= 0)))
        return n + 1, live(carry), carry

    _, _, carry = lax.while_loop(cond, body, (0, live(carry), carry))
    for (_, rsl, lsl), (_, acc) in zip(units, carry):
        o_ref[rsl, lsl] = acc.astype(o_ref.dtype)


def _sb_prompt_call(q, kt, vt):
    t, w = q.shape
    tq = min(t, 256)
    npair = 2 if w % (2 * LANES) == 0 else 1
    nq = 2 if t % (2 * tq) == 0 else 1
    wb = npair * LANES
    return pl.pallas_call(
        functools.partial(_sb_prompt_kernel, tq=tq, npair=npair, nq=nq),
        grid=(w // wb, t // (nq * tq)),
        in_specs=[pl.BlockSpec((nq * tq, wb), lambda p, i: (i, p)),
                  pl.BlockSpec((wb, t), lambda p, i: (p, 0)),
                  pl.BlockSpec((wb, t), lambda p, i: (p, 0))],
        out_specs=pl.BlockSpec((nq * tq, wb), lambda p, i: (i, p)),
        out_shape=jax.ShapeDtypeStruct((t, w), BF16),
        compiler_params=_cparams(("arbitrary", "arbitrary")),
        name="sb_prompt",
    )(q, kt, vt)


def _sb_sample_kernel(q_ref, kn_ref, vn_ref, kc_hbm, vc_hbm, o_ref, kbuf, vbuf, sem, *, heads, tq, tk, nblk):
    b = pl.program_id(0)
    hd = SB_HEAD_DIM

    def copies(j, slot):
        at = pl.multiple_of(j * tk, tk)
        return (pltpu.make_async_copy(kc_hbm.at[b, :, :, pl.ds(at, tk)], kbuf.at[slot], sem.at[0, slot]),
                pltpu.make_async_copy(vc_hbm.at[b, :, :, pl.ds(at, tk)], vbuf.at[slot], sem.at[1, slot]))

    def start(j, slot):
        for cp in copies(j, slot):
            cp.start()

    def wait(j, slot):
        for cp in copies(j, slot):
            cp.wait()

    slot_of = lambda j: lax.rem(nblk - 1 - j, 2)

    def stacked_block(qk_of, pv_of, suffix, visible, c_run, acc):
        q = q_ref[...]
        z = jnp.concatenate([qk_of(q[:, h * hd:(h + 1) * hd], h) for h in range(heads)], axis=0)
        w, c_run = _sb_weights(z, c_run, suffix, visible)
        acc = acc + jnp.concatenate([pv_of(w[h * tq:(h + 1) * tq], h) for h in range(heads)], axis=0)
        return c_run, acc

    start(nblk - 1, 0)
    rows = _stacked_row_ids(heads, tq, tq)
    cols = lax.broadcasted_iota(jnp.int32, (heads * tq, tq), 1)
    c_run, acc = stacked_block(lambda qh, h: _dot_nt(qh, kn_ref[:, h * hd:(h + 1) * hd]),
                               lambda wh, h: _dot(wh, vn_ref[:, h * hd:(h + 1) * hd]),
                               _suffix_matrix(tq), cols < rows,
                               jnp.zeros((heads * tq, 1), F32), jnp.zeros((heads * tq, hd), F32))
    suffix = _suffix_matrix(tk)

    def cond(st):
        return jnp.logical_and(st[0] >= 0, st[1] > 0)

    def body(st):
        j, _, c_run, acc = st
        slot = slot_of(j)
        wait(j, slot)

        @pl.when(j > 0)
        def _():
            start(j - 1, 1 - slot)

        c_run, acc = stacked_block(lambda qh, h: _dot(qh, kbuf[slot, h].astype(BF16)),
                                   lambda wh, h: _dot_nt(wh, vbuf[slot, h].astype(BF16)),
                                   suffix, None, c_run, acc)
        return j - 1, _sb_live(c_run), c_run, acc

    j_end, _, _, acc = lax.while_loop(cond, body, (nblk - 1, _sb_live(c_run), c_run, acc))

    @pl.when(j_end >= 0)
    def _():
        wait(j_end, slot_of(j_end))

    o_ref[...] = jnp.concatenate([acc[h * tq:(h + 1) * tq] for h in range(heads)], axis=1).astype(o_ref.dtype)


def _sb_sample_call(q, kn, vn, kc, vc, *, tq, heads):
    m, w = q.shape
    b, _, _, past = kc.shape
    tk = min(past, 256)
    new_spec = pl.BlockSpec((tq, w), lambda bi: (bi, 0))
    hbm_spec = pl.BlockSpec(memory_space=pl.ANY)
    return pl.pallas_call(
        functools.partial(_sb_sample_kernel, heads=heads, tq=tq, tk=tk, nblk=past // tk),
        grid=(b,),
        in_specs=[new_spec, new_spec, new_spec, hbm_spec, hbm_spec],
        out_specs=new_spec,
        out_shape=jax.ShapeDtypeStruct((m, w), BF16),
        scratch_shapes=[pltpu.VMEM((2, heads, SB_HEAD_DIM, tk), F32),
                        pltpu.VMEM((2, heads, SB_HEAD_DIM, tk), F32),
                        pltpu.SemaphoreType.DMA((2, 2))],
        compiler_params=_cparams(("arbitrary",)),
        name="sb_sample",
    )(q, kn, vn, kc, vc)


def _ssd_kernel(xbc_ref, z_ref, dt_ref, cprev_ref, h0_ref, cw_ref, cb_ref, dtb_ref, alog_ref, dsk_ref,
                nw_ref, o_ref, hout_ref, xpad_ref, st_ref, act_ref, y_ref, *, lc, cps, inner, npairs):
    ci = pl.program_id(1)
    eye = _eye(LANES, BF16)
    rows_step = cps * lc

    @pl.when(ci == 0)
    def _():
        xpad_ref[0:CONV_PAD_ROWS, :] = cprev_ref[0]
        for p in range(npairs):
            st_ref[p] = h0_ref[0, p].T

    @pl.when(ci > 0)
    def _():
        xpad_ref[0:CONV_PAD_ROWS, :] = xpad_ref[rows_step:rows_step + CONV_PAD_ROWS, :]

    xpad_ref[CONV_PAD_ROWS:CONV_PAD_ROWS + rows_step, :] = xbc_ref[...]
    act_ref[...] = _causal_conv_silu(xpad_ref[...], cw_ref, cb_ref)

    dt_all = _softplus(dt_ref[...] + dtb_ref[...])
    a_neg = -jnp.exp(alog_ref[...])
    r = lax.broadcasted_iota(jnp.int32, (lc, lc), 0)
    c = lax.broadcasted_iota(jnp.int32, (lc, lc), 1)
    causal = r >= c
    prefix = jnp.where(causal, 1.0, 0.0).astype(BF16)
    acausal_off = jnp.where(causal, 0.0, -1e30)
    low = lax.broadcasted_iota(jnp.int32, (lc, LANES), 1) < SSM_HEAD_DIM
    low1 = lax.broadcasted_iota(jnp.int32, (1, LANES), 1) < SSM_HEAD_DIM

    def pair_cols(arr, ha):
        return jnp.where(low, arr[:, ha:ha + 1], arr[:, ha + 1:ha + 2])

    pairs_per_group = npairs // SSM_GROUPS
    for cc in range(cps):
        rs = slice(cc * lc, (cc + 1) * lc)
        dtv = dt_all[rs]
        a_cs = _dot_exact_rhs(prefix, dtv * a_neg)
        a_cs_t = _transpose_exact(eye, a_cs)
        total = a_cs[lc - 1:lc, :]
        ea = jnp.exp(a_cs)
        to_end = jnp.exp(total - a_cs)
        chunk_decay = jnp.exp(total)
        for g in range(SSM_GROUPS):
            b_bf = act_ref[rs, inner + g * SSM_STATE:inner + (g + 1) * SSM_STATE].astype(BF16)
            c_off = inner + SSM_GROUPS * SSM_STATE
            c_bf = act_ref[rs, c_off + g * SSM_STATE:c_off + (g + 1) * SSM_STATE].astype(BF16)
            cb = _dot_nt(c_bf, b_bf)
            b_t = _dot_nt(eye, b_bf).astype(BF16)
            for kk in range(pairs_per_group):
                p = g * pairs_per_group + kk
                ha = 2 * p
                sl = slice(p * LANES, (p + 1) * LANES)
                x = act_ref[rs, sl]
                xdt = x * pair_cols(dtv, ha)
                xdt_bf = xdt.astype(BF16)
                ys = []
                for h in (ha, ha + 1):
                    seg = a_cs[:, h:h + 1] - a_cs_t[h:h + 1, :]
                    decay = jnp.exp(seg + acausal_off)
                    ys.append(_dot((cb * decay).astype(BF16), xdt_bf))
                st = st_ref[p]
                y_off = _dot(c_bf, st.astype(BF16)) * pair_cols(ea, ha)
                y_ref[rs, sl] = jnp.where(low, ys[0], ys[1]) + y_off + dsk_ref[:, sl] * x
                xw = (xdt * pair_cols(to_end, ha)).astype(BF16)
                dec = jnp.where(low1, chunk_decay[:, ha:ha + 1], chunk_decay[:, ha + 1:ha + 2])
                st_ref[p] = st * dec + _dot(b_t, xw)

    gy = y_ref[...] * _silu(z_ref[...].astype(F32))
    gw = inner // SSM_GROUPS
    for g in range(SSM_GROUPS):
        seg = gy[:, g * gw:(g + 1) * gw]
        ms = jnp.mean(seg * seg, axis=-1, keepdims=True)
        o_ref[:, g * gw:(g + 1) * gw] = (seg * lax.rsqrt(ms + LN_EPS) * nw_ref[:, g * gw:(g + 1) * gw]).astype(o_ref.dtype)

    @pl.when(ci == pl.num_programs(1) - 1)
    def _():
        for p in range(npairs):
            hout_ref[0, p] = st_ref[p].T


def _ssd_call(xbc, zg, dtr, conv_prev, h0, cw, cb, dtb, alog, dsk, nw, *, batch, lc):
    m = xbc.shape[0]
    conv_ch = cw.shape[1]
    inner = nw.shape[1]
    npairs = h0.shape[1]
    chunks = m // batch // lc
    cps = 4 if chunks % 4 == 0 else 1
    rs = cps * lc
    nc = chunks // cps
    row = lambda b, c: b * nc + c
    const = lambda b, c: (0, 0)
    return pl.pallas_call(
        functools.partial(_ssd_kernel, lc=lc, cps=cps, inner=inner, npairs=npairs),
        grid=(batch, nc),
        in_specs=[pl.BlockSpec((rs, conv_ch), lambda b, c: (row(b, c), 0)),
                  pl.BlockSpec((rs, inner), lambda b, c: (row(b, c), 0)),
                  pl.BlockSpec((rs, LANES), lambda b, c: (row(b, c), 0)),
                  pl.BlockSpec((1, CONV_PAD_ROWS, conv_ch), lambda b, c: (b, 0, 0)),
                  pl.BlockSpec((1, npairs, LANES, LANES), lambda b, c: (b, 0, 0, 0)),
                  pl.BlockSpec((CONV_WIDTH, conv_ch), const),
                  pl.BlockSpec((1, conv_ch), const),
                  pl.BlockSpec((1, LANES), const),
                  pl.BlockSpec((1, LANES), const),
                  pl.BlockSpec((1, inner), const),
                  pl.BlockSpec((1, inner), const)],
        out_specs=[pl.BlockSpec((rs, inner), lambda b, c: (row(b, c), 0)),
                   pl.BlockSpec((1, npairs, LANES, LANES), lambda b, c: (b, 0, 0, 0))],
        out_shape=[jax.ShapeDtypeStruct((m, inner), BF16),
                   jax.ShapeDtypeStruct(h0.shape, F32)],
        scratch_shapes=[pltpu.VMEM((CONV_PAD_ROWS + rs, conv_ch), F32),
                        pltpu.VMEM((npairs, LANES, LANES), F32),
                        pltpu.VMEM((rs, conv_ch), F32),
                        pltpu.VMEM((rs, inner), F32)],
        compiler_params=_cparams(("arbitrary", "arbitrary")),
        name="ssd",
    )(xbc, zg, dtr, conv_prev, h0, cw, cb, dtb, alog, dsk, nw)


def _merge_kernel(osb_ref, ossm_ref, gsb_ref, gssm_ref, x_ref, ga_ref, lg_ref, lb_ref,
                  wsb_ref, wssm_ref, wout_ref, o_ref, *, alpha, batch):
    merged = (jax.nn.sigmoid(gsb_ref[...].astype(F32)) * _dot(osb_ref[...], wsb_ref[...])
              + jax.nn.sigmoid(gssm_ref[...].astype(F32)) * _dot(ossm_ref[...], wssm_ref[...]))
    gate = _mod_rows(ga_ref, batch, x_ref.shape[0])
    res = alpha * x_ref[...] + gate * _dot(merged.astype(BF16), wout_ref[...])
    o_ref[...] = _ln(res) * lg_ref[...] + lb_ref[...]


def _resident(shape):
    return pl.BlockSpec(shape, lambda i: (0, 0), pipeline_mode=pl.Buffered(1))


def _merge_call(osb, ossm, zg, x, mod, ln_g, ln_b, wsb, wssm, wout, *, alpha, batch, row0):
    m, d = x.shape
    sbw = osb.shape[1]
    inner = ossm.shape[1]
    tm = _row_tile(m, batch, 1024)
    gcol = inner // d
    return pl.pallas_call(
        functools.partial(_merge_kernel, alpha=alpha, batch=batch),
        grid=(m // tm,),
        in_specs=[pl.BlockSpec((tm, sbw), lambda i: (i, 0)),
                  pl.BlockSpec((tm, inner), lambda i: (i, 0)),
                  pl.BlockSpec((tm, d), lambda i: (i, gcol)),
                  pl.BlockSpec((tm, d), lambda i: (i, gcol + 1)),
                  pl.BlockSpec((tm, d), lambda i: (i, 0)),
                  _mod_spec(d, "gate_a", batch, row0),
                  pl.BlockSpec((1, d), lambda i: (0, 0)),
                  pl.BlockSpec((1, d), lambda i: (0, 0)),
                  _resident(wsb.shape), _resident(wssm.shape), _resident(wout.shape)],
        out_specs=pl.BlockSpec((tm, d), lambda i: (i, 0)),
        out_shape=jax.ShapeDtypeStruct((m, d), F32),
        compiler_params=_cparams(("arbitrary",)),
        name="merge",
    )(osb, ossm, zg, zg, x, mod, ln_g, ln_b, wsb, wssm, wout)


def _ffn_kernel(x1_ref, sc_ref, sh_ref, gf_ref, lg_ref, lb_ref, wup_ref, wdn_ref, o_ref, *, alpha, nchunk, batch):
    x1 = x1_ref[...]
    m = x1.shape[0]
    h2 = (_ln(x1) * (1.0 + _mod_rows(sc_ref, batch, m)) + _mod_rows(sh_ref, batch, m)).astype(BF16)
    dff = wup_ref.shape[1]
    cw = dff // nchunk
    f = None
    for ck in range(nchunk):
        a = jnp.maximum(_dot(h2, wup_ref[:, ck * cw:(ck + 1) * cw]), 0.0)
        t = _dot((a * a).astype(BF16), wdn_ref[ck * cw:(ck + 1) * cw, :])
        f = t if f is None else f + t
    o_ref[...] = _ln(alpha * x1 + _mod_rows(gf_ref, batch, m) * f) * lg_ref[...] + lb_ref[...]


def _ffn_call(x1, mod, ln_g, ln_b, wup, wdn, *, alpha, batch, row0):
    m, d = x1.shape
    tm = _row_tile(m, batch, 1024)
    return pl.pallas_call(
        functools.partial(_ffn_kernel, alpha=alpha, nchunk=4, batch=batch),
        grid=(m // tm,),
        in_specs=[pl.BlockSpec((tm, d), lambda i: (i, 0)),
                  _mod_spec(d, "scale_f", batch, row0),
                  _mod_spec(d, "shift_f", batch, row0),
                  _mod_spec(d, "gate_f", batch, row0),
                  pl.BlockSpec((1, d), lambda i: (0, 0)),
                  pl.BlockSpec((1, d), lambda i: (0, 0)),
                  _resident(wup.shape), _resident(wdn.shape)],
        out_specs=pl.BlockSpec((tm, d), lambda i: (i, 0)),
        out_shape=jax.ShapeDtypeStruct((m, d), F32),
        compiler_params=_cparams(("arbitrary",)),
        name="ffn",
    )(x1, mod, mod, mod, ln_g, ln_b, wup, wdn)


def _layer(x, mod, cache_k, cache_v, conv_prev, ssm_prev, wts, *, batch, row0, alpha):
    m, d = x.shape
    t = m // batch
    h = _ln_mod_call(x, mod, batch=batch, row0=row0)
    def proj(cols, outs, name, **kw):
        return _matmul_call(h, wts["w_in_t"], outs, name, row0=wts[cols][0], nrows=wts[cols][1], **kw)

    (q,) = proj("q_cols", [(BF16, SB_HEAD_DIM ** -0.5 * LOG2E)], "proj_q")
    kv_t = cache_k is None
    k32, kb = proj("k_cols", [(F32, 1.0), (BF16, 1.0)], "proj_k", feature_major=kv_t)
    v32, vb = proj("v_cols", [(F32, 1.0), (BF16, 1.0)], "proj_v", feature_major=kv_t)
    if kv_t:
        k32, v32 = k32.T, v32.T
    conv_ch = wts["conv_w"].shape[1]
    cprev = jnp.pad(conv_prev, ((0, 0), (CONV_PAD_ROWS - (CONV_WIDTH - 1), 0), (0, 0)))
    (xbc,) = proj("xbc_cols", [(F32, 1.0)], "proj_xbc")
    conv_new = xbc.reshape(batch, t, conv_ch)[:, t - (CONV_WIDTH - 1):]
    (zg,) = _matmul_call(h, wts["w_zg"], [(BF16, 1.0)], "proj_zg")
    (dtr,) = _matmul_call(h, wts["w_dt"], [(F32, 1.0)], "proj_dt")
    if cache_k is None:
        o_sb = _sb_prompt_call(q, kb, vb)
    else:
        o_sb = _sb_sample_call(q, kb, vb, cache_k, cache_v, tq=t, heads=q.shape[1] // SB_HEAD_DIM)
    npairs = ssm_prev.shape[1] // 2
    h0 = ssm_prev.reshape(batch, npairs, LANES, SSM_STATE)
    o_ssm, h_fin = _ssd_call(xbc, zg, dtr, cprev, h0, wts["conv_w"], wts["conv_b"], wts["dt_bias"], wts["a_log"],
                             wts["d_skip"], wts["ssm_norm_w"], batch=batch, lc=min(t, 128))
    x1 = _merge_call(o_sb, o_ssm, zg, x, mod, wts["ln_attn_g"], wts["ln_attn_b"],
                     wts["w_branch_sb"], wts["w_branch_ssm"], wts["w_out"], alpha=alpha, batch=batch, row0=row0)
    y = _ffn_call(x1, mod, wts["ln_ffn_g"], wts["ln_ffn_b"], wts["w_up"], wts["w_down"],
                  alpha=alpha, batch=batch, row0=row0)
    return y, k32, v32, conv_new, h_fin.reshape(ssm_prev.shape)


def _prep_weights(l, w_in, conv_w, conv_b, dt_bias, a_log, d_skip, ssm_norm_w, w_branch_sb, w_branch_ssm,
                  w_out, ln_attn_g, ln_attn_b, w_up, w_down, ln_ffn_g, ln_ffn_b, sbw):
    d = w_in.shape[1]
    inner = ssm_norm_w.shape[1]
    conv_ch = conv_w.shape[2]
    nh = a_log.shape[1]
    o_z = 3 * sbw
    o_xbc = o_z + inner
    o_dt = o_xbc + conv_ch
    o_g = o_dt + nh
    wit = jnp.transpose(w_in[l])
    pad_h = lambda v: jnp.pad(v, (0, LANES - nh))[None, :]
    return {
        "w_in_t": wit, "q_cols": (0, sbw), "k_cols": (sbw, sbw), "v_cols": (2 * sbw, sbw),
        "xbc_cols": (o_xbc, conv_ch),
        "w_zg": jnp.concatenate([wit[o_z:o_xbc], wit[o_g:]], axis=0),
        "w_dt": jnp.pad(wit[o_dt:o_g], ((0, LANES - nh), (0, 0))),
        "conv_w": conv_w[l], "conv_b": conv_b[l][None, :],
        "dt_bias": pad_h(dt_bias[l]), "a_log": pad_h(a_log[l]),
        "d_skip": jnp.repeat(d_skip[l], SSM_HEAD_DIM)[None, :],
        "ssm_norm_w": ssm_norm_w[l][None, :],
        "w_branch_sb": w_branch_sb[l].astype(BF16), "w_branch_ssm": w_branch_ssm[l].astype(BF16),
        "w_out": w_out[l].astype(BF16),
        "ln_attn_g": ln_attn_g[l][None, :], "ln_attn_b": ln_attn_b[l][None, :],
        "w_up": w_up[l].astype(BF16), "w_down": w_down[l].astype(BF16),
        "ln_ffn_g": ln_ffn_g[l][None, :], "ln_ffn_b": ln_ffn_b[l][None, :],
    }


def kernel(x_prompt, x_sample, cache_sb_k, cache_sb_v, state_conv, state_ssm, c_prompt, c_sample, w_ada, b_ada, w_in, conv_w, conv_b, dt_bias, a_log, d_skip, ssm_norm_w, w_branch_sb, w_branch_ssm, w_out, ln_attn_g, ln_attn_b, w_up, w_down, ln_ffn_g, ln_ffn_b):
    depth = w_ada.shape[0]
    alpha = (2 * depth) ** 0.25
    n_p, t_p, d = x_prompt.shape
    n_s, t_s, _ = x_sample.shape
    past = cache_sb_k.shape[2]
    sbw = cache_sb_k.shape[3] * cache_sb_k.shape[4]
    conv_ch = conv_w.shape[2]
    y_p = x_prompt.reshape(n_p * t_p, d)
    y_s = x_sample.reshape(n_s * t_s, d)
    assert n_p == 1
    row0_p = -(-n_s // SUBLANES) * SUBLANES
    c_all = jnp.concatenate([jnp.pad(c_sample, ((0, row0_p - n_s), (0, 0))),
                             jnp.pad(c_prompt, ((0, SUBLANES - n_p), (0, 0)))], axis=0)
    new_p, new_s = [], []
    for l in range(depth):
        wts = _prep_weights(l, w_in, conv_w, conv_b, dt_bias, a_log, d_skip, ssm_norm_w, w_branch_sb,
                            w_branch_ssm, w_out, ln_attn_g, ln_attn_b, w_up, w_down, ln_ffn_g, ln_ffn_b, sbw)
        mod = _mod_call(c_all, w_ada[l], b_ada[l][None, :])
        conv0 = jnp.zeros((n_p, CONV_WIDTH - 1, conv_ch), F32)
        ssm0 = jnp.zeros((n_p,) + state_ssm.shape[2:], F32)
        y_p, k_p, v_p, cv_p, s_p = _layer(y_p, mod, None, None, conv0, ssm0, wts,
                                          batch=n_p, row0=row0_p, alpha=alpha)
        y_s, k_s, v_s, cv_s, s_s = _layer(y_s, mod, jnp.transpose(cache_sb_k[l], (0, 2, 3, 1)),
                                          jnp.transpose(cache_sb_v[l], (0, 2, 3, 1)), state_conv[l], state_ssm[l],
                                          wts, batch=n_s, row0=0, alpha=alpha)
        hshape = cache_sb_k.shape[3:]
        new_p.append((k_p.reshape((n_p, t_p) + hshape), v_p.reshape((n_p, t_p) + hshape), cv_p, s_p))
        new_s.append((k_s.reshape((n_s, t_s) + hshape), v_s.reshape((n_s, t_s) + hshape), cv_s, s_s))
    stack = lambda lst, i: jnp.stack([e[i] for e in lst])
    return (y_p.reshape(n_p, t_p, d), y_s.reshape(n_s, t_s, d),
            stack(new_p, 0), stack(new_p, 1), stack(new_p, 2), stack(new_p, 3),
            stack(new_s, 0), stack(new_s, 1), stack(new_s, 2), stack(new_s, 3))
```
